```python
import math
import jax, jax.numpy as jnp
from jax import lax
import numpy as np

D_MODEL = 4096
BATCH = 2
SEQ = 4096
DEPTH = 2

N_META = 16
Q_BLOCK = 128
LEAD = Q_BLOCK
N_PAD = LEAD - N_META
RMS_EPS = 1e-6
NEG_INF = -1e30

MLA_V = 128
MLA_HEADS = (3 * D_MODEL // 8) // MLA_V
MLA_NOPE = 128
MLA_ROPE = 64
MLA_Q_LORA = 1536
MLA_KV_LORA = 512
ROPE_THETA = 10000.0

GLA_DV = 256
GLA_DK = 128
GLA_HEADS = (D_MODEL // 4) // GLA_DV
GLA_GATE_RANK = 16
GLA_TAU = 16.0
GLA_CHUNK = 64

FOX_DH = 128
FOX_HEADS = (3 * D_MODEL // 8) // FOX_DH

W_MLA = MLA_HEADS * MLA_V
W_GLA = GLA_HEADS * GLA_DV
W_FOX = FOX_HEADS * FOX_DH
D_MIX = W_MLA + W_GLA + W_FOX

IN_SIZES = (
    MLA_Q_LORA, MLA_KV_LORA, MLA_ROPE,
    GLA_HEADS * GLA_DK, GLA_HEADS * GLA_DK, W_GLA,
    GLA_GATE_RANK, W_GLA,
    W_FOX, W_FOX, W_FOX, FOX_HEADS,
)
D_IN = sum(IN_SIZES)

D_FF = 256 * (-(-8 * D_MODEL // (3 * 256)))
CONV_W = 3

kernel_name = "hybrid_mla_gla_fox_convffn"


def rms_norm(x, g):
    xf = x.astype(jnp.float32)
    y = xf * lax.rsqrt(jnp.mean(xf * xf, axis=-1, keepdims=True) + RMS_EPS)
    return (y * g.astype(jnp.float32)).astype(x.dtype)


def head_rms_norm(o, gain, n_heads):
    b, l = o.shape[:2]
    o = o.reshape(b, l, n_heads, -1)
    return rms_norm(o, gain.reshape(n_heads, -1)).reshape(b, l, -1)


def split_cols(x, sizes):
    out, start = [], 0
    for s in sizes:
        out.append(x[..., start:start + s])
        start += s
    return out


def rope(x, cos, sin):
    half = x.shape[-1] // 2
    x1, x2 = x[..., :half], x[..., half:]
    xf1, xf2 = x1.astype(jnp.float32), x2.astype(jnp.float32)
    return jnp.concatenate([xf1 * cos - xf2 * sin, xf2 * cos + xf1 * sin], axis=-1).astype(x.dtype)


def causal_block_attention(q, k, v, key_valid, scale, log_f_cum=None):
    L = q.shape[1]
    outs = []
    for i in range(L // Q_BLOCK):
        q0 = i * Q_BLOCK
        k_end = q0 + Q_BLOCK
        qb = q[:, q0:k_end].astype(jnp.float32)
        kb = k[:, :k_end].astype(jnp.float32)
        vb = v[:, :k_end].astype(jnp.float32)
        s = jnp.einsum('bqhd,bkhd->bhqk', qb, kb) * scale
        if log_f_cum is not None:
            c_q = jnp.transpose(log_f_cum[:, q0:k_end], (0, 2, 1))[..., :, None]
            c_k = jnp.transpose(log_f_cum[:, :k_end], (0, 2, 1))[..., None, :]
            s = s + (c_q - c_k)
        q_pos = q0 + jnp.arange(Q_BLOCK)
        k_pos = jnp.arange(k_end)
        mask = (k_pos[None, :] <= q_pos[:, None]) & key_valid[None, :k_end]
        s = jnp.where(mask, s, NEG_INF)
        p = jax.nn.softmax(s, axis=-1)
        outs.append(jnp.einsum('bhqk,bkhd->bqhd', p, vb).astype(v.dtype))
    return jnp.concatenate(outs, axis=1)


def gla_chunked(q, k, v, log_a):
    B, L, H, DK = q.shape
    DV = v.shape[-1]
    C = GLA_CHUNK
    N = L // C
    q = q.reshape(B, N, C, H, DK)
    k = k.reshape(B, N, C, H, DK)
    v = v.reshape(B, N, C, H, DV)
    log_a = log_a.reshape(B, N, C, H, DK)
    b = jnp.cumsum(log_a, axis=2)
    b_last = b[:, :, -1:]
    q_dec = q * jnp.exp(b)
    k_dec = k * jnp.exp(-b)
    causal = jnp.tril(jnp.ones((C, C), dtype=bool))
    a = jnp.einsum('bnthk,bnshk->bnhts', q_dec, k_dec)
    a = jnp.where(causal, a, 0.0)
    o_intra = jnp.einsum('bnhts,bnshv->bnthv', a, v)
    k_state = k * jnp.exp(b_last - b)
    decay = jnp.exp(b_last[:, :, 0])

    def step(state, inp):
        q_c, k_c, v_c, d_c = inp
        o_c = jnp.einsum('bthk,bhkv->bthv', q_c, state)
        state = state * d_c[..., None] + jnp.einsum('bthk,bthv->bhkv', k_c, v_c)
        return state, o_c

    xs = (jnp.moveaxis(q_dec, 1, 0), jnp.moveaxis(k_state, 1, 0),
          jnp.moveaxis(v, 1, 0), jnp.moveaxis(decay, 1, 0))
    s0 = jnp.zeros((B, H, DK, DV), jnp.float32)
    _, o_inter = lax.scan(step, s0, xs)
    o = o_intra + jnp.moveaxis(o_inter, 0, 1)
    return o.reshape(B, L, H, DV)


def hybrid_mixer(hn, valid, cos, sin, w_in, mla_q_norm, mla_w_uq, mla_kv_norm, mla_w_ukv,
                 gla_w_gate2, gla_b_gate, fox_b_f, out_norm_mla, out_norm_gla,
                 out_norm_fox, w_out):
    B, L, _ = hn.shape
    proj = hn @ w_in
    (c_q, c_kv, k_rope, g_q, g_k, g_v, g_z, g_r,
     f_q, f_k, f_v, f_z) = split_cols(proj, IN_SIZES)

    q = (rms_norm(c_q, mla_q_norm) @ mla_w_uq).reshape(B, L, MLA_HEADS, MLA_NOPE + MLA_ROPE)
    q_nope, q_pe = q[..., :MLA_NOPE], q[..., MLA_NOPE:]
    q_pe = rope(q_pe, cos[:, None, :], sin[:, None, :])
    kv = (rms_norm(c_kv, mla_kv_norm) @ mla_w_ukv).reshape(B, L, MLA_HEADS, MLA_NOPE + MLA_V)
    k_nope, v_m = kv[..., :MLA_NOPE], kv[..., MLA_NOPE:]
    k_pe = rope(k_rope, cos, sin)
    q_m = jnp.concatenate([q_nope, q_pe], axis=-1)
    k_m = jnp.concatenate(
        [k_nope, jnp.broadcast_to(k_pe[:, :, None, :], (B, L, MLA_HEADS, MLA_ROPE))], axis=-1)
    o_mla = causal_block_attention(q_m, k_m, v_m, valid, (MLA_NOPE + MLA_ROPE) ** -0.5)
    o_mla = head_rms_norm(o_mla, out_norm_mla, MLA_HEADS)

    vf = valid.astype(jnp.float32)[None, :, None, None]
    gq = g_q.astype(jnp.float32).reshape(B, L, GLA_HEADS, GLA_DK) * (GLA_DK ** -0.5)
    gk = g_k.astype(jnp.float32).reshape(B, L, GLA_HEADS, GLA_DK) * vf
    gv = g_v.astype(jnp.float32).reshape(B, L, GLA_HEADS, GLA_DV)
    gate_logit = (g_z @ gla_w_gate2 + gla_b_gate).astype(jnp.float32)
    log_a = (jax.nn.log_sigmoid(gate_logit) / GLA_TAU).reshape(B, L, GLA_HEADS, GLA_DK) * vf
    o_gla = gla_chunked(gq, gk, gv, log_a).astype(hn.dtype)
    o_gla = head_rms_norm(o_gla, out_norm_gla, GLA_HEADS) * jax.nn.silu(g_r)

    fq = f_q.reshape(B, L, FOX_HEADS, FOX_DH)
    fk = f_k.reshape(B, L, FOX_HEADS, FOX_DH)
    fv = f_v.reshape(B, L, FOX_HEADS, FOX_DH)
    log_f = jax.nn.log_sigmoid((f_z + fox_b_f).astype(jnp.float32)) \
        * valid.astype(jnp.float32)[None, :, None]
    c = jnp.cumsum(log_f, axis=1)
    o_fox = causal_block_attention(fq, fk, fv, valid, FOX_DH ** -0.5, log_f_cum=c)
    o_fox = head_rms_norm(o_fox, out_norm_fox, FOX_HEADS)

    return jnp.concatenate([o_mla, o_gla, o_fox], axis=-1) @ w_out


def conv_ffn(hn, valid, w_up, conv_w, conv_b, w_down):
    L = hn.shape[1]
    u = (hn @ w_up) * valid.astype(hn.dtype)[None, :, None]
    u_pad = jnp.pad(u, ((0, 0), (CONV_W - 1, 0), (0, 0)))
    cv = conv_b + sum(conv_w[j] * u_pad[:, j:j + L] for j in range(CONV_W))
    gate, val = cv[..., :D_FF], cv[..., D_FF:]
    return (jax.nn.silu(gate) * val) @ w_down


def setup_inputs(seed: int = 0) -> dict:
    key = jax.random.key(seed)
    ks = jax.random.split(key, 24)
    f32 = jnp.float32
    nrm = lambda k, shape, s: jax.random.normal(k, shape, f32) * s
    gain = lambda k, shape: 1.0 + 0.02 * jax.random.normal(k, shape, f32)
    return {
        "x": nrm(ks[0], (BATCH, SEQ, D_MODEL), 1.0),
        "meta_tokens": nrm(ks[1], (N_META, D_MODEL), 1.0),
        "attn_norm": gain(ks[2], (DEPTH, D_MODEL)),
        "w_in": nrm(ks[3], (DEPTH, D_MODEL, D_IN), D_MODEL ** -0.5),
        "mla_q_norm": gain(ks[4], (DEPTH, MLA_Q_LORA)),
        "mla_w_uq": nrm(ks[5], (DEPTH, MLA_Q_LORA, MLA_HEADS * (MLA_NOPE + MLA_ROPE)), MLA_Q_LORA ** -0.5),
        "mla_kv_norm": gain(ks[6], (DEPTH, MLA_KV_LORA)),
        "mla_w_ukv": nrm(ks[7], (DEPTH, MLA_KV_LORA, MLA_HEADS * (MLA_NOPE + MLA_V)), MLA_KV_LORA ** -0.5),
        "gla_w_gate2": nrm(ks[8], (DEPTH, GLA_GATE_RANK, GLA_HEADS * GLA_DK), GLA_GATE_RANK ** -0.5),
        "gla_b_gate": nrm(ks[9], (DEPTH, GLA_HEADS * GLA_DK), 0.1),
        "fox_b_f": nrm(ks[10], (DEPTH, FOX_HEADS), 0.1),
        "out_norm_mla": gain(ks[11], (DEPTH, W_MLA)),
        "out_norm_gla": gain(ks[12], (DEPTH, W_GLA)),
        "out_norm_fox": gain(ks[13], (DEPTH, W_FOX)),
        "w_out": nrm(ks[14], (DEPTH, D_MIX, D_MODEL), D_MIX ** -0.5),
        "ffn_norm": gain(ks[15], (DEPTH, D_MODEL)),
        "ffn_w_up": nrm(ks[16], (DEPTH, D_MODEL, 2 * D_FF), D_MODEL ** -0.5),
        "ffn_conv_w": nrm(ks[17], (DEPTH, CONV_W, 2 * D_FF), CONV_W ** -0.5),
        "ffn_conv_b": nrm(ks[18], (DEPTH, 2 * D_FF), 0.02),
        "ffn_w_down": nrm(ks[19], (DEPTH, D_FF, D_MODEL), D_FF ** -0.5),
        "final_norm": gain(ks[20], (D_MODEL,)),
    }


def reference(x, meta_tokens, attn_norm, w_in, mla_q_norm, mla_w_uq, mla_kv_norm, mla_w_ukv,
              gla_w_gate2, gla_b_gate, fox_b_f, out_norm_mla, out_norm_gla, out_norm_fox,
              w_out, ffn_norm, ffn_w_up, ffn_conv_w, ffn_conv_b, ffn_w_down, final_norm):
    B = x.shape[0]
    meta = jnp.broadcast_to(meta_tokens.astype(x.dtype)[None], (B, N_META, D_MODEL))
    h = jnp.concatenate([jnp.zeros((B, N_PAD, D_MODEL), x.dtype), meta, x], axis=1)
    L = h.shape[1]
    idx = jnp.arange(L)
    valid = idx >= N_PAD
    vmask = valid.astype(x.dtype)[None, :, None]
    pos = jnp.maximum(idx - N_PAD, 0).astype(jnp.float32)
    inv_freq = 1.0 / (ROPE_THETA ** (jnp.arange(0, MLA_ROPE, 2, dtype=jnp.float32) / MLA_ROPE))
    ang = pos[:, None] * inv_freq[None, :]
    cos, sin = jnp.cos(ang), jnp.sin(ang)

    for layer in range(DEPTH):
        hn = rms_norm(h, attn_norm[layer])
        h = h + hybrid_mixer(hn, valid, cos, sin, w_in[layer], mla_q_norm[layer], mla_w_uq[layer],
                             mla_kv_norm[layer], mla_w_ukv[layer], gla_w_gate2[layer],
                             gla_b_gate[layer], fox_b_f[layer], out_norm_mla[layer],
                             out_norm_gla[layer], out_norm_fox[layer], w_out[layer]) * vmask
        hn = rms_norm(h, ffn_norm[layer])
        h = h + conv_ffn(hn, valid, ffn_w_up[layer], ffn_conv_w[layer], ffn_conv_b[layer],
                         ffn_w_down[layer]) * vmask

    return rms_norm(h, final_norm)[:, LEAD:]
```

```python
import functools

import jax
import jax.numpy as jnp
import numpy as np
from jax import lax
from jax.experimental import pallas as pl
from jax.experimental.pallas import tpu as pltpu

F32 = jnp.float32
BF16 = jnp.bfloat16

N_META = 16
LEAD = 128
N_PAD = LEAD - N_META
RMS_EPS = 1e-6
NEG_INF = -1e30

MLA_V = 128
MLA_HEADS = 12
MLA_NOPE = 128
MLA_ROPE = 64
MLA_Q_LORA = 1536
MLA_KV_LORA = 512
ROPE_THETA = 10000.0

GLA_DV = 256
GLA_DK = 128
GLA_HEADS = 4
GLA_GATE_RANK = 16
GLA_TAU = 16.0
GLA_CHUNK = 64

FOX_DH = 128
FOX_HEADS = 12

W_MLA = MLA_HEADS * MLA_V
W_GLA = GLA_HEADS * GLA_DV
W_FOX = FOX_HEADS * FOX_DH
CONV_W = 3

LANES = 128
VMEM_LIMIT_BYTES = 56 * 1024 * 1024

HALF_ROPE = MLA_ROPE // 2
COL_CQ = 0
COL_CKV = COL_CQ + MLA_Q_LORA
COL_GV = COL_CKV + MLA_KV_LORA
COL_GR = COL_GV + W_GLA
COL_GQ = COL_GR + W_GLA
COL_GK = COL_GQ + GLA_HEADS * GLA_DK
COL_FQ = COL_GK + GLA_HEADS * GLA_DK
COL_FK = COL_FQ + W_FOX
COL_FV = COL_FK + W_FOX
COL_KR = COL_FV + W_FOX
N_PROJ = COL_KR + LANES
GATE_FZ = GLA_GATE_RANK
ROPE_ONE_LANE = HALF_ROPE


def _pick(n, target, mult):
    best = None
    for d in range(mult, min(n, target) + 1, mult):
        if n % d == 0:
            best = d
    assert best is not None, (n, target, mult)
    return best


def _params(*sem):
    return pltpu.CompilerParams(dimension_semantics=sem, vmem_limit_bytes=VMEM_LIMIT_BYTES)


def _log_sigmoid(x):
    return jnp.minimum(x, 0.0) - jnp.log1p(jnp.exp(-jnp.abs(x)))


def _silu(x):
    return x / (1.0 + jnp.exp(-x))


def _rmsnorm_body(x_ref, g_ref, o_ref):
    x = x_ref[...].astype(F32)
    y = x * lax.rsqrt(jnp.mean(x * x, axis=-1, keepdims=True) + RMS_EPS)
    o_ref[...] = (y * g_ref[...]).astype(o_ref.dtype)


def rmsnorm(x, gain, *, col0=0, width=None, out_dtype=BF16, rows=256):
    m = x.shape[0]
    width = x.shape[1] if width is None else width
    assert col0 % width == 0
    tr = _pick(m, rows, 16)
    return pl.pallas_call(
        _rmsnorm_body,
        grid=(m // tr,),
        in_specs=[pl.BlockSpec((tr, width), lambda i: (i, col0 // width)),
                  pl.BlockSpec((1, width), lambda i: (0, 0))],
        out_specs=pl.BlockSpec((tr, width), lambda i: (i, 0)),
        out_shape=jax.ShapeDtypeStruct((m, width), out_dtype),
        compiler_params=_params("parallel"),
        name="rmsnorm",
    )(x, gain.reshape(1, width).astype(F32))


def _final_norm_body(x_ref, g_ref, o_ref):
    x = x_ref[...]
    y = x * lax.rsqrt(jnp.mean(x * x, axis=-1, keepdims=True) + RMS_EPS)
    o_ref[0] = y * g_ref[...]


def final_rmsnorm(h, gain, batch, seq_total):
    d = h.shape[1]
    tr = LEAD
    nblk = seq_total // tr
    return pl.pallas_call(
        _final_norm_body,
        grid=(batch, nblk - 1),
        in_specs=[pl.BlockSpec((tr, d), lambda b, i: (b * nblk + i + 1, 0)),
                  pl.BlockSpec((1, d), lambda b, i: (0, 0))],
        out_specs=pl.BlockSpec((1, tr, d), lambda b, i: (b, i, 0)),
        out_shape=jax.ShapeDtypeStruct((batch, seq_total - LEAD, d), F32),
        compiler_params=_params("parallel", "parallel"),
        name="final_norm",
    )(h, gain.reshape(1, d).astype(F32))


def _mm_body(*refs, nk, residual, seq_total, tm):
    if residual:
        a_ref, w_ref, r_ref, o_ref = refs[:4]
        rest = refs[4:]
    else:
        a_ref, w_ref, o_ref = refs[:3]
        rest = refs[3:]

    def finish(acc):
        if residual:
            start = (pl.program_id(0) % (seq_total // tm)) * tm
            pos = start + lax.broadcasted_iota(jnp.int32, acc.shape, 0)
            acc = r_ref[...] + jnp.where(pos >= N_PAD, acc, 0.0)
        o_ref[...] = acc.astype(o_ref.dtype)

    part = jnp.dot(a_ref[...], w_ref[...], preferred_element_type=F32)
    if nk == 1:
        finish(part)
        return
    acc_ref, = rest
    k = pl.program_id(2)

    @pl.when(k == 0)
    def _():
        acc_ref[...] = part

    @pl.when(jnp.logical_and(k > 0, k < nk - 1))
    def _():
        acc_ref[...] += part

    @pl.when(k == nk - 1)
    def _():
        finish(acc_ref[...] + part)


def matmul(a, w, *, tm, tn, tk=None, out_dtype=BF16, residual=None, seq_total=None):
    m, kdim = a.shape
    n = w.shape[1]
    tk = kdim if tk is None else tk
    assert m % tm == 0 and n % tn == 0 and kdim % tk == 0
    nk = kdim // tk
    assert nk == 1 or nk >= 2
    has_res = residual is not None
    if has_res:
        assert seq_total % tm == 0
    in_specs = [pl.BlockSpec((tm, tk), lambda i, j, k: (i, k)),
                pl.BlockSpec((tk, tn), lambda i, j, k: (k, j))]
    args = [a, w]
    if has_res:
        in_specs.append(pl.BlockSpec((tm, tn), lambda i, j, k: (i, j)))
        args.append(residual)
    scratch = [pltpu.VMEM((tm, tn), F32)] if nk > 1 else []
    return pl.pallas_call(
        functools.partial(_mm_body, nk=nk, residual=has_res, seq_total=seq_total, tm=tm),
        grid=(m // tm, n // tn, nk),
        in_specs=in_specs,
        out_specs=pl.BlockSpec((tm, tn), lambda i, j, k: (i, j)),
        out_shape=jax.ShapeDtypeStruct((m, n), out_dtype),
        scratch_shapes=scratch,
        compiler_params=_params("parallel", "parallel", "arbitrary"),
        name="matmul",
    )(*args)


def _attention_core(qs, ks, v_at, gain, o_ref, *, seq_total, tq):
    nq = seq_total // tq
    nt = (((1,), (1,)), ((), ()))

    def q_block(qi, _):
        q0 = pl.multiple_of(qi * tq, tq)
        q = qs[pl.ds(q0, tq), :]

        def kv_step(kj, carry, diagonal):
            m, l, acc = carry
            k0 = pl.multiple_of(kj * tq, tq)
            k = ks[pl.ds(k0, tq), :]
            s = lax.dot_general(q, k, nt, preferred_element_type=F32)
            if diagonal:
                row = lax.broadcasted_iota(jnp.int32, s.shape, 0)
                col = lax.broadcasted_iota(jnp.int32, s.shape, 1)
                s = jnp.where(col <= row, s, NEG_INF)
            m_new = jnp.maximum(m, jnp.max(s, axis=-1, keepdims=True))
            alpha = jnp.exp(m - m_new)
            p = jnp.exp(s - m_new)
            l = alpha * l + jnp.sum(p, axis=-1, keepdims=True)
            acc = alpha * acc + jnp.dot(p.astype(BF16), v_at(k0, tq),
                                        preferred_element_type=F32)
            return m_new, l, acc

        init = (jnp.full((tq, 1), NEG_INF, F32), jnp.zeros((tq, 1), F32),
                jnp.zeros((tq, LANES), F32))
        carry = lax.fori_loop(0, qi, functools.partial(kv_step, diagonal=False), init)
        _, l, acc = kv_step(qi, carry, True)
        o = acc / l
        o = o * lax.rsqrt(jnp.mean(o * o, axis=-1, keepdims=True) + RMS_EPS) * gain
        o_ref[pl.ds(q0, tq), :] = o.astype(o_ref.dtype)
        return 0

    lax.fori_loop(0, nq, q_block, 0)


def _key_mask_lane(shape, row0, lane):
    row = row0 + lax.broadcasted_iota(jnp.int32, shape, 0)
    return jnp.where(row < N_PAD, NEG_INF, 0.0)


def _mla_attn_body(q_ref, kv_ref, kr_ref, cos_ref, sin_ref, g_ref, o_ref, qs, ks,
                   *, seq_total, tq):
    scale = (MLA_NOPE + MLA_ROPE) ** -0.5
    nq = seq_total // tq

    def prep(i, _):
        r0 = pl.multiple_of(i * tq, tq)
        rows = pl.ds(r0, tq)
        cos = cos_ref[rows, :]
        sin = sin_ref[rows, :]
        lane = lax.broadcasted_iota(jnp.int32, (tq, LANES), 1)
        one_lane = lane == ROPE_ONE_LANE
        qn = q_ref[rows, pl.ds(0, LANES)].astype(F32) * scale
        qp = q_ref[rows, pl.ds(LANES, LANES)].astype(F32)
        qp = (qp * cos + pltpu.roll(qp, LANES // 2, 1) * sin) * scale
        qp = jnp.where(one_lane, 1.0, qp)
        qs[rows, pl.ds(0, LANES)] = qn.astype(BF16)
        qs[rows, pl.ds(LANES, LANES)] = qp.astype(BF16)
        kp = kr_ref[rows, :].astype(F32)
        kp = kp * cos + pltpu.roll(kp, LANES // 2, 1) * sin
        kp = jnp.where(one_lane, _key_mask_lane((tq, LANES), r0, None), kp)
        ks[rows, pl.ds(0, LANES)] = kv_ref[rows, pl.ds(0, LANES)]
        ks[rows, pl.ds(LANES, LANES)] = kp.astype(BF16)
        return 0

    lax.fori_loop(0, nq, prep, 0)
    v_at = lambda k0, n: kv_ref[pl.ds(k0, n), pl.ds(LANES, LANES)]
    _attention_core(qs, ks, v_at, g_ref[...], o_ref, seq_total=seq_total, tq=tq)


def mla_attention(q_raw, kv, proj, cos_t, sin_t, gain, *, batch, seq_total, tq):
    t = q_raw.shape[0]
    big = lambda w: pl.BlockSpec((seq_total, w), lambda b, h: (b, h))
    return pl.pallas_call(
        functools.partial(_mla_attn_body, seq_total=seq_total, tq=tq),
        grid=(batch, MLA_HEADS),
        in_specs=[big(2 * LANES), big(2 * LANES),
                  pl.BlockSpec((seq_total, LANES), lambda b, h: (b, COL_KR // LANES)),
                  pl.BlockSpec((seq_total, LANES), lambda b, h: (0, 0)),
                  pl.BlockSpec((seq_total, LANES), lambda b, h: (0, 0)),
                  pl.BlockSpec((1, LANES), lambda b, h: (0, h))],
        out_specs=big(LANES),
        out_shape=jax.ShapeDtypeStruct((t, W_MLA), BF16),
        scratch_shapes=[pltpu.VMEM((seq_total, 2 * LANES), BF16),
                        pltpu.VMEM((seq_total, 2 * LANES), BF16)],
        compiler_params=_params("parallel", "parallel"),
        name="mla_attention",
    )(q_raw, kv, proj, cos_t, sin_t, gain.reshape(1, W_MLA).astype(F32))


def _fox_attn_body(q_ref, k_ref, v_ref, aq_ref, ak_ref, g_ref, o_ref, qs, ks,
                   *, seq_total, tq):
    scale = FOX_DH ** -0.5
    nq = seq_total // tq

    def prep(i, _):
        rows = pl.ds(pl.multiple_of(i * tq, tq), tq)
        qs[rows, pl.ds(0, LANES)] = (q_ref[rows, :].astype(F32) * scale).astype(BF16)
        qs[rows, pl.ds(LANES, LANES)] = aq_ref[rows, :]
        ks[rows, pl.ds(0, LANES)] = k_ref[rows, :]
        ks[rows, pl.ds(LANES, LANES)] = ak_ref[rows, :]
        return 0

    lax.fori_loop(0, nq, prep, 0)
    v_at = lambda k0, n: v_ref[pl.ds(k0, n), :]
    _attention_core(qs, ks, v_at, g_ref[...], o_ref, seq_total=seq_total, tq=tq)


def fox_attention(proj, aug_q, aug_k, gain, *, batch, seq_total, tq):
    t = proj.shape[0]
    col = lambda c0: pl.BlockSpec((seq_total, LANES), lambda b, h: (b, c0 // LANES + h))
    return pl.pallas_call(
        functools.partial(_fox_attn_body, seq_total=seq_total, tq=tq),
        grid=(batch, FOX_HEADS),
        in_specs=[col(COL_FQ), col(COL_FK), col(COL_FV), col(0), col(0),
                  pl.BlockSpec((1, LANES), lambda b, h: (0, h))],
        out_specs=col(0),
        out_shape=jax.ShapeDtypeStruct((t, W_FOX), BF16),
        scratch_shapes=[pltpu.VMEM((seq_total, 2 * LANES), BF16),
                        pltpu.VMEM((seq_total, 2 * LANES), BF16)],
        compiler_params=_params("parallel", "parallel"),
        name="fox_attention",
    )(proj, proj, proj, aug_q, aug_k, gain.reshape(1, W_FOX).astype(F32))


def _split3(x):
    hi = x.astype(BF16)
    r1 = x - hi.astype(F32)
    mid = r1.astype(BF16)
    lo = (r1 - mid.astype(F32)).astype(BF16)
    return hi.astype(F32), mid.astype(F32), lo.astype(F32)


def _fox_prep_body(z_ref, b_ref, aq_ref, ak_ref, carry_ref, *, tr):
    i = pl.program_id(1)

    @pl.when(i == 0)
    def _():
        carry_ref[...] = jnp.zeros_like(carry_ref)

    row = i * tr + lax.broadcasted_iota(jnp.int32, (tr, LANES), 0)
    valid = row >= N_PAD
    log_f = jnp.where(valid, _log_sigmoid(z_ref[...] + b_ref[...]), 0.0)
    r = lax.broadcasted_iota(jnp.int32, (tr, tr), 0)
    c = lax.broadcasted_iota(jnp.int32, (tr, tr), 1)
    tri = jnp.where(c <= r, 1.0, 0.0).astype(F32)
    csum = jnp.dot(tri, log_f, preferred_element_type=F32,
                   precision=lax.Precision.HIGHEST) + carry_ref[0:1, :]
    carry_ref[...] = jnp.broadcast_to(csum[tr - 1:tr, :], carry_ref.shape)

    lane = lax.broadcasted_iota(jnp.int32, (tr, LANES), 1)
    key_mask = jnp.where(valid, 0.0, NEG_INF)
    for h in range(FOX_HEADS):
        col = csum[:, GATE_FZ + h:GATE_FZ + h + 1]
        hi, mid, lo = _split3(col)
        aq = jnp.where(lane == 0, hi, jnp.where(lane == 1, mid, jnp.where(lane == 2, lo,
             jnp.where(lane < 7, 1.0, 0.0))))
        ak = jnp.where(lane < 3, 1.0, jnp.where(lane == 3, -hi, jnp.where(lane == 4, -mid,
             jnp.where(lane == 5, -lo, jnp.where(lane == 6, key_mask, 0.0)))))
        aq_ref[:, h * LANES:(h + 1) * LANES] = aq.astype(BF16)
        ak_ref[:, h * LANES:(h + 1) * LANES] = ak.astype(BF16)


def fox_prep(gates, fox_b, *, batch, seq_total):
    t = gates.shape[0]
    tr = _pick(seq_total, 384, LANES)
    nblk = seq_total // tr
    bias = jnp.zeros((1, LANES), F32).at[0, GATE_FZ:GATE_FZ + FOX_HEADS].set(fox_b.astype(F32))
    out = jax.ShapeDtypeStruct((t, W_FOX), BF16)
    return pl.pallas_call(
        functools.partial(_fox_prep_body, tr=tr),
        grid=(batch, nblk),
        in_specs=[pl.BlockSpec((tr, LANES), lambda b, i: (b * nblk + i, 0)),
                  pl.BlockSpec((1, LANES), lambda b, i: (0, 0))],
        out_specs=[pl.BlockSpec((tr, W_FOX), lambda b, i: (b * nblk + i, 0))] * 2,
        out_shape=[out, out],
        scratch_shapes=[pltpu.VMEM((8, LANES), F32)],
        compiler_params=_params("parallel", "arbitrary"),
        name="fox_prep",
    )(gates, bias)


def _gla_body(q_ref, k_ref, v_ref, r_ref, z_ref, w2_ref, b2_ref, g_ref, o_ref, s_ref,
              *, seq_total):
    c = GLA_CHUNK
    nchunk = seq_total // c
    nt = (((1,), (1,)), ((), ()))
    tn = (((0,), (0,)), ((), ()))
    r_i = lax.broadcasted_iota(jnp.int32, (c, c), 0)
    c_i = lax.broadcasted_iota(jnp.int32, (c, c), 1)
    lower = c_i <= r_i
    tri = jnp.where(lower, 1.0, 0.0).astype(F32)
    s_ref[...] = jnp.zeros_like(s_ref)

    def chunk(i, _):
        r0 = pl.multiple_of(i * c, c)
        rows = pl.ds(r0, c)
        valid = (r0 + lax.broadcasted_iota(jnp.int32, (c, GLA_DK), 0)) >= N_PAD
        q = q_ref[rows, :].astype(F32) * (GLA_DK ** -0.5)
        k = jnp.where(valid, k_ref[rows, :].astype(F32), 0.0)
        v = v_ref[rows, :]
        logit = jnp.dot(z_ref[rows, :].astype(BF16), w2_ref[...],
                        preferred_element_type=F32) + b2_ref[...]
        log_a = jnp.where(valid, _log_sigmoid(logit) / GLA_TAU, 0.0)
        bc = jnp.dot(tri, log_a, preferred_element_type=F32, precision=lax.Precision.HIGHEST)
        b_last = bc[c - 1:c, :]
        q_dec = (q * jnp.exp(bc)).astype(BF16)
        k_dec = (k * jnp.exp(-bc)).astype(BF16)
        a = lax.dot_general(q_dec, k_dec, nt, preferred_element_type=F32)
        a = jnp.where(lower, a, 0.0).astype(BF16)
        state = s_ref[...]
        o = (jnp.dot(a, v, preferred_element_type=F32)
             + jnp.dot(q_dec, state.astype(BF16), preferred_element_type=F32))
        k_state = (k * jnp.exp(b_last - bc)).astype(BF16)
        decay = jnp.exp(b_last)
        dec_t = jnp.transpose(jnp.broadcast_to(decay, (GLA_DK, GLA_DK)))
        dec_t = jnp.concatenate([dec_t, dec_t], axis=1)
        s_ref[...] = state * dec_t + lax.dot_general(k_state, v, tn, preferred_element_type=F32)
        o = o * lax.rsqrt(jnp.mean(o * o, axis=-1, keepdims=True) + RMS_EPS) * g_ref[...]
        o_ref[rows, :] = (o * _silu(r_ref[rows, :].astype(F32))).astype(o_ref.dtype)
        return 0

    lax.fori_loop(0, nchunk, chunk, 0)


def gla(proj, gates, w2, b2, gain, *, batch, seq_total):
    t = proj.shape[0]
    blk = lambda w, c0: pl.BlockSpec((seq_total, w), lambda b, h: (b, c0 // w + h))
    return pl.pallas_call(
        functools.partial(_gla_body, seq_total=seq_total),
        grid=(batch, GLA_HEADS),
        in_specs=[blk(GLA_DK, COL_GQ), blk(GLA_DK, COL_GK), blk(GLA_DV, COL_GV),
                  blk(GLA_DV, COL_GR),
                  pl.BlockSpec((seq_total, LANES), lambda b, h: (b, 0)),
                  pl.BlockSpec((LANES, GLA_DK), lambda b, h: (0, h)),
                  pl.BlockSpec((1, GLA_DK), lambda b, h: (0, h)),
                  pl.BlockSpec((1, GLA_DV), lambda b, h: (0, h))],
        out_specs=pl.BlockSpec((seq_total, GLA_DV), lambda b, h: (b, h)),
        out_shape=jax.ShapeDtypeStruct((t, W_GLA), BF16),
        scratch_shapes=[pltpu.VMEM((GLA_DK, GLA_DV), F32)],
        compiler_params=_params("parallel", "parallel"),
        name="gla",
    )(proj, proj, proj, proj, gates, w2, b2, gain.reshape(1, W_GLA).astype(F32))


HALO = 16


def _conv_glu_body(ug_ref, uv_ref, hg_ref, hv_ref, wg_ref, wv_ref, bg_ref, bv_ref, o_ref,
                   *, seq_total, tr):
    start = (pl.program_id(0) % (seq_total // tr)) * tr
    shape = ug_ref.shape
    row = lax.broadcasted_iota(jnp.int32, shape, 0)
    valid = (start + row) >= N_PAD
    halo_ok = jnp.logical_and(start > 0, start - 2 >= N_PAD)

    def conv(u_ref, h_ref, w_ref, b_ref):
        u = jnp.where(valid, u_ref[...].astype(F32), 0.0)
        halo = h_ref[...].astype(F32)
        h1 = jnp.where(halo_ok, halo[HALO - 1:HALO, :], 0.0)
        h2 = jnp.where(halo_ok, halo[HALO - 2:HALO - 1, :], 0.0)
        u1 = jnp.where(row >= 1, pltpu.roll(u, 1, 0), h1)
        u2 = jnp.where(row >= 2, pltpu.roll(u, 2, 0), jnp.where(row == 1, h1, h2))
        w = w_ref[...]
        return b_ref[...] + w[0:1, :] * u2 + w[1:2, :] * u1 + w[2:3, :] * u

    gate = conv(ug_ref, hg_ref, wg_ref, bg_ref)
    val = conv(uv_ref, hv_ref, wv_ref, bv_ref)
    o_ref[...] = (_silu(gate) * val).astype(o_ref.dtype)


def conv_glu(u, conv_w, conv_b, *, seq_total, d_ff):
    t = u.shape[0]
    tr = _pick(seq_total, 704, HALO)
    tc = _pick(d_ff, 256, LANES)
    nj = d_ff // tc
    hb = tr // HALO
    main = lambda off: pl.BlockSpec((tr, tc), lambda i, j: (i, j + off))
    halo = lambda off: pl.BlockSpec((HALO, tc), lambda i, j: (jnp.maximum(i * hb - 1, 0), j + off))
    wspec = lambda off: pl.BlockSpec((CONV_W, tc), lambda i, j: (0, j + off))
    bspec = lambda off: pl.BlockSpec((1, tc), lambda i, j: (0, j + off))
    return pl.pallas_call(
        functools.partial(_conv_glu_body, seq_total=seq_total, tr=tr),
        grid=(t // tr, nj),
        in_specs=[main(0), main(nj), halo(0), halo(nj), wspec(0), wspec(nj), bspec(0), bspec(nj)],
        out_specs=pl.BlockSpec((tr, tc), lambda i, j: (i, j)),
        out_shape=jax.ShapeDtypeStruct((t, d_ff), BF16),
        compiler_params=_params("parallel", "parallel"),
        name="conv_glu",
    )(u, u, u, u, conv_w, conv_w, conv_b.reshape(1, -1), conv_b.reshape(1, -1))


def _rope_cols(w):
    z = jnp.zeros((w.shape[0], HALF_ROPE), w.dtype)
    return jnp.concatenate([w[:, :HALF_ROPE], z, w[:, HALF_ROPE:], z], axis=1)


def _prep_w_in(w_in):
    sizes = (MLA_Q_LORA, MLA_KV_LORA, MLA_ROPE, GLA_HEADS * GLA_DK, GLA_HEADS * GLA_DK, W_GLA,
             GLA_GATE_RANK, W_GLA, W_FOX, W_FOX, W_FOX, FOX_HEADS)
    parts, start = [], 0
    for s in sizes:
        parts.append(w_in[:, start:start + s])
        start += s
    c_q, c_kv, k_rope, g_q, g_k, g_v, g_z, g_r, f_q, f_k, f_v, f_z = parts
    big = jnp.concatenate([c_q, c_kv, g_v, g_r, g_q, g_k, f_q, f_k, f_v, _rope_cols(k_rope)],
                          axis=1).astype(BF16)
    pad = jnp.zeros((w_in.shape[0], LANES - GLA_GATE_RANK - FOX_HEADS), w_in.dtype)
    gates = jnp.concatenate([g_z, f_z, pad], axis=1).astype(BF16)
    return big, gates


def _prep_w_uq(w_uq):
    k = w_uq.shape[0]
    w = w_uq.reshape(k, MLA_HEADS, MLA_NOPE + MLA_ROPE)
    rope = _rope_cols(w[:, :, MLA_NOPE:].reshape(k * MLA_HEADS, MLA_ROPE)).reshape(k, MLA_HEADS, LANES)
    return jnp.concatenate([w[:, :, :MLA_NOPE], rope], axis=2).reshape(k, MLA_HEADS * 2 * LANES).astype(BF16)


def _rope_tables(seq_total):
    pos = np.maximum(np.arange(seq_total) - N_PAD, 0).astype(np.float32)
    inv_freq = (1.0 / (ROPE_THETA ** (np.arange(0, MLA_ROPE, 2, dtype=np.float32) / MLA_ROPE))).astype(np.float32)
    ang = jnp.asarray(pos)[:, None] * jnp.asarray(inv_freq)[None, :]
    cos, sin = jnp.cos(ang), jnp.sin(ang)
    z = jnp.zeros_like(cos)
    return (jnp.concatenate([cos, z, cos, z], axis=1),
            jnp.concatenate([-sin, z, sin, z], axis=1))


def kernel(x, meta_tokens, attn_norm, w_in, mla_q_norm, mla_w_uq, mla_kv_norm, mla_w_ukv,
           gla_w_gate2, gla_b_gate, fox_b_f, out_norm_mla, out_norm_gla, out_norm_fox,
           w_out, ffn_norm, ffn_w_up, ffn_conv_w, ffn_conv_b, ffn_w_down, final_norm):
    batch, seq, d_model = x.shape
    depth = w_in.shape[0]
    d_ff = ffn_w_down.shape[1]
    seq_total = LEAD + seq
    t = batch * seq_total

    meta = jnp.broadcast_to(meta_tokens.astype(x.dtype)[None], (batch, N_META, d_model))
    h = jnp.concatenate([jnp.zeros((batch, N_PAD, d_model), x.dtype), meta, x], axis=1)
    h = h.reshape(t, d_model)
    cos_t, sin_t = _rope_tables(seq_total)

    tm_big = _pick(seq_total, 1408, LANES)
    tm_mid = _pick(seq_total, 704, LANES)
    tq = _pick(seq_total, 384, LANES)

    for layer in range(depth):
        w_big, w_gates = _prep_w_in(w_in[layer])
        hn = rmsnorm(h, attn_norm[layer])
        proj = matmul(hn, w_big, tm=tm_big, tn=_pick(N_PROJ, 896, LANES))
        gates = matmul(hn, w_gates, tm=tm_big, tn=LANES, out_dtype=F32)

        cq = rmsnorm(proj, mla_q_norm[layer], col0=COL_CQ, width=MLA_Q_LORA)
        ckv = rmsnorm(proj, mla_kv_norm[layer], col0=COL_CKV, width=MLA_KV_LORA)
        q_raw = matmul(cq, _prep_w_uq(mla_w_uq[layer]), tm=tm_big, tn=1024)
        kv = matmul(ckv, mla_w_ukv[layer].astype(BF16), tm=tm_big, tn=1024)
        o_mla = mla_attention(q_raw, kv, proj, cos_t, sin_t, out_norm_mla[layer],
                              batch=batch, seq_total=seq_total, tq=tq)

        w2 = jnp.zeros((LANES, GLA_HEADS * GLA_DK), F32).at[:GLA_GATE_RANK].set(gla_w_gate2[layer])
        o_gla = gla(proj, gates, w2.astype(BF16), gla_b_gate[layer].reshape(1, -1),
                    out_norm_gla[layer], batch=batch, seq_total=seq_total)

        aug_q, aug_k = fox_prep(gates, fox_b_f[layer], batch=batch, seq_total=seq_total)
        o_fox = fox_attention(proj, aug_q, aug_k, out_norm_fox[layer],
                              batch=batch, seq_total=seq_total, tq=tq)

        mix = jnp.concatenate([o_mla, o_gla, o_fox], axis=1)
        h = matmul(mix, w_out[layer].astype(BF16), tm=tm_big, tn=512, out_dtype=F32,
                   residual=h, seq_total=seq_total)

        hn = rmsnorm(h, ffn_norm[layer])
        u = matmul(hn, ffn_w_up[layer].astype(BF16), tm=tm_big, tn=512)
        act = conv_glu(u, ffn_conv_w[layer], ffn_conv_b[layer], seq_total=seq_total, d_ff=d_ff)
        h = matmul(act, ffn_w_down[layer].astype(BF16), tm=tm_mid, tn=512,
                   tk=_pick(d_ff, 5504, LANES), out_dtype=F32, residual=h, seq_total=seq_total)

    return final_rmsnorm(h, final_norm, batch, seq_total)
```

```python
import functools

import jax
import jax.numpy as jnp
import numpy as np
from jax import lax
from jax.experimental import pallas as pl
from jax.experimental.pallas import tpu as pltpu

F32 = jnp.float32
BF16 = jnp.bfloat16

N_META = 16
LEAD = 128
N_PAD = LEAD - N_META
RMS_EPS = 1e-6
NEG_INF = -1e30

MLA_V = 128
MLA_HEADS = 12
MLA_NOPE = 128
MLA_ROPE = 64
MLA_Q_LORA = 1536
MLA_KV_LORA = 512
ROPE_THETA = 10000.0

GLA_DV = 256
GLA_DK = 128
GLA_HEADS = 4
GLA_GATE_RANK = 16
GLA_TAU = 16.0
GLA_CHUNK = 64

FOX_DH = 128
FOX_HEADS = 12

W_MLA = MLA_HEADS * MLA_V
W_GLA = GLA_HEADS * GLA_DV
W_FOX = FOX_HEADS * FOX_DH
CONV_W = 3

LANES = 128
SUBLANES = 8
VMEM_LIMIT_BYTES = 56 * 1024 * 1024

TM_BIG = 1408
TM_MID = 704
ATTN_BLOCK = 512
DOWN_TK = 1024
GLA_HEADS_PER_STEP = 2
QUERY_PART = 256

HALF_ROPE = MLA_ROPE // 2
W_GQK = GLA_HEADS * GLA_DK
COL_CQ = 0
COL_CKV = COL_CQ + MLA_Q_LORA
COL_GV = COL_CKV + MLA_KV_LORA
COL_GR = COL_GV + W_GLA
COL_GQ = COL_GR + W_GLA
COL_GK = COL_GQ + W_GQK
COL_FQ = COL_GK + W_GQK
COL_FK = COL_FQ + W_FOX
COL_FV = COL_FK + W_FOX
COL_KR = COL_FV + W_FOX
N_PROJ = COL_KR + LANES
SRC_CQ = 0
SRC_CKV = SRC_CQ + MLA_Q_LORA
SRC_KR = SRC_CKV + MLA_KV_LORA
SRC_GQ = SRC_KR + MLA_ROPE
SRC_GK = SRC_GQ + W_GQK
SRC_GV = SRC_GK + W_GQK
SRC_GZ = SRC_GV + W_GLA
SRC_GR = SRC_GZ + GLA_GATE_RANK
SRC_FQ = SRC_GR + W_GLA
SRC_FK = SRC_FQ + W_FOX
SRC_FV = SRC_FK + W_FOX
SRC_FZ = SRC_FV + W_FOX
GATE_FZ = GLA_GATE_RANK
ONE_LANE = MLA_ROPE
LOG2E = 1.4426950408889634
V_ROWS = LANES + 16


def _pick(n, target, mult):
    best = None
    for d in range(mult, min(n, target) + 1, mult):
        if n % d == 0:
            best = d
    assert best is not None, (n, target, mult)
    return best


def _params(*sem):
    return pltpu.CompilerParams(dimension_semantics=sem, vmem_limit_bytes=VMEM_LIMIT_BYTES)


def _log_sigmoid(x):
    return jnp.minimum(x, 0.0) - jnp.log1p(jnp.exp(-jnp.abs(x)))


def _silu(x):
    return x / (1.0 + jnp.exp(-x))


def _batch_row0(step, seq_total, tm):
    return (step % (seq_total // tm)) * tm


def _relayout_body(x_ref, *o_refs, plans):
    for o_ref, plan in zip(o_refs, plans):
        width = o_ref.shape[1]
        pos = 0
        for src, dst, w in plan:
            assert dst >= pos
            if dst > pos:
                o_ref[:, pos:dst] = jnp.zeros((o_ref.shape[0], dst - pos), o_ref.dtype)
            o_ref[:, dst:dst + w] = x_ref[:, src:src + w].astype(o_ref.dtype)
            pos = dst + w
        if pos < width:
            o_ref[:, pos:width] = jnp.zeros((o_ref.shape[0], width - pos), o_ref.dtype)


def relayout_cols(w, plans, widths, *, rows=256):
    k, n = w.shape
    tr = _pick(k, rows, 16)
    return pl.pallas_call(
        functools.partial(_relayout_body, plans=plans),
        grid=(k // tr,),
        in_specs=[pl.BlockSpec((tr, n), lambda i: (i, 0))],
        out_specs=[pl.BlockSpec((tr, wd), lambda i: (i, 0)) for wd in widths],
        out_shape=[jax.ShapeDtypeStruct((k, wd), BF16) for wd in widths],
        compiler_params=_params("parallel"),
        name="relayout_cols",
    )(w)


W_IN_PLAN = sorted([
    (SRC_CQ, COL_CQ, MLA_Q_LORA), (SRC_CKV, COL_CKV, MLA_KV_LORA), (SRC_KR, COL_KR, MLA_ROPE),
    (SRC_GQ, COL_GQ, W_GQK), (SRC_GK, COL_GK, W_GQK), (SRC_GV, COL_GV, W_GLA),
    (SRC_GR, COL_GR, W_GLA), (SRC_FQ, COL_FQ, W_FOX), (SRC_FK, COL_FK, W_FOX),
    (SRC_FV, COL_FV, W_FOX)], key=lambda p: p[1])
W_GATE_PLAN = [(SRC_GZ, 0, GLA_GATE_RANK), (SRC_FZ, GATE_FZ, FOX_HEADS)]
W_UQ_PLAN = [(h * (MLA_NOPE + MLA_ROPE), h * 2 * LANES, MLA_NOPE + MLA_ROPE) for h in range(MLA_HEADS)]


def _rmsnorm_body(x_ref, g_ref, o_ref):
    x = x_ref[...].astype(F32)
    y = x * lax.rsqrt(jnp.mean(x * x, axis=-1, keepdims=True) + RMS_EPS)
    o_ref[...] = (y * g_ref[...]).astype(o_ref.dtype)


def rmsnorm(x, gain, *, col0=0, width=None, out_dtype=BF16, rows=256):
    m = x.shape[0]
    width = x.shape[1] if width is None else width
    assert col0 % width == 0
    tr = _pick(m, rows, 16)
    return pl.pallas_call(
        _rmsnorm_body,
        grid=(m // tr,),
        in_specs=[pl.BlockSpec((tr, width), lambda i: (i, col0 // width)),
                  pl.BlockSpec((1, width), lambda i: (0, 0))],
        out_specs=pl.BlockSpec((tr, width), lambda i: (i, 0)),
        out_shape=jax.ShapeDtypeStruct((m, width), out_dtype),
        compiler_params=_params("parallel"),
        name="rmsnorm",
    )(x, gain.reshape(1, width).astype(F32))


def _final_norm_body(x_ref, g_ref, o_ref):
    x = x_ref[...]
    y = x * lax.rsqrt(jnp.mean(x * x, axis=-1, keepdims=True) + RMS_EPS)
    o_ref[0] = y * g_ref[...]


def final_rmsnorm(h, gain, batch, seq_total):
    d = h.shape[1]
    tr = LEAD
    nblk = seq_total // tr
    return pl.pallas_call(
        _final_norm_body,
        grid=(batch, nblk - 1),
        in_specs=[pl.BlockSpec((tr, d), lambda b, i: (b * nblk + i + 1, 0)),
                  pl.BlockSpec((1, d), lambda b, i: (0, 0))],
        out_specs=pl.BlockSpec((1, tr, d), lambda b, i: (b, i, 0)),
        out_shape=jax.ShapeDtypeStruct((batch, seq_total - LEAD, d), F32),
        compiler_params=_params("parallel", "parallel"),
        name="final_norm",
    )(h, gain.reshape(1, d).astype(F32))


def _mm_body(a_ref, w_ref, o_ref):
    o_ref[...] = jnp.dot(a_ref[...], w_ref[...].astype(BF16),
                         preferred_element_type=F32).astype(o_ref.dtype)


def matmul(a, w, *, tm, tn, out_dtype=BF16):
    m, kdim = a.shape
    n = w.shape[1]
    assert m % tm == 0 and n % tn == 0 and w.shape[0] == kdim
    return pl.pallas_call(
        _mm_body,
        grid=(m // tm, n // tn),
        in_specs=[pl.BlockSpec((tm, kdim), lambda i, j: (i, 0)),
                  pl.BlockSpec((kdim, tn), lambda i, j: (0, j))],
        out_specs=pl.BlockSpec((tm, tn), lambda i, j: (i, j)),
        out_shape=jax.ShapeDtypeStruct((m, n), out_dtype),
        compiler_params=_params("parallel", "parallel"),
        name="matmul",
    )(a, w)


def _out_proj_body(a1_ref, a2_ref, a3_ref, w_ref, r_ref, o_ref, wb_ref, *, seq_total, tm):
    i = pl.program_id(1)

    @pl.when(i == 0)
    def _():
        wb_ref[...] = w_ref[...].astype(BF16)

    k1 = a1_ref.shape[1]
    k2 = k1 + a2_ref.shape[1]
    acc = jnp.dot(a1_ref[...], wb_ref[0:k1, :], preferred_element_type=F32)
    acc += jnp.dot(a2_ref[...], wb_ref[k1:k2, :], preferred_element_type=F32)
    acc += jnp.dot(a3_ref[...], wb_ref[k2:, :], preferred_element_type=F32)
    pos = _batch_row0(i, seq_total, tm) + lax.broadcasted_iota(jnp.int32, acc.shape, 0)
    o_ref[...] = r_ref[...] + jnp.where(pos >= N_PAD, acc, 0.0)


def out_proj(a1, a2, a3, w, res, *, seq_total, tm, tn):
    m = a1.shape[0]
    kdim, n = w.shape
    assert a1.shape[1] + a2.shape[1] + a3.shape[1] == kdim
    assert m % tm == 0 and n % tn == 0 and seq_total % tm == 0
    a_spec = lambda a: pl.BlockSpec((tm, a.shape[1]), lambda j, i: (i, 0))
    return pl.pallas_call(
        functools.partial(_out_proj_body, seq_total=seq_total, tm=tm),
        grid=(n // tn, m // tm),
        in_specs=[a_spec(a1), a_spec(a2), a_spec(a3),
                  pl.BlockSpec((kdim, tn), lambda j, i: (0, j)),
                  pl.BlockSpec((tm, tn), lambda j, i: (i, j))],
        out_specs=pl.BlockSpec((tm, tn), lambda j, i: (i, j)),
        out_shape=jax.ShapeDtypeStruct((m, n), F32),
        scratch_shapes=[pltpu.VMEM((kdim, tn), BF16)],
        compiler_params=_params("parallel", "arbitrary"),
        name="out_proj",
    )(a1, a2, a3, w, res)


CONV_CHUNK = 128


def _ffn_up_body(a_ref, wg_ref, wv_ref, cwg_ref, cwv_ref, cbg_ref, cbv_ref, o_ref,
                 wcat_ref, u_ref, halo_ref, *, seq_total, tm, tc, nj):
    i = pl.program_id(0)
    j = pl.program_id(1)

    @pl.when(j == nj)
    def _():
        o_ref[...] = jnp.zeros_like(o_ref)

    @pl.when(j < nj)
    def _():
        wcat_ref[:, 0:tc] = wg_ref[...].astype(BF16)
        wcat_ref[:, tc:2 * tc] = wv_ref[...].astype(BF16)
        row0 = _batch_row0(i, seq_total, tm)
        u = jnp.dot(a_ref[...], wcat_ref[...], preferred_element_type=F32)
        pos = row0 + lax.broadcasted_iota(jnp.int32, u.shape, 0)
        u_ref[SUBLANES:SUBLANES + tm, :] = jnp.where(pos >= N_PAD, u, 0.0)
        @pl.when(row0 == 0)
        def _():
            u_ref[0:SUBLANES, :] = jnp.zeros((SUBLANES, 2 * tc), F32)

        @pl.when(row0 > 0)
        def _():
            u_ref[0:SUBLANES, :] = halo_ref[j]

        halo_ref[j] = u_ref[tm:tm + SUBLANES, :]
        cw = jnp.concatenate([cwg_ref[...], cwv_ref[...]], axis=1)
        cb = jnp.concatenate([cbg_ref[...], cbv_ref[...]], axis=1)
        for c in range(tm // CONV_CHUNK):
            r0 = c * CONV_CHUNK
            cv = cb
            for tap in range(CONV_W):
                lo = SUBLANES + r0 - (CONV_W - 1 - tap)
                cv = cv + cw[tap:tap + 1, :] * u_ref[lo:lo + CONV_CHUNK, :]
            o_ref[r0:r0 + CONV_CHUNK, :] = (_silu(cv[:, 0:tc]) * cv[:, tc:2 * tc]).astype(o_ref.dtype)


def ffn_up(hn, w_up, conv_w, conv_b, *, seq_total, tm, tc, d_ff, d_ff_pad):
    m, kdim = hn.shape
    assert m % tm == 0 and seq_total % tm == 0 and tm % CONV_CHUNK == 0
    assert d_ff % tc == 0 and (d_ff_pad - d_ff) in (0, tc)
    nj = d_ff // tc
    jc = lambda j: jnp.minimum(j, nj - 1)
    wspec = lambda off: pl.BlockSpec((kdim, tc), lambda i, j: (0, jc(j) + off))
    cspec = lambda r, off: pl.BlockSpec((r, tc), lambda i, j: (0, jc(j) + off))
    return pl.pallas_call(
        functools.partial(_ffn_up_body, seq_total=seq_total, tm=tm, tc=tc, nj=nj),
        grid=(m // tm, d_ff_pad // tc),
        in_specs=[pl.BlockSpec((tm, kdim), lambda i, j: (i, 0)),
                  wspec(0), wspec(nj), cspec(CONV_W, 0), cspec(CONV_W, nj),
                  cspec(1, 0), cspec(1, nj)],
        out_specs=pl.BlockSpec((tm, tc), lambda i, j: (i, j)),
        out_shape=jax.ShapeDtypeStruct((m, d_ff_pad), BF16),
        scratch_shapes=[pltpu.VMEM((kdim, 2 * tc), BF16),
                        pltpu.VMEM((tm + SUBLANES, 2 * tc), F32),
                        pltpu.VMEM((nj, SUBLANES, 2 * tc), F32)],
        compiler_params=_params("arbitrary", "arbitrary"),
        name="ffn_up",
    )(hn, w_up, w_up, conv_w, conv_w, conv_b.reshape(1, -1), conv_b.reshape(1, -1))


def _ffn_down_body(a_ref, w_ref, r_ref, o_ref, wb_ref, *, seq_total, tm, tk, d_ff, nk):
    k = pl.program_id(2)
    wrow = k * tk + lax.broadcasted_iota(jnp.int32, w_ref.shape, 0)
    wb_ref[...] = jnp.where(wrow < d_ff, w_ref[...], 0.0).astype(BF16)

    def part():
        return jnp.dot(a_ref[...], wb_ref[...], preferred_element_type=F32)

    @pl.when(k == 0)
    def _():
        o_ref[...] = part()

    @pl.when(jnp.logical_and(k > 0, k < nk - 1))
    def _():
        o_ref[...] += part()

    @pl.when(k == nk - 1)
    def _():
        pos = _batch_row0(pl.program_id(0), seq_total, tm) + lax.broadcasted_iota(
            jnp.int32, o_ref.shape, 0)
        o_ref[...] = r_ref[...] + jnp.where(pos >= N_PAD, o_ref[...] + part(), 0.0)


def ffn_down(act, w, res, *, seq_total, tm, tn, tk, d_ff):
    m, kpad = act.shape
    n = w.shape[1]
    assert m % tm == 0 and n % tn == 0 and kpad % tk == 0 and seq_total % tm == 0
    nk = kpad // tk
    assert nk >= 2
    return pl.pallas_call(
        functools.partial(_ffn_down_body, seq_total=seq_total, tm=tm, tk=tk, d_ff=d_ff, nk=nk),
        grid=(m // tm, n // tn, kpad // tk),
        in_specs=[pl.BlockSpec((tm, tk), lambda i, j, k: (i, k)),
                  pl.BlockSpec((tk, tn), lambda i, j, k: (k, j)),
                  pl.BlockSpec((tm, tn), lambda i, j, k: (i, j))],
        out_specs=pl.BlockSpec((tm, tn), lambda i, j, k: (i, j)),
        out_shape=jax.ShapeDtypeStruct((m, n), F32),
        scratch_shapes=[pltpu.VMEM((tk, tn), BF16)],
        compiler_params=_params("parallel", "parallel", "arbitrary"),
        name="ffn_down",
    )(act, w, res)


def _scores(k_rows, q_parts):
    out = []
    for q in q_parts:
        s = jnp.dot(k_rows, q, preferred_element_type=F32)
        out.append((s, jnp.max(s, axis=0, keepdims=True)))
    return tuple(out)


def _causal(scored, col0s):
    out = []
    for (s, _), c0 in zip(scored, col0s):
        key = lax.broadcasted_iota(jnp.int32, s.shape, 0)
        qry = c0 + lax.broadcasted_iota(jnp.int32, s.shape, 1)
        s = jnp.where(key <= qry, s, NEG_INF)
        out.append((s, jnp.max(s, axis=0, keepdims=True)))
    return tuple(out)


def _softmax_pv(carry, scored, v_ext):
    out = []
    for (m, acc), (s, s_max) in zip(carry, scored):
        m_new = jnp.maximum(m, s_max)
        alpha = jnp.exp2(m - m_new)
        p = jnp.exp2(s - m_new).astype(BF16)
        out.append((m_new, alpha * acc + jnp.dot(v_ext, p, preferred_element_type=F32)))
    return tuple(out)


def _attn_init(widths):
    return tuple((jnp.full((1, w), NEG_INF, F32), jnp.zeros((V_ROWS, w), F32)) for w in widths)


def _attn_finish(carry, gain, o_ref, row0, col0s):
    for (_, acc), c0 in zip(carry, col0s):
        o = jnp.transpose(acc[0:LANES, :] / acc[LANES:LANES + 1, :])
        o = o * lax.rsqrt(jnp.mean(o * o, axis=-1, keepdims=True) + RMS_EPS) * gain
        o_ref[pl.ds(row0 + c0, o.shape[0]), :] = o.astype(o_ref.dtype)


def _query_parts(tb):
    part = QUERY_PART if tb % QUERY_PART == 0 else tb
    return part, tuple(range(0, tb, part))


def _attention_core(qt, qt0, ks, vt, vt0, s_scr, m_scr, gain, o_ref, *, nblk, tb):
    lead = _causal(_scores(ks[0:LEAD, :], (qt0[...],)), (0,))
    _attn_finish(_softmax_pv(_attn_init((LEAD,)), lead, vt0[...]), gain, o_ref, 0, (0,))

    part, col0s = _query_parts(tb)
    widths = (part,) * len(col0s)

    def q_block(qi, _):
        q_parts = tuple(qt[qi, :, c0:c0 + part] for c0 in col0s)

        def score(slot, kj):
            k_rows = ks[pl.ds(pl.multiple_of(LEAD + kj * tb, LANES), tb), :]
            for p, (s, s_max) in enumerate(_scores(k_rows, q_parts)):
                s_scr[slot, p] = s
                m_scr[slot, p] = s_max

        def scored(slot):
            return tuple((s_scr[slot, p], m_scr[slot, p]) for p in range(len(col0s)))

        carry = _softmax_pv(_attn_init(widths), _scores(ks[0:LEAD, :], q_parts), vt0[...])
        score(0, 0)

        def pair(t, carry):
            kj = 2 * t
            score(1, kj + 1)
            carry = _softmax_pv(carry, scored(0), vt[kj])
            score(0, kj + 2)
            return _softmax_pv(carry, scored(1), vt[kj + 1])

        carry = lax.fori_loop(0, qi // 2, pair, carry)

        def odd_tail(carry):
            score(1, qi)
            carry = _softmax_pv(carry, scored(0), vt[qi - 1])
            return _softmax_pv(carry, _causal(scored(1), col0s), vt[qi])

        def even_tail(carry):
            return _softmax_pv(carry, _causal(scored(0), col0s), vt[qi])

        carry = lax.cond(qi % 2 == 1, odd_tail, even_tail, carry)
        _attn_finish(carry, gain, o_ref, pl.multiple_of(LEAD + qi * tb, LANES), col0s)
        return 0

    lax.fori_loop(0, nblk, q_block, 0)


def _store_v_ext(vt_dst, v_rows):
    n = v_rows.shape[0]
    vt_dst[0:LANES, :] = _to_t(v_rows)
    r = lax.broadcasted_iota(jnp.int32, (V_ROWS - LANES, n), 0)
    vt_dst[LANES:V_ROWS, :] = jnp.where(r == 0, 1.0, 0.0).astype(BF16)


def _to_t(x):
    return jnp.transpose(x.astype(F32)).astype(BF16)


def _mla_attn_body(q_ref, kv_ref, kr_ref, cos_ref, sa_ref, sb_ref, g_ref, o_ref,
                   qt, qt0, ks, vt, vt0, s_scr, m_scr, *, nblk, tb):
    scale = (MLA_NOPE + MLA_ROPE) ** -0.5 * LOG2E

    def rope(x, rows):
        return (x * cos_ref[rows, :] + pltpu.roll(x, HALF_ROPE, 1) * sa_ref[rows, :]
                + pltpu.roll(x, LANES - HALF_ROPE, 1) * sb_ref[rows, :])

    def prep(r0, n, qt_dst, vt_dst):
        rows = pl.ds(r0, n)
        one = lax.broadcasted_iota(jnp.int32, (n, LANES), 1) == ONE_LANE
        qn = q_ref[rows, pl.ds(0, LANES)].astype(F32) * scale
        qp = rope(q_ref[rows, pl.ds(LANES, LANES)].astype(F32), rows) * scale
        qp = jnp.where(one, 1.0, qp)
        qt_dst[0:LANES, :] = jnp.transpose(qn).astype(BF16)
        qt_dst[LANES:2 * LANES, :] = jnp.transpose(qp).astype(BF16)
        pos = r0 + lax.broadcasted_iota(jnp.int32, (n, LANES), 0)
        kp = rope(kr_ref[rows, :].astype(F32), rows)
        kp = jnp.where(one, jnp.where(pos < N_PAD, NEG_INF, 0.0), kp)
        ks[rows, pl.ds(0, LANES)] = kv_ref[rows, pl.ds(0, LANES)]
        ks[rows, pl.ds(LANES, LANES)] = kp.astype(BF16)
        _store_v_ext(vt_dst, kv_ref[rows, pl.ds(LANES, LANES)])

    prep(0, LEAD, qt0, vt0)

    def prep_block(i, _):
        prep(pl.multiple_of(LEAD + i * tb, LANES), tb, qt.at[i], vt.at[i])
        return 0

    lax.fori_loop(0, nblk, prep_block, 0)
    _attention_core(qt, qt0, ks, vt, vt0, s_scr, m_scr, g_ref[...], o_ref, nblk=nblk, tb=tb)


def _attn_scratch(seq_total, nblk, tb):
    part, col0s = _query_parts(tb)
    return [pltpu.VMEM((nblk, 2 * LANES, tb), BF16), pltpu.VMEM((2 * LANES, LEAD), BF16),
            pltpu.VMEM((seq_total, 2 * LANES), BF16),
            pltpu.VMEM((nblk, V_ROWS, tb), BF16), pltpu.VMEM((V_ROWS, LEAD), BF16),
            pltpu.VMEM((2, len(col0s), tb, part), F32), pltpu.VMEM((2, len(col0s), 1, part), F32)]


def mla_attention(q_raw, kv, proj, cos_t, sin_a, sin_b, gain, *, batch, seq_total, tb):
    t = q_raw.shape[0]
    nblk = (seq_total - LEAD) // tb
    big = lambda w: pl.BlockSpec((seq_total, w), lambda b, h: (b, h))
    table = pl.BlockSpec((seq_total, LANES), lambda b, h: (0, 0))
    return pl.pallas_call(
        functools.partial(_mla_attn_body, nblk=nblk, tb=tb),
        grid=(batch, MLA_HEADS),
        in_specs=[big(2 * LANES), big(2 * LANES),
                  pl.BlockSpec((seq_total, LANES), lambda b, h: (b, COL_KR // LANES)),
                  table, table, table,
                  pl.BlockSpec((1, LANES), lambda b, h: (0, h))],
        out_specs=big(LANES),
        out_shape=jax.ShapeDtypeStruct((t, W_MLA), BF16),
        scratch_shapes=_attn_scratch(seq_total, nblk, tb),
        compiler_params=_params("parallel", "parallel"),
        name="mla_attention",
    )(q_raw, kv, proj, cos_t, sin_a, sin_b, gain.reshape(1, W_MLA).astype(F32))


def _fox_attn_body(q_ref, k_ref, v_ref, aq_ref, ak_ref, g_ref, o_ref,
                   qt, qt0, ks, vt, vt0, s_scr, m_scr, *, nblk, tb):
    scale = FOX_DH ** -0.5 * LOG2E

    def prep(r0, n, qt_dst, vt_dst):
        rows = pl.ds(r0, n)
        qt_dst[0:LANES, :] = jnp.transpose(q_ref[rows, :].astype(F32) * scale).astype(BF16)
        qt_dst[LANES:2 * LANES, :] = _to_t(aq_ref[rows, :])
        ks[rows, pl.ds(0, LANES)] = k_ref[rows, :]
        ks[rows, pl.ds(LANES, LANES)] = ak_ref[rows, :]
        _store_v_ext(vt_dst, v_ref[rows, :])

    prep(0, LEAD, qt0, vt0)

    def prep_block(i, _):
        prep(pl.multiple_of(LEAD + i * tb, LANES), tb, qt.at[i], vt.at[i])
        return 0

    lax.fori_loop(0, nblk, prep_block, 0)
    _attention_core(qt, qt0, ks, vt, vt0, s_scr, m_scr, g_ref[...], o_ref, nblk=nblk, tb=tb)


def fox_attention(proj, aug_q, aug_k, gain, *, batch, seq_total, tb):
    t = proj.shape[0]
    nblk = (seq_total - LEAD) // tb
    col = lambda c0: pl.BlockSpec((seq_total, LANES), lambda b, h: (b, c0 // LANES + h))
    return pl.pallas_call(
        functools.partial(_fox_attn_body, nblk=nblk, tb=tb),
        grid=(batch, FOX_HEADS),
        in_specs=[col(COL_FQ), col(COL_FK), col(COL_FV), col(0), col(0),
                  pl.BlockSpec((1, LANES), lambda b, h: (0, h))],
        out_specs=col(0),
        out_shape=jax.ShapeDtypeStruct((t, W_FOX), BF16),
        scratch_shapes=_attn_scratch(seq_total, nblk, tb),
        compiler_params=_params("parallel", "parallel"),
        name="fox_attention",
    )(proj, proj, proj, aug_q, aug_k, gain.reshape(1, W_FOX).astype(F32))


def _split3(x):
    hi = x.astype(BF16)
    r1 = x - hi.astype(F32)
    mid = r1.astype(BF16)
    lo = (r1 - mid.astype(F32)).astype(BF16)
    return hi.astype(F32), mid.astype(F32), lo.astype(F32)


def _fox_prep_body(z_ref, b_ref, aq_ref, ak_ref, carry_ref, *, tr):
    i = pl.program_id(1)

    @pl.when(i == 0)
    def _():
        carry_ref[...] = jnp.zeros_like(carry_ref)

    row = i * tr + lax.broadcasted_iota(jnp.int32, (tr, LANES), 0)
    valid = row >= N_PAD
    log_f = jnp.where(valid, _log_sigmoid(z_ref[...] + b_ref[...]), 0.0)
    r = lax.broadcasted_iota(jnp.int32, (tr, tr), 0)
    c = lax.broadcasted_iota(jnp.int32, (tr, tr), 1)
    tri = jnp.where(c <= r, 1.0, 0.0).astype(F32)
    csum = jnp.dot(tri, log_f, preferred_element_type=F32,
                   precision=lax.Precision.HIGHEST) + carry_ref[0:1, :]
    carry_ref[...] = jnp.broadcast_to(csum[tr - 1:tr, :], carry_ref.shape)

    lane = lax.broadcasted_iota(jnp.int32, (tr, LANES), 1)
    key_mask = jnp.where(valid, 0.0, NEG_INF)
    for h in range(FOX_HEADS):
        col = csum[:, GATE_FZ + h:GATE_FZ + h + 1] * LOG2E
        hi, mid, lo = _split3(col)
        aq = jnp.where(lane == 0, hi, jnp.where(lane == 1, mid, jnp.where(lane == 2, lo,
             jnp.where(lane < 7, 1.0, 0.0))))
        ak = jnp.where(lane < 3, 1.0, jnp.where(lane == 3, -hi, jnp.where(lane == 4, -mid,
             jnp.where(lane == 5, -lo, jnp.where(lane == 6, key_mask, 0.0)))))
        aq_ref[:, h * LANES:(h + 1) * LANES] = aq.astype(BF16)
        ak_ref[:, h * LANES:(h + 1) * LANES] = ak.astype(BF16)


def fox_prep(gates, fox_b, *, batch, seq_total):
    t = gates.shape[0]
    tr = _pick(seq_total, 384, LANES)
    nblk = seq_total // tr
    bias = jnp.zeros((1, LANES), F32).at[0, GATE_FZ:GATE_FZ + FOX_HEADS].set(fox_b.astype(F32))
    out = jax.ShapeDtypeStruct((t, W_FOX), BF16)
    return pl.pallas_call(
        functools.partial(_fox_prep_body, tr=tr),
        grid=(batch, nblk),
        in_specs=[pl.BlockSpec((tr, LANES), lambda b, i: (b * nblk + i, 0)),
                  pl.BlockSpec((1, LANES), lambda b, i: (0, 0))],
        out_specs=[pl.BlockSpec((tr, W_FOX), lambda b, i: (b * nblk + i, 0))] * 2,
        out_shape=[out, out],
        scratch_shapes=[pltpu.VMEM((SUBLANES, LANES), F32)],
        compiler_params=_params("parallel", "arbitrary"),
        name="fox_prep",
    )(gates, bias)


def _gla_body(q_ref, k_ref, v_ref, r_ref, z_ref, w2_ref, b2_ref, g_ref, o_ref, s_ref,
              *, seq_total):
    c = GLA_CHUNK
    nchunk = seq_total // c
    heads = s_ref.shape[0]
    nt = (((1,), (1,)), ((), ()))
    tn = (((0,), (0,)), ((), ()))
    r_i = lax.broadcasted_iota(jnp.int32, (c, c), 0)
    c_i = lax.broadcasted_iota(jnp.int32, (c, c), 1)
    lower = c_i <= r_i
    tri = jnp.where(lower, 1.0, 0.0).astype(F32)
    s_ref[...] = jnp.zeros_like(s_ref)

    def chunk(i, _):
        r0 = pl.multiple_of(i * c, c)
        rows = pl.ds(r0, c)
        valid = (r0 + lax.broadcasted_iota(jnp.int32, (c, GLA_DK), 0)) >= N_PAD
        z = z_ref[rows, :].astype(BF16)
        for hh in range(heads):
            kcols = pl.ds(hh * GLA_DK, GLA_DK)
            vcols = pl.ds(hh * GLA_DV, GLA_DV)
            q = q_ref[rows, kcols].astype(F32) * (GLA_DK ** -0.5)
            k = jnp.where(valid, k_ref[rows, kcols].astype(F32), 0.0)
            v = v_ref[rows, vcols]
            logit = jnp.dot(z, w2_ref[:, kcols], preferred_element_type=F32) + b2_ref[:, kcols]
            log_a = jnp.where(valid, _log_sigmoid(logit) / GLA_TAU, 0.0)
            bc = jnp.dot(tri, log_a, preferred_element_type=F32, precision=lax.Precision.HIGHEST)
            b_last = bc[c - 1:c, :]
            q_dec = (q * jnp.exp(bc)).astype(BF16)
            k_dec = (k * jnp.exp(-bc)).astype(BF16)
            a = lax.dot_general(q_dec, k_dec, nt, preferred_element_type=F32)
            a = jnp.where(lower, a, 0.0).astype(BF16)
            state = s_ref[hh]
            o = (jnp.dot(a, v, preferred_element_type=F32)
                 + jnp.dot(q_dec, state.astype(BF16), preferred_element_type=F32))
            k_state = (k * jnp.exp(b_last - bc)).astype(BF16)
            decay = jnp.exp(b_last)
            dec_t = jnp.transpose(jnp.broadcast_to(decay, (GLA_DK, GLA_DK)))
            dec_t = jnp.concatenate([dec_t, dec_t], axis=1)
            s_ref[hh] = state * dec_t + lax.dot_general(k_state, v, tn, preferred_element_type=F32)
            o = o * lax.rsqrt(jnp.mean(o * o, axis=-1, keepdims=True) + RMS_EPS) * g_ref[:, vcols]
            o_ref[rows, vcols] = (o * _silu(r_ref[rows, vcols].astype(F32))).astype(o_ref.dtype)
        return 0

    lax.fori_loop(0, nchunk, chunk, 0)


def gla(proj, gates, w2, b2, gain, *, batch, seq_total):
    t = proj.shape[0]
    hps = GLA_HEADS_PER_STEP
    assert GLA_HEADS % hps == 0
    wk, wv = hps * GLA_DK, hps * GLA_DV
    blk = lambda w, c0: pl.BlockSpec((seq_total, w), lambda b, h: (b, c0 // w + h))
    return pl.pallas_call(
        functools.partial(_gla_body, seq_total=seq_total),
        grid=(batch, GLA_HEADS // hps),
        in_specs=[blk(wk, COL_GQ), blk(wk, COL_GK), blk(wv, COL_GV), blk(wv, COL_GR),
                  pl.BlockSpec((seq_total, LANES), lambda b, h: (b, 0)),
                  pl.BlockSpec((LANES, wk), lambda b, h: (0, h)),
                  pl.BlockSpec((1, wk), lambda b, h: (0, h)),
                  pl.BlockSpec((1, wv), lambda b, h: (0, h))],
        out_specs=pl.BlockSpec((seq_total, wv), lambda b, h: (b, h)),
        out_shape=jax.ShapeDtypeStruct((t, W_GLA), BF16),
        scratch_shapes=[pltpu.VMEM((hps, GLA_DK, GLA_DV), F32)],
        compiler_params=_params("parallel", "parallel"),
        name="gla",
    )(proj, proj, proj, proj, gates, w2, b2, gain.reshape(1, W_GLA).astype(F32))


def _rope_tables(seq_total):
    pos = np.maximum(np.arange(seq_total) - N_PAD, 0).astype(np.float32)
    inv_freq = (1.0 / (ROPE_THETA ** (np.arange(0, MLA_ROPE, 2, dtype=np.float32) / MLA_ROPE))).astype(np.float32)
    ang = jnp.asarray(pos)[:, None] * jnp.asarray(inv_freq)[None, :]
    cos, sin = jnp.cos(ang), jnp.sin(ang)
    z = jnp.zeros_like(cos)
    cos_t = jnp.concatenate([cos, cos, z, z], axis=1)
    sin_a = jnp.concatenate([z, sin, z, z], axis=1)
    sin_b = jnp.concatenate([-sin, z, z, z], axis=1)
    return cos_t, sin_a, sin_b


def kernel(x, meta_tokens, attn_norm, w_in, mla_q_norm, mla_w_uq, mla_kv_norm, mla_w_ukv,
           gla_w_gate2, gla_b_gate, fox_b_f, out_norm_mla, out_norm_gla, out_norm_fox,
           w_out, ffn_norm, ffn_w_up, ffn_conv_w, ffn_conv_b, ffn_w_down, final_norm):
    batch, seq, d_model = x.shape
    depth = w_in.shape[0]
    d_ff = ffn_w_down.shape[1]
    seq_total = LEAD + seq
    t = batch * seq_total

    meta = jnp.broadcast_to(meta_tokens.astype(x.dtype)[None], (batch, N_META, d_model))
    h = jnp.concatenate([jnp.zeros((batch, N_PAD, d_model), x.dtype), meta, x], axis=1)
    h = h.reshape(t, d_model)
    cos_t, sin_a, sin_b = _rope_tables(seq_total)

    tm_big = _pick(seq_total, TM_BIG, LANES)
    tm_mid = _pick(seq_total, TM_MID, LANES)
    tb = _pick(seq, ATTN_BLOCK, LANES)
    tc = _pick(d_ff, 256, LANES)
    tk_down = DOWN_TK
    d_ff_pad = -(-d_ff // tk_down) * tk_down

    for layer in range(depth):
        w_big, w_gates = relayout_cols(w_in[layer], (W_IN_PLAN, W_GATE_PLAN), (N_PROJ, LANES))
        w_uq, = relayout_cols(mla_w_uq[layer], (W_UQ_PLAN,), (MLA_HEADS * 2 * LANES,))
        hn = rmsnorm(h, attn_norm[layer])
        proj = matmul(hn, w_big, tm=tm_big, tn=_pick(N_PROJ, 896, LANES))
        gates = matmul(hn, w_gates, tm=tm_big, tn=LANES, out_dtype=F32)

        cq = rmsnorm(proj, mla_q_norm[layer], col0=COL_CQ, width=MLA_Q_LORA)
        ckv = rmsnorm(proj, mla_kv_norm[layer], col0=COL_CKV, width=MLA_KV_LORA)
        q_raw = matmul(cq, w_uq, tm=tm_big, tn=1024)
        kv = matmul(ckv, mla_w_ukv[layer], tm=tm_big, tn=1024)
        o_mla = mla_attention(q_raw, kv, proj, cos_t, sin_a, sin_b, out_norm_mla[layer],
                              batch=batch, seq_total=seq_total, tb=tb)

        w2 = jnp.zeros((LANES, W_GQK), F32).at[:GLA_GATE_RANK].set(gla_w_gate2[layer])
        o_gla = gla(proj, gates, w2.astype(BF16), gla_b_gate[layer].reshape(1, -1),
                    out_norm_gla[layer], batch=batch, seq_total=seq_total)

        aug_q, aug_k = fox_prep(gates, fox_b_f[layer], batch=batch, seq_total=seq_total)
        o_fox = fox_attention(proj, aug_q, aug_k, out_norm_fox[layer],
                              batch=batch, seq_total=seq_total, tb=tb)

        h = out_proj(o_mla, o_gla, o_fox, w_out[layer], h, seq_total=seq_total, tm=tm_mid, tn=512)

        hn = rmsnorm(h, ffn_norm[layer])
        act = ffn_up(hn, ffn_w_up[layer], ffn_conv_w[layer], ffn_conv_b[layer],
                     seq_total=seq_total, tm=tm_big, tc=tc, d_ff=d_ff, d_ff_pad=d_ff_pad)
        h = ffn_down(act, ffn_w_down[layer], h, seq_total=seq_total, tm=tm_big, tn=1024,
                     tk=tk_down, d_ff=d_ff)

    return final_rmsnorm(h, final_norm, batch, seq_total)
```

```python
import functools

import jax
import jax.numpy as jnp
import numpy as np
from jax import lax
from jax.experimental import pallas as pl
from jax.experimental.pallas import tpu as pltpu

F32 = jnp.float32
BF16 = jnp.bfloat16

N_META = 16
LEAD = 128
N_PAD = LEAD - N_META
RMS_EPS = 1e-6
NEG_INF = -1e30

MLA_V = 128
MLA_HEADS = 12
MLA_NOPE = 128
MLA_ROPE = 64
MLA_Q_LORA = 1536
MLA_KV_LORA = 512
ROPE_THETA = 10000.0

GLA_DV = 256
GLA_DK = 128
GLA_HEADS = 4
GLA_GATE_RANK = 16
GLA_TAU = 16.0
GLA_CHUNK = 64

FOX_DH = 128
FOX_HEADS = 12

W_MLA = MLA_HEADS * MLA_V
W_GLA = GLA_HEADS * GLA_DV
W_FOX = FOX_HEADS * FOX_DH
CONV_W = 3

LANES = 128
SUBLANES = 8
VMEM_LIMIT_BYTES = 56 * 1024 * 1024

TM_BIG = 1408
TM_MID = 704
ATTN_BLOCK = 512
DOWN_TK = 1024
GLA_HEADS_PER_STEP = 2
QUERY_PART = 256

HALF_ROPE = MLA_ROPE // 2
W_GQK = GLA_HEADS * GLA_DK
COL_CQ = 0
COL_CKV = COL_CQ + MLA_Q_LORA
COL_GV = COL_CKV + MLA_KV_LORA
COL_GR = COL_GV + W_GLA
COL_GQ = COL_GR + W_GLA
COL_GK = COL_GQ + W_GQK
COL_FQ = COL_GK + W_GQK
COL_FK = COL_FQ + W_FOX
COL_FV = COL_FK + W_FOX
COL_KR = COL_FV + W_FOX
N_PROJ = COL_KR + LANES
SRC_CQ = 0
SRC_CKV = SRC_CQ + MLA_Q_LORA
SRC_KR = SRC_CKV + MLA_KV_LORA
SRC_GQ = SRC_KR + MLA_ROPE
SRC_GK = SRC_GQ + W_GQK
SRC_GV = SRC_GK + W_GQK
SRC_GZ = SRC_GV + W_GLA
SRC_GR = SRC_GZ + GLA_GATE_RANK
SRC_FQ = SRC_GR + W_GLA
SRC_FK = SRC_FQ + W_FOX
SRC_FV = SRC_FK + W_FOX
SRC_FZ = SRC_FV + W_FOX
GATE_FZ = GLA_GATE_RANK
ONE_LANE = MLA_ROPE
LOG2E = 1.4426950408889634
V_ROWS = LANES + 16


def _pick(n, target, mult):
    best = None
    for d in range(mult, min(n, target) + 1, mult):
        if n % d == 0:
            best = d
    assert best is not None, (n, target, mult)
    return best


def _params(*sem):
    return pltpu.CompilerParams(dimension_semantics=sem, vmem_limit_bytes=VMEM_LIMIT_BYTES)


def _log_sigmoid(x):
    return jnp.minimum(x, 0.0) - jnp.log1p(jnp.exp(-jnp.abs(x)))


def _silu(x):
    return x / (1.0 + jnp.exp(-x))


def _wspec(w, layer, block, index):
    if layer is None:
        assert w.ndim == len(block)
        return pl.BlockSpec(block, index)
    assert w.ndim == len(block) + 1
    return pl.BlockSpec((None,) + tuple(block), lambda *g: (layer,) + tuple(index(*g)))


def _batch_row0(step, seq_total, tm):
    return (step % (seq_total // tm)) * tm


def _relayout_body(x_ref, *o_refs, plans):
    for o_ref, plan in zip(o_refs, plans):
        width = o_ref.shape[1]
        pos = 0
        for src, dst, w in plan:
            assert dst >= pos
            if dst > pos:
                o_ref[:, pos:dst] = jnp.zeros((o_ref.shape[0], dst - pos), o_ref.dtype)
            o_ref[:, dst:dst + w] = x_ref[:, src:src + w].astype(o_ref.dtype)
            pos = dst + w
        if pos < width:
            o_ref[:, pos:width] = jnp.zeros((o_ref.shape[0], width - pos), o_ref.dtype)


def relayout_cols(w, plans, widths, *, layer=None, rows=256):
    k, n = w.shape[-2:]
    tr = _pick(k, rows, 16)
    return pl.pallas_call(
        functools.partial(_relayout_body, plans=plans),
        grid=(k // tr,),
        in_specs=[_wspec(w, layer, (tr, n), lambda i: (i, 0))],
        out_specs=[pl.BlockSpec((tr, wd), lambda i: (i, 0)) for wd in widths],
        out_shape=[jax.ShapeDtypeStruct((k, wd), BF16) for wd in widths],
        compiler_params=_params("parallel"),
        name="relayout_cols",
    )(w)


W_IN_PLAN = sorted([
    (SRC_CQ, COL_CQ, MLA_Q_LORA), (SRC_CKV, COL_CKV, MLA_KV_LORA), (SRC_KR, COL_KR, MLA_ROPE),
    (SRC_GQ, COL_GQ, W_GQK), (SRC_GK, COL_GK, W_GQK), (SRC_GV, COL_GV, W_GLA),
    (SRC_GR, COL_GR, W_GLA), (SRC_FQ, COL_FQ, W_FOX), (SRC_FK, COL_FK, W_FOX),
    (SRC_FV, COL_FV, W_FOX)], key=lambda p: p[1])
W_GATE_PLAN = [(SRC_GZ, 0, GLA_GATE_RANK), (SRC_FZ, GATE_FZ, FOX_HEADS)]
W_UQ_PLAN = [(h * (MLA_NOPE + MLA_ROPE), h * 2 * LANES, MLA_NOPE + MLA_ROPE) for h in range(MLA_HEADS)]


def _rmsnorm_body(x_ref, g_ref, o_ref):
    x = x_ref[...].astype(F32)
    y = x * lax.rsqrt(jnp.mean(x * x, axis=-1, keepdims=True) + RMS_EPS)
    o_ref[...] = (y * g_ref[...]).astype(o_ref.dtype)


def rmsnorm(x, gain, *, col0=0, width=None, out_dtype=BF16, rows=256):
    m = x.shape[0]
    width = x.shape[1] if width is None else width
    assert col0 % width == 0
    tr = _pick(m, rows, 16)
    return pl.pallas_call(
        _rmsnorm_body,
        grid=(m // tr,),
        in_specs=[pl.BlockSpec((tr, width), lambda i: (i, col0 // width)),
                  pl.BlockSpec((1, width), lambda i: (0, 0))],
        out_specs=pl.BlockSpec((tr, width), lambda i: (i, 0)),
        out_shape=jax.ShapeDtypeStruct((m, width), out_dtype),
        compiler_params=_params("parallel"),
        name="rmsnorm",
    )(x, gain.reshape(1, width).astype(F32))


def _final_norm_body(x_ref, g_ref, o_ref):
    x = x_ref[...]
    y = x * lax.rsqrt(jnp.mean(x * x, axis=-1, keepdims=True) + RMS_EPS)
    o_ref[0] = y * g_ref[...]


def final_rmsnorm(h, gain, batch, seq_total):
    d = h.shape[1]
    tr = LEAD
    nblk = seq_total // tr
    return pl.pallas_call(
        _final_norm_body,
        grid=(batch, nblk - 1),
        in_specs=[pl.BlockSpec((tr, d), lambda b, i: (b * nblk + i + 1, 0)),
                  pl.BlockSpec((1, d), lambda b, i: (0, 0))],
        out_specs=pl.BlockSpec((1, tr, d), lambda b, i: (b, i, 0)),
        out_shape=jax.ShapeDtypeStruct((batch, seq_total - LEAD, d), F32),
        compiler_params=_params("parallel", "parallel"),
        name="final_norm",
    )(h, gain.reshape(1, d).astype(F32))


def _embed_norm_body(x_ref, meta_ref, g_ref, h_ref, hn_ref):
    lead = pl.program_id(1) == 0

    def emit(v):
        h_ref[...] = v
        hn = v * lax.rsqrt(jnp.mean(v * v, axis=-1, keepdims=True) + RMS_EPS) * g_ref[...]
        hn_ref[...] = hn.astype(hn_ref.dtype)

    @pl.when(lead)
    def _():
        d = h_ref.shape[1]
        emit(jnp.concatenate([jnp.zeros((N_PAD, d), F32), meta_ref[...]], axis=0))

    @pl.when(jnp.logical_not(lead))
    def _():
        emit(x_ref[0])


def embed_norm(x, meta_tokens, gain):
    batch, seq, d = x.shape
    assert seq % LEAD == 0
    nblk = seq // LEAD + 1
    t = batch * nblk * LEAD
    row = lambda b, i: (b * nblk + i, 0)
    return pl.pallas_call(
        _embed_norm_body,
        grid=(batch, nblk),
        in_specs=[pl.BlockSpec((1, LEAD, d), lambda b, i: (b, jnp.maximum(i - 1, 0), 0)),
                  pl.BlockSpec((N_META, d), lambda b, i: (0, 0)),
                  pl.BlockSpec((1, d), lambda b, i: (0, 0))],
        out_specs=[pl.BlockSpec((LEAD, d), row), pl.BlockSpec((LEAD, d), row)],
        out_shape=[jax.ShapeDtypeStruct((t, d), F32), jax.ShapeDtypeStruct((t, d), BF16)],
        compiler_params=_params("parallel", "arbitrary"),
        name="embed_norm",
    )(x.astype(F32), meta_tokens.astype(F32), gain.reshape(1, d).astype(F32))


def _mm_body(a_ref, w_ref, o_ref):
    o_ref[...] = jnp.dot(a_ref[...], w_ref[...].astype(BF16),
                         preferred_element_type=F32).astype(o_ref.dtype)


def matmul(a, w, *, tm, tn, out_dtype=BF16, layer=None):
    m, kdim = a.shape
    n = w.shape[-1]
    assert m % tm == 0 and n % tn == 0 and w.shape[-2] == kdim
    return pl.pallas_call(
        _mm_body,
        grid=(m // tm, n // tn),
        in_specs=[pl.BlockSpec((tm, kdim), lambda i, j: (i, 0)),
                  _wspec(w, layer, (kdim, tn), lambda i, j: (0, j))],
        out_specs=pl.BlockSpec((tm, tn), lambda i, j: (i, j)),
        out_shape=jax.ShapeDtypeStruct((m, n), out_dtype),
        compiler_params=_params("parallel", "parallel"),
        name="matmul",
    )(a, w)


def _out_proj_body(a1_ref, a2_ref, a3_ref, w_ref, r_ref, o_ref, wb_ref, *, seq_total, tm):
    i = pl.program_id(1)

    @pl.when(i == 0)
    def _():
        wb_ref[...] = w_ref[...].astype(BF16)

    k1 = a1_ref.shape[1]
    k2 = k1 + a2_ref.shape[1]
    acc = jnp.dot(a1_ref[...], wb_ref[0:k1, :], preferred_element_type=F32)
    acc += jnp.dot(a2_ref[...], wb_ref[k1:k2, :], preferred_element_type=F32)
    acc += jnp.dot(a3_ref[...], wb_ref[k2:, :], preferred_element_type=F32)
    pos = _batch_row0(i, seq_total, tm) + lax.broadcasted_iota(jnp.int32, acc.shape, 0)
    o_ref[...] = r_ref[...] + jnp.where(pos >= N_PAD, acc, 0.0)


def out_proj(a1, a2, a3, w, res, *, seq_total, tm, tn, layer=None):
    m = a1.shape[0]
    kdim, n = w.shape[-2:]
    assert a1.shape[1] + a2.shape[1] + a3.shape[1] == kdim
    assert m % tm == 0 and n % tn == 0 and seq_total % tm == 0
    a_spec = lambda a: pl.BlockSpec((tm, a.shape[1]), lambda j, i: (i, 0))
    return pl.pallas_call(
        functools.partial(_out_proj_body, seq_total=seq_total, tm=tm),
        grid=(n // tn, m // tm),
        in_specs=[a_spec(a1), a_spec(a2), a_spec(a3),
                  _wspec(w, layer, (kdim, tn), lambda j, i: (0, j)),
                  pl.BlockSpec((tm, tn), lambda j, i: (i, j))],
        out_specs=pl.BlockSpec((tm, tn), lambda j, i: (i, j)),
        out_shape=jax.ShapeDtypeStruct((m, n), F32),
        scratch_shapes=[pltpu.VMEM((kdim, tn), BF16)],
        compiler_params=_params("parallel", "arbitrary"),
        name="out_proj",
    )(a1, a2, a3, w, res)


CONV_CHUNK = 32


def _ffn_up_body(a_ref, wg_ref, wv_ref, cwg_ref, cwv_ref, cbg_ref, cbv_ref, o_ref,
                 wcat_ref, u_ref, halo_ref, *, seq_total, tm, tc):
    i = pl.program_id(0)
    j = pl.program_id(1)
    wcat_ref[:, 0:tc] = wg_ref[...].astype(BF16)
    wcat_ref[:, tc:2 * tc] = wv_ref[...].astype(BF16)
    row0 = _batch_row0(i, seq_total, tm)
    u = jnp.dot(a_ref[...], wcat_ref[...], preferred_element_type=F32)
    pos = row0 + lax.broadcasted_iota(jnp.int32, u.shape, 0)
    u_ref[SUBLANES:SUBLANES + tm, :] = jnp.where(pos >= N_PAD, u, 0.0)

    @pl.when(row0 == 0)
    def _():
        u_ref[0:SUBLANES, :] = jnp.zeros((SUBLANES, 2 * tc), F32)

    @pl.when(row0 > 0)
    def _():
        u_ref[0:SUBLANES, :] = halo_ref[j]

    halo_ref[j] = u_ref[tm:tm + SUBLANES, :]
    cw = jnp.concatenate([cwg_ref[...], cwv_ref[...]], axis=1)
    cb = jnp.concatenate([cbg_ref[...], cbv_ref[...]], axis=1)
    for r0 in range(0, tm, CONV_CHUNK):
        cv = cb
        for tap in range(CONV_W):
            lo = SUBLANES + r0 - (CONV_W - 1 - tap)
            cv = cv + cw[tap:tap + 1, :] * u_ref[lo:lo + CONV_CHUNK, :]
        o_ref[r0:r0 + CONV_CHUNK, :] = (_silu(cv[:, 0:tc]) * cv[:, tc:2 * tc]).astype(o_ref.dtype)


def ffn_up(hn, w_up, conv_w, conv_b, *, seq_total, tm, tc, d_ff, layer=None):
    m, kdim = hn.shape
    assert m % tm == 0 and seq_total % tm == 0 and tm % CONV_CHUNK == 0 and d_ff % tc == 0
    nj = d_ff // tc
    wspec = lambda off: _wspec(w_up, layer, (kdim, tc), lambda i, j: (0, j + off))
    cspec = lambda c, r, off: _wspec(c, layer, (r, tc), lambda i, j: (0, j + off))
    conv_b = conv_b.reshape(conv_b.shape[:-1] + (1, conv_b.shape[-1]))
    return pl.pallas_call(
        functools.partial(_ffn_up_body, seq_total=seq_total, tm=tm, tc=tc),
        grid=(m // tm, nj),
        in_specs=[pl.BlockSpec((tm, kdim), lambda i, j: (i, 0)),
                  wspec(0), wspec(nj), cspec(conv_w, CONV_W, 0), cspec(conv_w, CONV_W, nj),
                  cspec(conv_b, 1, 0), cspec(conv_b, 1, nj)],
        out_specs=pl.BlockSpec((tm, tc), lambda i, j: (i, j)),
        out_shape=jax.ShapeDtypeStruct((m, d_ff), BF16),
        scratch_shapes=[pltpu.VMEM((kdim, 2 * tc), BF16),
                        pltpu.VMEM((tm + SUBLANES, 2 * tc), F32),
                        pltpu.VMEM((nj, SUBLANES, 2 * tc), F32)],
        compiler_params=_params("arbitrary", "arbitrary"),
        name="ffn_up",
    )(hn, w_up, w_up, conv_w, conv_w, conv_b, conv_b)


def _ffn_down_body(a_ref, w_ref, r_ref, o_ref, wb_ref, *, seq_total, tm, tk, d_ff, nk):
    k = pl.program_id(2)
    wrow = k * tk + lax.broadcasted_iota(jnp.int32, w_ref.shape, 0)
    wb_ref[...] = jnp.where(wrow < d_ff, w_ref[...], 0.0).astype(BF16)

    def part(a):
        return jnp.dot(a, wb_ref[...], preferred_element_type=F32)

    @pl.when(k == 0)
    def _():
        o_ref[...] = part(a_ref[...])

    @pl.when(jnp.logical_and(k > 0, k < nk - 1))
    def _():
        o_ref[...] += part(a_ref[...])

    @pl.when(k == nk - 1)
    def _():
        acol = k * tk + lax.broadcasted_iota(jnp.int32, a_ref.shape, 1)
        a = jnp.where(acol < d_ff, a_ref[...], jnp.zeros(a_ref.shape, a_ref.dtype))
        pos = _batch_row0(pl.program_id(0), seq_total, tm) + lax.broadcasted_iota(
            jnp.int32, o_ref.shape, 0)
        o_ref[...] = r_ref[...] + jnp.where(pos >= N_PAD, o_ref[...] + part(a), 0.0)


def ffn_down(act, w, res, *, seq_total, tm, tn, tk, layer=None):
    m, d_ff = act.shape
    n = w.shape[-1]
    assert m % tm == 0 and n % tn == 0 and seq_total % tm == 0 and w.shape[-2] == d_ff
    nk = pl.cdiv(d_ff, tk)
    assert nk >= 2
    return pl.pallas_call(
        functools.partial(_ffn_down_body, seq_total=seq_total, tm=tm, tk=tk, d_ff=d_ff, nk=nk),
        grid=(m // tm, n // tn, nk),
        in_specs=[pl.BlockSpec((tm, tk), lambda i, j, k: (i, k)),
                  _wspec(w, layer, (tk, tn), lambda i, j, k: (k, j)),
                  pl.BlockSpec((tm, tn), lambda i, j, k: (i, j))],
        out_specs=pl.BlockSpec((tm, tn), lambda i, j, k: (i, j)),
        out_shape=jax.ShapeDtypeStruct((m, n), F32),
        scratch_shapes=[pltpu.VMEM((tk, tn), BF16)],
        compiler_params=_params("parallel", "parallel", "arbitrary"),
        name="ffn_down",
    )(act, w, res)


def _scores(k_rows, q_parts):
    out = []
    for q in q_parts:
        s = jnp.dot(k_rows, q, preferred_element_type=F32)
        out.append((s, jnp.max(s, axis=0, keepdims=True)))
    return tuple(out)


def _causal(scored, col0s):
    out = []
    for (s, _), c0 in zip(scored, col0s):
        key = lax.broadcasted_iota(jnp.int32, s.shape, 0)
        qry = c0 + lax.broadcasted_iota(jnp.int32, s.shape, 1)
        s = jnp.where(key <= qry, s, NEG_INF)
        out.append((s, jnp.max(s, axis=0, keepdims=True)))
    return tuple(out)


def _softmax_pv(carry, scored, v_ext):
    out = []
    for (m, acc), (s, s_max) in zip(carry, scored):
        m_new = jnp.maximum(m, s_max)
        alpha = jnp.exp2(m - m_new)
        p = jnp.exp2(s - m_new).astype(BF16)
        out.append((m_new, alpha * acc + jnp.dot(v_ext, p, preferred_element_type=F32)))
    return tuple(out)


def _attn_init(widths):
    return tuple((jnp.full((1, w), NEG_INF, F32), jnp.zeros((V_ROWS, w), F32)) for w in widths)


def _attn_finish(carry, gain, o_ref, row0, col0s):
    for (_, acc), c0 in zip(carry, col0s):
        o = jnp.transpose(acc[0:LANES, :] / acc[LANES:LANES + 1, :])
        o = o * lax.rsqrt(jnp.mean(o * o, axis=-1, keepdims=True) + RMS_EPS) * gain
        o_ref[pl.ds(row0 + c0, o.shape[0]), :] = o.astype(o_ref.dtype)


def _query_parts(tb):
    part = QUERY_PART if tb % QUERY_PART == 0 else tb
    return part, tuple(range(0, tb, part))


def _attention_core(qt, qt0, ks, vt, vt0, s_scr, m_scr, gain, o_ref, *, nblk, tb):
    lead = _causal(_scores(ks[0:LEAD, :], (qt0[...],)), (0,))
    _attn_finish(_softmax_pv(_attn_init((LEAD,)), lead, vt0[...]), gain, o_ref, 0, (0,))

    part, col0s = _query_parts(tb)
    widths = (part,) * len(col0s)

    def q_block(qi, _):
        q_parts = tuple(qt[qi, :, c0:c0 + part] for c0 in col0s)

        def score(slot, kj):
            k_rows = ks[pl.ds(pl.multiple_of(LEAD + kj * tb, LANES), tb), :]
            for p, (s, s_max) in enumerate(_scores(k_rows, q_parts)):
                s_scr[slot, p] = s
                m_scr[slot, p] = s_max

        def scored(slot):
            return tuple((s_scr[slot, p], m_scr[slot, p]) for p in range(len(col0s)))

        carry = _softmax_pv(_attn_init(widths), _scores(ks[0:LEAD, :], q_parts), vt0[...])
        score(0, 0)

        def pair(t, carry):
            kj = 2 * t
            score(1, kj + 1)
            carry = _softmax_pv(carry, scored(0), vt[kj])
            score(0, kj + 2)
            return _softmax_pv(carry, scored(1), vt[kj + 1])

        carry = lax.fori_loop(0, qi // 2, pair, carry)

        def odd_tail(carry):
            score(1, qi)
            carry = _softmax_pv(carry, scored(0), vt[qi - 1])
            return _softmax_pv(carry, _causal(scored(1), col0s), vt[qi])

        def even_tail(carry):
            return _softmax_pv(carry, _causal(scored(0), col0s), vt[qi])

        carry = lax.cond(qi % 2 == 1, odd_tail, even_tail, carry)
        _attn_finish(carry, gain, o_ref, pl.multiple_of(LEAD + qi * tb, LANES), col0s)
        return 0

    lax.fori_loop(0, nblk, q_block, 0)


def _store_v_ext(vt_dst, v_rows):
    n = v_rows.shape[0]
    vt_dst[0:LANES, :] = _to_t(v_rows)
    r = lax.broadcasted_iota(jnp.int32, (V_ROWS - LANES, n), 0)
    vt_dst[LANES:V_ROWS, :] = jnp.where(r == 0, 1.0, 0.0).astype(BF16)


def _to_t(x):
    return jnp.transpose(x.astype(F32)).astype(BF16)


def _mla_attn_body(q_ref, kv_ref, kr_ref, cos_ref, sa_ref, sb_ref, g_ref, o_ref,
                   qt, qt0, ks, vt, vt0, s_scr, m_scr, *, nblk, tb):
    scale = (MLA_NOPE + MLA_ROPE) ** -0.5 * LOG2E

    def rope(x, rows):
        return (x * cos_ref[rows, :] + pltpu.roll(x, HALF_ROPE, 1) * sa_ref[rows, :]
                + pltpu.roll(x, LANES - HALF_ROPE, 1) * sb_ref[rows, :])

    def prep(r0, n, qt_dst, vt_dst):
        rows = pl.ds(r0, n)
        one = lax.broadcasted_iota(jnp.int32, (n, LANES), 1) == ONE_LANE
        qn = q_ref[rows, pl.ds(0, LANES)].astype(F32) * scale
        qp = rope(q_ref[rows, pl.ds(LANES, LANES)].astype(F32), rows) * scale
        qp = jnp.where(one, 1.0, qp)
        qt_dst[0:LANES, :] = jnp.transpose(qn).astype(BF16)
        qt_dst[LANES:2 * LANES, :] = jnp.transpose(qp).astype(BF16)
        pos = r0 + lax.broadcasted_iota(jnp.int32, (n, LANES), 0)
        kp = rope(kr_ref[rows, :].astype(F32), rows)
        kp = jnp.where(one, jnp.where(pos < N_PAD, NEG_INF, 0.0), kp)
        ks[rows, pl.ds(0, LANES)] = kv_ref[rows, pl.ds(0, LANES)]
        ks[rows, pl.ds(LANES, LANES)] = kp.astype(BF16)
        _store_v_ext(vt_dst, kv_ref[rows, pl.ds(LANES, LANES)])

    prep(0, LEAD, qt0, vt0)

    def prep_block(i, _):
        prep(pl.multiple_of(LEAD + i * tb, LANES), tb, qt.at[i], vt.at[i])
        return 0

    lax.fori_loop(0, nblk, prep_block, 0)
    _attention_core(qt, qt0, ks, vt, vt0, s_scr, m_scr, g_ref[...], o_ref, nblk=nblk, tb=tb)


def _attn_scratch(seq_total, nblk, tb):
    part, col0s = _query_parts(tb)
    return [pltpu.VMEM((nblk, 2 * LANES, tb), BF16), pltpu.VMEM((2 * LANES, LEAD), BF16),
            pltpu.VMEM((seq_total, 2 * LANES), BF16),
            pltpu.VMEM((nblk, V_ROWS, tb), BF16), pltpu.VMEM((V_ROWS, LEAD), BF16),
            pltpu.VMEM((2, len(col0s), tb, part), F32), pltpu.VMEM((2, len(col0s), 1, part), F32)]


def mla_attention(q_raw, kv, proj, cos_t, sin_a, sin_b, gain, *, batch, seq_total, tb):
    t = q_raw.shape[0]
    nblk = (seq_total - LEAD) // tb
    big = lambda w: pl.BlockSpec((seq_total, w), lambda b, h: (b, h))
    table = pl.BlockSpec((seq_total, LANES), lambda b, h: (0, 0))
    return pl.pallas_call(
        functools.partial(_mla_attn_body, nblk=nblk, tb=tb),
        grid=(batch, MLA_HEADS),
        in_specs=[big(2 * LANES), big(2 * LANES),
                  pl.BlockSpec((seq_total, LANES), lambda b, h: (b, COL_KR // LANES)),
                  table, table, table,
                  pl.BlockSpec((1, LANES), lambda b, h: (0, h))],
        out_specs=big(LANES),
        out_shape=jax.ShapeDtypeStruct((t, W_MLA), BF16),
        scratch_shapes=_attn_scratch(seq_total, nblk, tb),
        compiler_params=_params("parallel", "parallel"),
        name="mla_attention",
    )(q_raw, kv, proj, cos_t, sin_a, sin_b, gain.reshape(1, W_MLA).astype(F32))


def _fox_attn_body(q_ref, k_ref, v_ref, aq_ref, ak_ref, g_ref, o_ref,
                   qt, qt0, ks, vt, vt0, s_scr, m_scr, *, nblk, tb):
    scale = FOX_DH ** -0.5 * LOG2E

    def prep(r0, n, qt_dst, vt_dst):
        rows = pl.ds(r0, n)
        qt_dst[0:LANES, :] = jnp.transpose(q_ref[rows, :].astype(F32) * scale).astype(BF16)
        qt_dst[LANES:2 * LANES, :] = _to_t(aq_ref[rows, :])
        ks[rows, pl.ds(0, LANES)] = k_ref[rows, :]
        ks[rows, pl.ds(LANES, LANES)] = ak_ref[rows, :]
        _store_v_ext(vt_dst, v_ref[rows, :])

    prep(0, LEAD, qt0, vt0)

    def prep_block(i, _):
        prep(pl.multiple_of(LEAD + i * tb, LANES), tb, qt.at[i], vt.at[i])
        return 0

    lax.fori_loop(0, nblk, prep_block, 0)
    _attention_core(qt, qt0, ks, vt, vt0, s_scr, m_scr, g_ref[...], o_ref, nblk=nblk, tb=tb)


def fox_attention(proj, aug_q, aug_k, gain, *, batch, seq_total, tb):
    t = proj.shape[0]
    nblk = (seq_total - LEAD) // tb
    col = lambda c0: pl.BlockSpec((seq_total, LANES), lambda b, h: (b, c0 // LANES + h))
    return pl.pallas_call(
        functools.partial(_fox_attn_body, nblk=nblk, tb=tb),
        grid=(batch, FOX_HEADS),
        in_specs=[col(COL_FQ), col(COL_FK), col(COL_FV), col(0), col(0),
                  pl.BlockSpec((1, LANES), lambda b, h: (0, h))],
        out_specs=col(0),
        out_shape=jax.ShapeDtypeStruct((t, W_FOX), BF16),
        scratch_shapes=_attn_scratch(seq_total, nblk, tb),
        compiler_params=_params("parallel", "parallel"),
        name="fox_attention",
    )(proj, proj, proj, aug_q, aug_k, gain.reshape(1, W_FOX).astype(F32))


def _split3(x):
    hi = x.astype(BF16)
    r1 = x - hi.astype(F32)
    mid = r1.astype(BF16)
    lo = (r1 - mid.astype(F32)).astype(BF16)
    return hi.astype(F32), mid.astype(F32), lo.astype(F32)


def _fox_prep_body(z_ref, b_ref, aq_ref, ak_ref, carry_ref, *, tr):
    i = pl.program_id(1)

    @pl.when(i == 0)
    def _():
        carry_ref[...] = jnp.zeros_like(carry_ref)

    row = i * tr + lax.broadcasted_iota(jnp.int32, (tr, LANES), 0)
    valid = row >= N_PAD
    log_f = jnp.where(valid, _log_sigmoid(z_ref[...] + b_ref[...]), 0.0)
    r = lax.broadcasted_iota(jnp.int32, (tr, tr), 0)
    c = lax.broadcasted_iota(jnp.int32, (tr, tr), 1)
    tri = jnp.where(c <= r, 1.0, 0.0).astype(F32)
    csum = jnp.dot(tri, log_f, preferred_element_type=F32,
                   precision=lax.Precision.HIGHEST) + carry_ref[0:1, :]
    carry_ref[...] = jnp.broadcast_to(csum[tr - 1:tr, :], carry_ref.shape)

    lane = lax.broadcasted_iota(jnp.int32, (tr, LANES), 1)
    key_mask = jnp.where(valid, 0.0, NEG_INF)
    for h in range(FOX_HEADS):
        col = csum[:, GATE_FZ + h:GATE_FZ + h + 1] * LOG2E
        hi, mid, lo = _split3(col)
        aq = jnp.where(lane == 0, hi, jnp.where(lane == 1, mid, jnp.where(lane == 2, lo,
             jnp.where(lane < 7, 1.0, 0.0))))
        ak = jnp.where(lane < 3, 1.0, jnp.where(lane == 3, -hi, jnp.where(lane == 4, -mid,
             jnp.where(lane == 5, -lo, jnp.where(lane == 6, key_mask, 0.0)))))
        aq_ref[:, h * LANES:(h + 1) * LANES] = aq.astype(BF16)
        ak_ref[:, h * LANES:(h + 1) * LANES] = ak.astype(BF16)


def fox_prep(gates, fox_b, *, batch, seq_total):
    t = gates.shape[0]
    tr = _pick(seq_total, 384, LANES)
    nblk = seq_total // tr
    bias = jnp.zeros((1, LANES), F32).at[0, GATE_FZ:GATE_FZ + FOX_HEADS].set(fox_b.astype(F32))
    out = jax.ShapeDtypeStruct((t, W_FOX), BF16)
    return pl.pallas_call(
        functools.partial(_fox_prep_body, tr=tr),
        grid=(batch, nblk),
        in_specs=[pl.BlockSpec((tr, LANES), lambda b, i: (b * nblk + i, 0)),
                  pl.BlockSpec((1, LANES), lambda b, i: (0, 0))],
        out_specs=[pl.BlockSpec((tr, W_FOX), lambda b, i: (b * nblk + i, 0))] * 2,
        out_shape=[out, out],
        scratch_shapes=[pltpu.VMEM((SUBLANES, LANES), F32)],
        compiler_params=_params("parallel", "arbitrary"),
        name="fox_prep",
    )(gates, bias)


def _gla_body(q_ref, k_ref, v_ref, r_ref, z_ref, w2_ref, b2_ref, g_ref, o_ref, s_ref,
              *, seq_total):
    c = GLA_CHUNK
    nchunk = seq_total // c
    heads = s_ref.shape[0]
    nt = (((1,), (1,)), ((), ()))
    tn = (((0,), (0,)), ((), ()))
    r_i = lax.broadcasted_iota(jnp.int32, (c, c), 0)
    c_i = lax.broadcasted_iota(jnp.int32, (c, c), 1)
    lower = c_i <= r_i
    tri = jnp.where(lower, 1.0, 0.0).astype(F32)
    s_ref[...] = jnp.zeros_like(s_ref)

    def chunk(i, _):
        r0 = pl.multiple_of(i * c, c)
        rows = pl.ds(r0, c)
        valid = (r0 + lax.broadcasted_iota(jnp.int32, (c, GLA_DK), 0)) >= N_PAD
        z = z_ref[rows, :].astype(BF16)
        for hh in range(heads):
            kcols = pl.ds(hh * GLA_DK, GLA_DK)
            vcols = pl.ds(hh * GLA_DV, GLA_DV)
            q = q_ref[rows, kcols].astype(F32) * (GLA_DK ** -0.5)
            k = jnp.where(valid, k_ref[rows, kcols].astype(F32), 0.0)
            v = v_ref[rows, vcols]
            logit = jnp.dot(z, w2_ref[:, kcols], preferred_element_type=F32) + b2_ref[:, kcols]
            log_a = jnp.where(valid, _log_sigmoid(logit) / GLA_TAU, 0.0)
            bc = jnp.dot(tri, log_a, preferred_element_type=F32, precision=lax.Precision.HIGHEST)
            b_last = bc[c - 1:c, :]
            q_dec = (q * jnp.exp(bc)).astype(BF16)
            k_dec = (k * jnp.exp(-bc)).astype(BF16)
            a = lax.dot_general(q_dec, k_dec, nt, preferred_element_type=F32)
            a = jnp.where(lower, a, 0.0).astype(BF16)
            state = s_ref[hh]
            o = (jnp.dot(a, v, preferred_element_type=F32)
                 + jnp.dot(q_dec, state.astype(BF16), preferred_element_type=F32))
            k_state = (k * jnp.exp(b_last - bc)).astype(BF16)
            decay = jnp.exp(b_last)
            dec_t = jnp.transpose(jnp.broadcast_to(decay, (GLA_DK, GLA_DK)))
            dec_t = jnp.concatenate([dec_t, dec_t], axis=1)
            s_ref[hh] = state * dec_t + lax.dot_general(k_state, v, tn, preferred_element_type=F32)
            o = o * lax.rsqrt(jnp.mean(o * o, axis=-1, keepdims=True) + RMS_EPS) * g_ref[:, vcols]
            o_ref[rows, vcols] = (o * _silu(r_ref[rows, vcols].astype(F32))).astype(o_ref.dtype)
        return 0

    lax.fori_loop(0, nchunk, chunk, 0)


def gla(proj, gates, w2, b2, gain, *, batch, seq_total):
    t = proj.shape[0]
    hps = GLA_HEADS_PER_STEP
    assert GLA_HEADS % hps == 0
    wk, wv = hps * GLA_DK, hps * GLA_DV
    blk = lambda w, c0: pl.BlockSpec((seq_total, w), lambda b, h: (b, c0 // w + h))
    return pl.pallas_call(
        functools.partial(_gla_body, seq_total=seq_total),
        grid=(batch, GLA_HEADS // hps),
        in_specs=[blk(wk, COL_GQ), blk(wk, COL_GK), blk(wv, COL_GV), blk(wv, COL_GR),
                  pl.BlockSpec((seq_total, LANES), lambda b, h: (b, 0)),
                  pl.BlockSpec((LANES, wk), lambda b, h: (0, h)),
                  pl.BlockSpec((1, wk), lambda b, h: (0, h)),
                  pl.BlockSpec((1, wv), lambda b, h: (0, h))],
        out_specs=pl.BlockSpec((seq_total, wv), lambda b, h: (b, h)),
        out_shape=jax.ShapeDtypeStruct((t, W_GLA), BF16),
        scratch_shapes=[pltpu.VMEM((hps, GLA_DK, GLA_DV), F32)],
        compiler_params=_params("parallel", "parallel"),
        name="gla",
    )(proj, proj, proj, proj, gates, w2, b2, gain.reshape(1, W_GLA).astype(F32))


def _rope_tables(seq_total):
    pos = np.maximum(np.arange(seq_total) - N_PAD, 0).astype(np.float32)
    inv_freq = (1.0 / (ROPE_THETA ** (np.arange(0, MLA_ROPE, 2, dtype=np.float32) / MLA_ROPE))).astype(np.float32)
    ang = jnp.asarray(pos)[:, None] * jnp.asarray(inv_freq)[None, :]
    cos, sin = jnp.cos(ang), jnp.sin(ang)
    z = jnp.zeros_like(cos)
    cos_t = jnp.concatenate([cos, cos, z, z], axis=1)
    sin_a = jnp.concatenate([z, sin, z, z], axis=1)
    sin_b = jnp.concatenate([-sin, z, z, z], axis=1)
    return cos_t, sin_a, sin_b


def kernel(x, meta_tokens, attn_norm, w_in, mla_q_norm, mla_w_uq, mla_kv_norm, mla_w_ukv,
           gla_w_gate2, gla_b_gate, fox_b_f, out_norm_mla, out_norm_gla, out_norm_fox,
           w_out, ffn_norm, ffn_w_up, ffn_conv_w, ffn_conv_b, ffn_w_down, final_norm):
    batch, seq, d_model = x.shape
    depth = w_in.shape[0]
    d_ff = ffn_w_down.shape[1]
    seq_total = LEAD + seq
    cos_t, sin_a, sin_b = _rope_tables(seq_total)

    tm_big = _pick(seq_total, TM_BIG, LANES)
    tm_mid = _pick(seq_total, TM_MID, LANES)
    tb = _pick(seq, ATTN_BLOCK, LANES)
    tc = _pick(d_ff, 256, LANES)

    h, hn = embed_norm(x, meta_tokens, attn_norm[0])
    for layer in range(depth):
        w_big, w_gates = relayout_cols(w_in, (W_IN_PLAN, W_GATE_PLAN), (N_PROJ, LANES), layer=layer)
        w_uq, = relayout_cols(mla_w_uq, (W_UQ_PLAN,), (MLA_HEADS * 2 * LANES,), layer=layer)
        if layer > 0:
            hn = rmsnorm(h, attn_norm[layer])
        proj = matmul(hn, w_big, tm=tm_big, tn=_pick(N_PROJ, 896, LANES))
        gates = matmul(hn, w_gates, tm=tm_big, tn=LANES, out_dtype=F32)

        cq = rmsnorm(proj, mla_q_norm[layer], col0=COL_CQ, width=MLA_Q_LORA)
        ckv = rmsnorm(proj, mla_kv_norm[layer], col0=COL_CKV, width=MLA_KV_LORA)
        q_raw = matmul(cq, w_uq, tm=tm_big, tn=1024)
        kv = matmul(ckv, mla_w_ukv, tm=tm_big, tn=1024, layer=layer)
        o_mla = mla_attention(q_raw, kv, proj, cos_t, sin_a, sin_b, out_norm_mla[layer],
                              batch=batch, seq_total=seq_total, tb=tb)

        w2 = jnp.zeros((LANES, W_GQK), F32).at[:GLA_GATE_RANK].set(gla_w_gate2[layer])
        o_gla = gla(proj, gates, w2.astype(BF16), gla_b_gate[layer].reshape(1, -1),
                    out_norm_gla[layer], batch=batch, seq_total=seq_total)

        aug_q, aug_k = fox_prep(gates, fox_b_f[layer], batch=batch, seq_total=seq_total)
        o_fox = fox_attention(proj, aug_q, aug_k, out_norm_fox[layer],
                              batch=batch, seq_total=seq_total, tb=tb)

        h = out_proj(o_mla, o_gla, o_fox, w_out, h, seq_total=seq_total, tm=tm_mid, tn=512,
                     layer=layer)

        hn = rmsnorm(h, ffn_norm[layer])
        act = ffn_up(hn, ffn_w_up, ffn_conv_w, ffn_conv_b, seq_total=seq_total, tm=tm_big,
                     tc=tc, d_ff=d_ff, layer=layer)
        h = ffn_down(act, ffn_w_down, h, seq_total=seq_total, tm=tm_big, tn=1024, tk=DOWN_TK,
                     layer=layer)

    return final_rmsnorm(h, final_norm, batch, seq_total)
```

```python
import functools

import jax
import jax.numpy as jnp
import numpy as np
from jax import lax
from jax.experimental import pallas as pl
from jax.experimental.pallas import tpu as pltpu

F32 = jnp.float32
BF16 = jnp.bfloat16

N_META = 16
LEAD = 128
N_PAD = LEAD - N_META
RMS_EPS = 1e-6
NEG_INF = -1e30

MLA_V = 128
MLA_HEADS = 12
MLA_NOPE = 128
MLA_ROPE = 64
MLA_Q_LORA = 1536
MLA_KV_LORA = 512
ROPE_THETA = 10000.0

GLA_DV = 256
GLA_DK = 128
GLA_HEADS = 4
GLA_GATE_RANK = 16
GLA_TAU = 16.0
GLA_CHUNK = 64

FOX_DH = 128
FOX_HEADS = 12

W_MLA = MLA_HEADS * MLA_V
W_GLA = GLA_HEADS * GLA_DV
W_FOX = FOX_HEADS * FOX_DH
CONV_W = 3

LANES = 128
SUBLANES = 8
VMEM_LIMIT_BYTES = 56 * 1024 * 1024

TM_BIG = 1408
TM_MID = 704
ATTN_BLOCK = 512
DOWN_TK = 1024
GLA_HEADS_PER_STEP = 2
QUERY_PART = 256

HALF_ROPE = MLA_ROPE // 2
W_GQK = GLA_HEADS * GLA_DK
COL_CQ = 0
COL_CKV = COL_CQ + MLA_Q_LORA
COL_GV = COL_CKV + MLA_KV_LORA
COL_GR = COL_GV + W_GLA
COL_GQ = COL_GR + W_GLA
COL_GK = COL_GQ + W_GQK
COL_FQ = COL_GK + W_GQK
COL_FK = COL_FQ + W_FOX
COL_FV = COL_FK + W_FOX
COL_KR = COL_FV + W_FOX
N_PROJ = COL_KR + LANES
SRC_CQ = 0
SRC_CKV = SRC_CQ + MLA_Q_LORA
SRC_KR = SRC_CKV + MLA_KV_LORA
SRC_GQ = SRC_KR + MLA_ROPE
SRC_GK = SRC_GQ + W_GQK
SRC_GV = SRC_GK + W_GQK
SRC_GZ = SRC_GV + W_GLA
SRC_GR = SRC_GZ + GLA_GATE_RANK
SRC_FQ = SRC_GR + W_GLA
SRC_FK = SRC_FQ + W_FOX
SRC_FV = SRC_FK + W_FOX
SRC_FZ = SRC_FV + W_FOX
GATE_FZ = GLA_GATE_RANK
ONE_LANE = MLA_ROPE
LOG2E = 1.4426950408889634
V_ROWS = LANES + 16


def _pick(n, target, mult):
    best = None
    for d in range(mult, min(n, target) + 1, mult):
        if n % d == 0:
            best = d
    assert best is not None, (n, target, mult)
    return best


def _params(*sem):
    return pltpu.CompilerParams(dimension_semantics=sem, vmem_limit_bytes=VMEM_LIMIT_BYTES)


def _log_sigmoid(x):
    return jnp.minimum(x, 0.0) - jnp.log1p(jnp.exp(-jnp.abs(x)))


def _silu(x):
    return x / (1.0 + jnp.exp(-x))


def _wspec(w, layer, block, index, **kw):
    if layer is None:
        assert w.ndim == len(block)
        return pl.BlockSpec(block, index, **kw)
    assert w.ndim == len(block) + 1
    return pl.BlockSpec((None,) + tuple(block), lambda *g: (layer,) + tuple(index(*g)), **kw)


def _batch_row0(step, seq_total, tm):
    return (step % (seq_total // tm)) * tm


def _relayout_body(x_ref, *o_refs, plans):
    for o_ref, plan in zip(o_refs, plans):
        width = o_ref.shape[1]
        pos = 0
        for src, dst, w in plan:
            assert dst >= pos
            if dst > pos:
                o_ref[:, pos:dst] = jnp.zeros((o_ref.shape[0], dst - pos), o_ref.dtype)
            o_ref[:, dst:dst + w] = x_ref[:, src:src + w].astype(o_ref.dtype)
            pos = dst + w
        if pos < width:
            o_ref[:, pos:width] = jnp.zeros((o_ref.shape[0], width - pos), o_ref.dtype)


def relayout_cols(w, plans, widths, *, layer=None, rows=256):
    k, n = w.shape[-2:]
    tr = _pick(k, rows, 16)
    return pl.pallas_call(
        functools.partial(_relayout_body, plans=plans),
        grid=(k // tr,),
        in_specs=[_wspec(w, layer, (tr, n), lambda i: (i, 0))],
        out_specs=[pl.BlockSpec((tr, wd), lambda i: (i, 0)) for wd in widths],
        out_shape=[jax.ShapeDtypeStruct((k, wd), BF16) for wd in widths],
        compiler_params=_params("parallel"),
        name="relayout_cols",
    )(w)


def _transpose_cast_body(x_ref, o_ref):
    for layer in range(x_ref.shape[1]):
        o_ref[layer] = jnp.transpose(x_ref[:, layer, :]).astype(o_ref.dtype)


def transpose_cast(wt):
    n, depth, k = wt.shape
    nb = pl.cdiv(n, LANES)
    return pl.pallas_call(
        _transpose_cast_body,
        grid=(nb,),
        in_specs=[pl.BlockSpec((LANES, depth, k), lambda i: (i, 0, 0))],
        out_specs=pl.BlockSpec((depth, k, LANES), lambda i: (0, 0, i)),
        out_shape=jax.ShapeDtypeStruct((depth, k, nb * LANES), BF16),
        compiler_params=_params("parallel"),
        name="transpose_cast",
    )(wt)


W_IN_PLAN = sorted([
    (SRC_CQ, COL_CQ, MLA_Q_LORA), (SRC_CKV, COL_CKV, MLA_KV_LORA), (SRC_KR, COL_KR, MLA_ROPE),
    (SRC_GQ, COL_GQ, W_GQK), (SRC_GK, COL_GK, W_GQK), (SRC_GV, COL_GV, W_GLA),
    (SRC_GR, COL_GR, W_GLA), (SRC_FQ, COL_FQ, W_FOX), (SRC_FK, COL_FK, W_FOX),
    (SRC_FV, COL_FV, W_FOX)], key=lambda p: p[1])
W_GATE_PLAN = [(SRC_GZ, 0, GLA_GATE_RANK), (SRC_FZ, GATE_FZ, FOX_HEADS)]
W_UQ_PLAN = [(h * (MLA_NOPE + MLA_ROPE), h * 2 * LANES, MLA_NOPE + MLA_ROPE) for h in range(MLA_HEADS)]


def _rmsnorm_body(x_ref, g_ref, o_ref):
    x = x_ref[...].astype(F32)
    y = x * lax.rsqrt(jnp.mean(x * x, axis=-1, keepdims=True) + RMS_EPS)
    o_ref[...] = (y * g_ref[...]).astype(o_ref.dtype)


def rmsnorm(x, gain, *, col0=0, width=None, out_dtype=BF16, rows=256):
    m = x.shape[0]
    width = x.shape[1] if width is None else width
    assert col0 % width == 0
    tr = _pick(m, rows, 16)
    return pl.pallas_call(
        _rmsnorm_body,
        grid=(m // tr,),
        in_specs=[pl.BlockSpec((tr, width), lambda i: (i, col0 // width)),
                  pl.BlockSpec((1, width), lambda i: (0, 0))],
        out_specs=pl.BlockSpec((tr, width), lambda i: (i, 0)),
        out_shape=jax.ShapeDtypeStruct((m, width), out_dtype),
        compiler_params=_params("parallel"),
        name="rmsnorm",
    )(x, gain.reshape(1, width).astype(F32))


def _final_norm_body(x_ref, g_ref, o_ref):
    x = x_ref[...]
    y = x * lax.rsqrt(jnp.mean(x * x, axis=-1, keepdims=True) + RMS_EPS)
    o_ref[0] = y * g_ref[...]


def final_rmsnorm(h, gain, batch, seq_total):
    d = h.shape[1]
    tr = LEAD
    nblk = seq_total // tr
    return pl.pallas_call(
        _final_norm_body,
        grid=(batch, nblk - 1),
        in_specs=[pl.BlockSpec((tr, d), lambda b, i: (b * nblk + i + 1, 0)),
                  pl.BlockSpec((1, d), lambda b, i: (0, 0))],
        out_specs=pl.BlockSpec((1, tr, d), lambda b, i: (b, i, 0)),
        out_shape=jax.ShapeDtypeStruct((batch, seq_total - LEAD, d), F32),
        compiler_params=_params("parallel", "parallel"),
        name="final_norm",
    )(h, gain.reshape(1, d).astype(F32))


def _embed_norm_body(x_ref, meta_ref, g_ref, h_ref, hn_ref):
    lead = pl.program_id(1) == 0

    def emit(v):
        h_ref[...] = v
        hn = v * lax.rsqrt(jnp.mean(v * v, axis=-1, keepdims=True) + RMS_EPS) * g_ref[...]
        hn_ref[...] = hn.astype(hn_ref.dtype)

    @pl.when(lead)
    def _():
        d = h_ref.shape[1]
        emit(jnp.concatenate([jnp.zeros((N_PAD, d), F32), meta_ref[...]], axis=0))

    @pl.when(jnp.logical_not(lead))
    def _():
        emit(x_ref[0])


def embed_norm(x, meta_tokens, gain):
    batch, seq, d = x.shape
    assert seq % LEAD == 0
    nblk = seq // LEAD + 1
    t = batch * nblk * LEAD
    row = lambda b, i: (b * nblk + i, 0)
    return pl.pallas_call(
        _embed_norm_body,
        grid=(batch, nblk),
        in_specs=[pl.BlockSpec((1, LEAD, d), lambda b, i: (b, jnp.maximum(i - 1, 0), 0)),
                  pl.BlockSpec((N_META, d), lambda b, i: (0, 0)),
                  pl.BlockSpec((1, d), lambda b, i: (0, 0))],
        out_specs=[pl.BlockSpec((LEAD, d), row), pl.BlockSpec((LEAD, d), row)],
        out_shape=[jax.ShapeDtypeStruct((t, d), F32), jax.ShapeDtypeStruct((t, d), BF16)],
        compiler_params=_params("parallel", "arbitrary"),
        name="embed_norm",
    )(x.astype(F32), meta_tokens.astype(F32), gain.reshape(1, d).astype(F32))


def _mm_body(a_ref, w_ref, o_ref):
    o_ref[...] = jnp.dot(a_ref[...], w_ref[...].astype(BF16),
                         preferred_element_type=F32).astype(o_ref.dtype)


def matmul(a, w, *, tm, tn, out_dtype=BF16, layer=None):
    m, kdim = a.shape
    n = w.shape[-1]
    assert m % tm == 0 and n % tn == 0 and w.shape[-2] == kdim
    return pl.pallas_call(
        _mm_body,
        grid=(m // tm, n // tn),
        in_specs=[pl.BlockSpec((tm, kdim), lambda i, j: (i, 0)),
                  _wspec(w, layer, (kdim, tn), lambda i, j: (0, j))],
        out_specs=pl.BlockSpec((tm, tn), lambda i, j: (i, j)),
        out_shape=jax.ShapeDtypeStruct((m, n), out_dtype),
        compiler_params=_params("parallel", "parallel"),
        name="matmul",
    )(a, w)


def _out_proj_body(a1_ref, a2_ref, a3_ref, w_ref, r_ref, o_ref, wb_ref, *, seq_total, tm):
    i = pl.program_id(1)

    @pl.when(i == 0)
    def _():
        wb_ref[...] = w_ref[...].astype(BF16)

    k1 = a1_ref.shape[1]
    k2 = k1 + a2_ref.shape[1]
    acc = jnp.dot(a1_ref[...], wb_ref[0:k1, :], preferred_element_type=F32)
    acc += jnp.dot(a2_ref[...], wb_ref[k1:k2, :], preferred_element_type=F32)
    acc += jnp.dot(a3_ref[...], wb_ref[k2:, :], preferred_element_type=F32)
    pos = _batch_row0(i, seq_total, tm) + lax.broadcasted_iota(jnp.int32, acc.shape, 0)
    o_ref[...] = r_ref[...] + jnp.where(pos >= N_PAD, acc, 0.0)


def out_proj(a1, a2, a3, w, res, *, seq_total, tm, tn, layer=None):
    m = a1.shape[0]
    kdim, n = w.shape[-2:]
    assert a1.shape[1] + a2.shape[1] + a3.shape[1] == kdim
    assert m % tm == 0 and n % tn == 0 and seq_total % tm == 0
    a_spec = lambda a: pl.BlockSpec((tm, a.shape[1]), lambda j, i: (i, 0))
    return pl.pallas_call(
        functools.partial(_out_proj_body, seq_total=seq_total, tm=tm),
        grid=(n // tn, m // tm),
        in_specs=[a_spec(a1), a_spec(a2), a_spec(a3),
                  _wspec(w, layer, (kdim, tn), lambda j, i: (0, j), pipeline_mode=pl.Buffered(1)),
                  pl.BlockSpec((tm, tn), lambda j, i: (i, j))],
        out_specs=pl.BlockSpec((tm, tn), lambda j, i: (i, j)),
        out_shape=jax.ShapeDtypeStruct((m, n), F32),
        scratch_shapes=[pltpu.VMEM((kdim, tn), BF16)],
        compiler_params=_params("parallel", "arbitrary"),
        name="out_proj",
    )(a1, a2, a3, w, res)


CONV_CHUNK = 32


def _ffn_up_body(a_ref, wg_ref, wv_ref, cwg_ref, cwv_ref, cbg_ref, cbv_ref, o_ref,
                 wcat_ref, u_ref, halo_ref, *, seq_total, tm, tc):
    i = pl.program_id(0)
    j = pl.program_id(1)
    wcat_ref[:, 0:tc] = wg_ref[...].astype(BF16)
    wcat_ref[:, tc:2 * tc] = wv_ref[...].astype(BF16)
    row0 = _batch_row0(i, seq_total, tm)
    u = jnp.dot(a_ref[...], wcat_ref[...], preferred_element_type=F32)
    pos = row0 + lax.broadcasted_iota(jnp.int32, u.shape, 0)
    u_ref[SUBLANES:SUBLANES + tm, :] = jnp.where(pos >= N_PAD, u, 0.0)

    @pl.when(row0 == 0)
    def _():
        u_ref[0:SUBLANES, :] = jnp.zeros((SUBLANES, 2 * tc), F32)

    @pl.when(row0 > 0)
    def _():
        u_ref[0:SUBLANES, :] = halo_ref[j]

    halo_ref[j] = u_ref[tm:tm + SUBLANES, :]
    cw = jnp.concatenate([cwg_ref[...], cwv_ref[...]], axis=1)
    cb = jnp.concatenate([cbg_ref[...], cbv_ref[...]], axis=1)
    for r0 in range(0, tm, CONV_CHUNK):
        cv = cb
        for tap in range(CONV_W):
            lo = SUBLANES + r0 - (CONV_W - 1 - tap)
            cv = cv + cw[tap:tap + 1, :] * u_ref[lo:lo + CONV_CHUNK, :]
        o_ref[r0:r0 + CONV_CHUNK, :] = (_silu(cv[:, 0:tc]) * cv[:, tc:2 * tc]).astype(o_ref.dtype)


def ffn_up(hn, w_up, conv_w, conv_b, *, seq_total, tm, tc, d_ff, layer=None):
    m, kdim = hn.shape
    assert m % tm == 0 and seq_total % tm == 0 and tm % CONV_CHUNK == 0 and d_ff % tc == 0
    nj = d_ff // tc
    wspec = lambda off: _wspec(w_up, layer, (kdim, tc), lambda i, j: (0, j + off))
    cspec = lambda c, r, off: _wspec(c, layer, (r, tc), lambda i, j: (0, j + off))
    conv_b = conv_b.reshape(conv_b.shape[:-1] + (1, conv_b.shape[-1]))
    return pl.pallas_call(
        functools.partial(_ffn_up_body, seq_total=seq_total, tm=tm, tc=tc),
        grid=(m // tm, nj),
        in_specs=[pl.BlockSpec((tm, kdim), lambda i, j: (i, 0)),
                  wspec(0), wspec(nj), cspec(conv_w, CONV_W, 0), cspec(conv_w, CONV_W, nj),
                  cspec(conv_b, 1, 0), cspec(conv_b, 1, nj)],
        out_specs=pl.BlockSpec((tm, tc), lambda i, j: (i, j)),
        out_shape=jax.ShapeDtypeStruct((m, d_ff), BF16),
        scratch_shapes=[pltpu.VMEM((kdim, 2 * tc), BF16),
                        pltpu.VMEM((tm + SUBLANES, 2 * tc), F32),
                        pltpu.VMEM((nj, SUBLANES, 2 * tc), F32)],
        compiler_params=_params("arbitrary", "arbitrary"),
        name="ffn_up",
    )(hn, w_up, w_up, conv_w, conv_w, conv_b, conv_b)


def _ffn_down_body(a_ref, w_ref, r_ref, o_ref, wb_ref, *, seq_total, tm, tk, d_ff, nk):
    k = pl.program_id(2)
    wrow = k * tk + lax.broadcasted_iota(jnp.int32, w_ref.shape, 0)
    wb_ref[...] = jnp.where(wrow < d_ff, w_ref[...], 0.0).astype(BF16)

    def part(a):
        return jnp.dot(a, wb_ref[...], preferred_element_type=F32)

    @pl.when(k == 0)
    def _():
        o_ref[...] = part(a_ref[...])

    @pl.when(jnp.logical_and(k > 0, k < nk - 1))
    def _():
        o_ref[...] += part(a_ref[...])

    @pl.when(k == nk - 1)
    def _():
        acol = k * tk + lax.broadcasted_iota(jnp.int32, a_ref.shape, 1)
        a = jnp.where(acol < d_ff, a_ref[...], jnp.zeros(a_ref.shape, a_ref.dtype))
        pos = _batch_row0(pl.program_id(0), seq_total, tm) + lax.broadcasted_iota(
            jnp.int32, o_ref.shape, 0)
        o_ref[...] = r_ref[...] + jnp.where(pos >= N_PAD, o_ref[...] + part(a), 0.0)


def ffn_down(act, w, res, *, seq_total, tm, tn, tk, layer=None):
    m, d_ff = act.shape
    n = w.shape[-1]
    assert m % tm == 0 and n % tn == 0 and seq_total % tm == 0 and w.shape[-2] == d_ff
    nk = pl.cdiv(d_ff, tk)
    assert nk >= 2
    return pl.pallas_call(
        functools.partial(_ffn_down_body, seq_total=seq_total, tm=tm, tk=tk, d_ff=d_ff, nk=nk),
        grid=(m // tm, n // tn, nk),
        in_specs=[pl.BlockSpec((tm, tk), lambda i, j, k: (i, k)),
                  _wspec(w, layer, (tk, tn), lambda i, j, k: (k, j)),
                  pl.BlockSpec((tm, tn), lambda i, j, k: (i, j))],
        out_specs=pl.BlockSpec((tm, tn), lambda i, j, k: (i, j)),
        out_shape=jax.ShapeDtypeStruct((m, n), F32),
        scratch_shapes=[pltpu.VMEM((tk, tn), BF16)],
        compiler_params=_params("parallel", "parallel", "arbitrary"),
        name="ffn_down",
    )(act, w, res)


def _scores(k_rows, q_parts):
    out = []
    for q in q_parts:
        s = jnp.dot(k_rows, q, preferred_element_type=F32)
        out.append((s, jnp.max(s, axis=0, keepdims=True)))
    return tuple(out)


def _causal(scored, col0s):
    out = []
    for (s, _), c0 in zip(scored, col0s):
        key = lax.broadcasted_iota(jnp.int32, s.shape, 0)
        qry = c0 + lax.broadcasted_iota(jnp.int32, s.shape, 1)
        s = jnp.where(key <= qry, s, NEG_INF)
        out.append((s, jnp.max(s, axis=0, keepdims=True)))
    return tuple(out)


def _softmax_pv(carry, scored, v_ext):
    out = []
    for (m, acc), (s, s_max) in zip(carry, scored):
        m_new = jnp.maximum(m, s_max)
        alpha = jnp.exp2(m - m_new)
        p = jnp.exp2(s - m_new).astype(BF16)
        out.append((m_new, alpha * acc + jnp.dot(v_ext, p, preferred_element_type=F32)))
    return tuple(out)


def _attn_init(widths):
    return tuple((jnp.full((1, w), NEG_INF, F32), jnp.zeros((V_ROWS, w), F32)) for w in widths)


def _attn_finish(carry, gain, o_ref, row0, col0s):
    for (_, acc), c0 in zip(carry, col0s):
        o = jnp.transpose(acc[0:LANES, :] / acc[LANES:LANES + 1, :])
        o = o * lax.rsqrt(jnp.mean(o * o, axis=-1, keepdims=True) + RMS_EPS) * gain
        o_ref[pl.ds(row0 + c0, o.shape[0]), :] = o.astype(o_ref.dtype)


def _query_parts(tb):
    part = QUERY_PART if tb % QUERY_PART == 0 else tb
    return part, tuple(range(0, tb, part))


def _attention_core(qt, qt0, ks, vt, vt0, s_scr, m_scr, gain, o_ref, *, nblk, tb):
    lead = _causal(_scores(ks[0:LEAD, :], (qt0[...],)), (0,))
    _attn_finish(_softmax_pv(_attn_init((LEAD,)), lead, vt0[...]), gain, o_ref, 0, (0,))

    part, col0s = _query_parts(tb)
    widths = (part,) * len(col0s)

    def q_block(qi, _):
        q_parts = tuple(qt[qi, :, c0:c0 + part] for c0 in col0s)

        def score(slot, kj):
            k_rows = ks[pl.ds(pl.multiple_of(LEAD + kj * tb, LANES), tb), :]
            for p, (s, s_max) in enumerate(_scores(k_rows, q_parts)):
                s_scr[slot, p] = s
                m_scr[slot, p] = s_max

        def scored(slot):
            return tuple((s_scr[slot, p], m_scr[slot, p]) for p in range(len(col0s)))

        carry = _softmax_pv(_attn_init(widths), _scores(ks[0:LEAD, :], q_parts), vt0[...])
        score(0, 0)

        def pair(t, carry):
            kj = 2 * t
            score(1, kj + 1)
            carry = _softmax_pv(carry, scored(0), vt[kj])
            score(0, kj + 2)
            return _softmax_pv(carry, scored(1), vt[kj + 1])

        carry = lax.fori_loop(0, qi // 2, pair, carry)

        def odd_tail(carry):
            score(1, qi)
            carry = _softmax_pv(carry, scored(0), vt[qi - 1])
            return _softmax_pv(carry, _causal(scored(1), col0s), vt[qi])

        def even_tail(carry):
            return _softmax_pv(carry, _causal(scored(0), col0s), vt[qi])

        carry = lax.cond(qi % 2 == 1, odd_tail, even_tail, carry)
        _attn_finish(carry, gain, o_ref, pl.multiple_of(LEAD + qi * tb, LANES), col0s)
        return 0

    lax.fori_loop(0, nblk, q_block, 0)


def _store_v_ext(vt_dst, v_rows):
    n = v_rows.shape[0]
    vt_dst[0:LANES, :] = _to_t(v_rows)
    r = lax.broadcasted_iota(jnp.int32, (V_ROWS - LANES, n), 0)
    vt_dst[LANES:V_ROWS, :] = jnp.where(r == 0, 1.0, 0.0).astype(BF16)


def _to_t(x):
    return jnp.transpose(x.astype(F32)).astype(BF16)


def _mla_attn_body(q_ref, kv_ref, kr_ref, cos_ref, sa_ref, sb_ref, g_ref, o_ref,
                   qt, qt0, ks, vt, vt0, s_scr, m_scr, *, nblk, tb):
    scale = (MLA_NOPE + MLA_ROPE) ** -0.5 * LOG2E

    def rope(x, rows):
        return (x * cos_ref[rows, :] + pltpu.roll(x, HALF_ROPE, 1) * sa_ref[rows, :]
                + pltpu.roll(x, LANES - HALF_ROPE, 1) * sb_ref[rows, :])

    def prep(r0, n, qt_dst, vt_dst):
        rows = pl.ds(r0, n)
        one = lax.broadcasted_iota(jnp.int32, (n, LANES), 1) == ONE_LANE
        qn = q_ref[rows, pl.ds(0, LANES)].astype(F32) * scale
        qp = rope(q_ref[rows, pl.ds(LANES, LANES)].astype(F32), rows) * scale
        qp = jnp.where(one, 1.0, qp)
        qt_dst[0:LANES, :] = jnp.transpose(qn).astype(BF16)
        qt_dst[LANES:2 * LANES, :] = jnp.transpose(qp).astype(BF16)
        pos = r0 + lax.broadcasted_iota(jnp.int32, (n, LANES), 0)
        kp = rope(kr_ref[rows, :].astype(F32), rows)
        kp = jnp.where(one, jnp.where(pos < N_PAD, NEG_INF, 0.0), kp)
        ks[rows, pl.ds(0, LANES)] = kv_ref[rows, pl.ds(0, LANES)]
        ks[rows, pl.ds(LANES, LANES)] = kp.astype(BF16)
        _store_v_ext(vt_dst, kv_ref[rows, pl.ds(LANES, LANES)])

    prep(0, LEAD, qt0, vt0)

    def prep_block(i, _):
        prep(pl.multiple_of(LEAD + i * tb, LANES), tb, qt.at[i], vt.at[i])
        return 0

    lax.fori_loop(0, nblk, prep_block, 0)
    _attention_core(qt, qt0, ks, vt, vt0, s_scr, m_scr, g_ref[...], o_ref, nblk=nblk, tb=tb)


def _attn_scratch(seq_total, nblk, tb):
    part, col0s = _query_parts(tb)
    return [pltpu.VMEM((nblk, 2 * LANES, tb), BF16), pltpu.VMEM((2 * LANES, LEAD), BF16),
            pltpu.VMEM((seq_total, 2 * LANES), BF16),
            pltpu.VMEM((nblk, V_ROWS, tb), BF16), pltpu.VMEM((V_ROWS, LEAD), BF16),
            pltpu.VMEM((2, len(col0s), tb, part), F32), pltpu.VMEM((2, len(col0s), 1, part), F32)]


def mla_attention(q_raw, kv, proj, cos_t, sin_a, sin_b, gain, *, batch, seq_total, tb):
    t = q_raw.shape[0]
    nblk = (seq_total - LEAD) // tb
    big = lambda w: pl.BlockSpec((seq_total, w), lambda b, h: (b, h))
    table = pl.BlockSpec((seq_total, LANES), lambda b, h: (0, 0))
    return pl.pallas_call(
        functools.partial(_mla_attn_body, nblk=nblk, tb=tb),
        grid=(batch, MLA_HEADS),
        in_specs=[big(2 * LANES), big(2 * LANES),
                  pl.BlockSpec((seq_total, LANES), lambda b, h: (b, COL_KR // LANES)),
                  table, table, table,
                  pl.BlockSpec((1, LANES), lambda b, h: (0, h))],
        out_specs=big(LANES),
        out_shape=jax.ShapeDtypeStruct((t, W_MLA), BF16),
        scratch_shapes=_attn_scratch(seq_total, nblk, tb),
        compiler_params=_params("parallel", "parallel"),
        name="mla_attention",
    )(q_raw, kv, proj, cos_t, sin_a, sin_b, gain.reshape(1, W_MLA).astype(F32))


def _fox_attn_body(q_ref, k_ref, v_ref, aq_ref, ak_ref, g_ref, o_ref,
                   qt, qt0, ks, vt, vt0, s_scr, m_scr, *, nblk, tb):
    scale = FOX_DH ** -0.5 * LOG2E

    def prep(r0, n, qt_dst, vt_dst):
        rows = pl.ds(r0, n)
        qt_dst[0:LANES, :] = jnp.transpose(q_ref[rows, :].astype(F32) * scale).astype(BF16)
        qt_dst[LANES:2 * LANES, :] = _to_t(aq_ref[rows, :])
        ks[rows, pl.ds(0, LANES)] = k_ref[rows, :]
        ks[rows, pl.ds(LANES, LANES)] = ak_ref[rows, :]
        _store_v_ext(vt_dst, v_ref[rows, :])

    prep(0, LEAD, qt0, vt0)

    def prep_block(i, _):
        prep(pl.multiple_of(LEAD + i * tb, LANES), tb, qt.at[i], vt.at[i])
        return 0

    lax.fori_loop(0, nblk, prep_block, 0)
    _attention_core(qt, qt0, ks, vt, vt0, s_scr, m_scr, g_ref[...], o_ref, nblk=nblk, tb=tb)


def fox_attention(proj, aug_q, aug_k, gain, *, batch, seq_total, tb):
    t = proj.shape[0]
    nblk = (seq_total - LEAD) // tb
    col = lambda c0: pl.BlockSpec((seq_total, LANES), lambda b, h: (b, c0 // LANES + h))
    return pl.pallas_call(
        functools.partial(_fox_attn_body, nblk=nblk, tb=tb),
        grid=(batch, FOX_HEADS),
        in_specs=[col(COL_FQ), col(COL_FK), col(COL_FV), col(0), col(0),
                  pl.BlockSpec((1, LANES), lambda b, h: (0, h))],
        out_specs=col(0),
        out_shape=jax.ShapeDtypeStruct((t, W_FOX), BF16),
        scratch_shapes=_attn_scratch(seq_total, nblk, tb),
        compiler_params=_params("parallel", "parallel"),
        name="fox_attention",
    )(proj, proj, proj, aug_q, aug_k, gain.reshape(1, W_FOX).astype(F32))


def _split3(x):
    hi = x.astype(BF16)
    r1 = x - hi.astype(F32)
    mid = r1.astype(BF16)
    lo = (r1 - mid.astype(F32)).astype(BF16)
    return hi.astype(F32), mid.astype(F32), lo.astype(F32)


def _fox_prep_body(z_ref, b_ref, aq_ref, ak_ref, carry_ref, *, tr):
    i = pl.program_id(1)

    @pl.when(i == 0)
    def _():
        carry_ref[...] = jnp.zeros_like(carry_ref)

    row = i * tr + lax.broadcasted_iota(jnp.int32, (tr, LANES), 0)
    valid = row >= N_PAD
    log_f = jnp.where(valid, _log_sigmoid(z_ref[...] + b_ref[...]), 0.0)
    r = lax.broadcasted_iota(jnp.int32, (tr, tr), 0)
    c = lax.broadcasted_iota(jnp.int32, (tr, tr), 1)
    tri = jnp.where(c <= r, 1.0, 0.0).astype(F32)
    csum = jnp.dot(tri, log_f, preferred_element_type=F32,
                   precision=lax.Precision.HIGHEST) + carry_ref[0:1, :]
    carry_ref[...] = jnp.broadcast_to(csum[tr - 1:tr, :], carry_ref.shape)

    lane = lax.broadcasted_iota(jnp.int32, (tr, LANES), 1)
    key_mask = jnp.where(valid, 0.0, NEG_INF)
    for h in range(FOX_HEADS):
        col = csum[:, GATE_FZ + h:GATE_FZ + h + 1] * LOG2E
        hi, mid, lo = _split3(col)
        aq = jnp.where(lane == 0, hi, jnp.where(lane == 1, mid, jnp.where(lane == 2, lo,
             jnp.where(lane < 7, 1.0, 0.0))))
        ak = jnp.where(lane < 3, 1.0, jnp.where(lane == 3, -hi, jnp.where(lane == 4, -mid,
             jnp.where(lane == 5, -lo, jnp.where(lane == 6, key_mask, 0.0)))))
        aq_ref[:, h * LANES:(h + 1) * LANES] = aq.astype(BF16)
        ak_ref[:, h * LANES:(h + 1) * LANES] = ak.astype(BF16)


def fox_prep(gates, fox_b, *, batch, seq_total):
    t = gates.shape[0]
    tr = _pick(seq_total, 384, LANES)
    nblk = seq_total // tr
    bias = jnp.zeros((1, LANES), F32).at[0, GATE_FZ:GATE_FZ + FOX_HEADS].set(fox_b.astype(F32))
    out = jax.ShapeDtypeStruct((t, W_FOX), BF16)
    return pl.pallas_call(
        functools.partial(_fox_prep_body, tr=tr),
        grid=(batch, nblk),
        in_specs=[pl.BlockSpec((tr, LANES), lambda b, i: (b * nblk + i, 0)),
                  pl.BlockSpec((1, LANES), lambda b, i: (0, 0))],
        out_specs=[pl.BlockSpec((tr, W_FOX), lambda b, i: (b * nblk + i, 0))] * 2,
        out_shape=[out, out],
        scratch_shapes=[pltpu.VMEM((SUBLANES, LANES), F32)],
        compiler_params=_params("parallel", "arbitrary"),
        name="fox_prep",
    )(gates, bias)


def _gla_body(q_ref, k_ref, v_ref, r_ref, z_ref, w2_ref, b2_ref, g_ref, o_ref, s_ref,
              *, seq_total):
    c = GLA_CHUNK
    nchunk = seq_total // c
    heads = s_ref.shape[0]
    nt = (((1,), (1,)), ((), ()))
    tn = (((0,), (0,)), ((), ()))
    r_i = lax.broadcasted_iota(jnp.int32, (c, c), 0)
    c_i = lax.broadcasted_iota(jnp.int32, (c, c), 1)
    lower = c_i <= r_i
    tri = jnp.where(lower, 1.0, 0.0).astype(F32)
    s_ref[...] = jnp.zeros_like(s_ref)

    def chunk(i, _):
        r0 = pl.multiple_of(i * c, c)
        rows = pl.ds(r0, c)
        valid = (r0 + lax.broadcasted_iota(jnp.int32, (c, GLA_DK), 0)) >= N_PAD
        z = z_ref[rows, :].astype(BF16)
        for hh in range(heads):
            kcols = pl.ds(hh * GLA_DK, GLA_DK)
            vcols = pl.ds(hh * GLA_DV, GLA_DV)
            q = q_ref[rows, kcols].astype(F32) * (GLA_DK ** -0.5)
            k = jnp.where(valid, k_ref[rows, kcols].astype(F32), 0.0)
            v = v_ref[rows, vcols]
            logit = jnp.dot(z, w2_ref[:, kcols], preferred_element_type=F32) + b2_ref[:, kcols]
            log_a = jnp.where(valid, _log_sigmoid(logit) / GLA_TAU, 0.0)
            bc = jnp.dot(tri, log_a, preferred_element_type=F32, precision=lax.Precision.HIGHEST)
            b_last = bc[c - 1:c, :]
            q_dec = (q * jnp.exp(bc)).astype(BF16)
            k_dec = (k * jnp.exp(-bc)).astype(BF16)
            a = lax.dot_general(q_dec, k_dec, nt, preferred_element_type=F32)
            a = jnp.where(lower, a, 0.0).astype(BF16)
            state = s_ref[hh]
            o = (jnp.dot(a, v, preferred_element_type=F32)
                 + jnp.dot(q_dec, state.astype(BF16), preferred_element_type=F32))
            k_state = (k * jnp.exp(b_last - bc)).astype(BF16)
            decay = jnp.exp(b_last)
            dec_t = jnp.transpose(jnp.broadcast_to(decay, (GLA_DK, GLA_DK)))
            dec_t = jnp.concatenate([dec_t, dec_t], axis=1)
            s_ref[hh] = state * dec_t + lax.dot_general(k_state, v, tn, preferred_element_type=F32)
            o = o * lax.rsqrt(jnp.mean(o * o, axis=-1, keepdims=True) + RMS_EPS) * g_ref[:, vcols]
            o_ref[rows, vcols] = (o * _silu(r_ref[rows, vcols].astype(F32))).astype(o_ref.dtype)
        return 0

    lax.fori_loop(0, nchunk, chunk, 0)


def gla(proj, gates, w2, b2, gain, *, batch, seq_total):
    t = proj.shape[0]
    hps = GLA_HEADS_PER_STEP
    assert GLA_HEADS % hps == 0
    wk, wv = hps * GLA_DK, hps * GLA_DV
    blk = lambda w, c0: pl.BlockSpec((seq_total, w), lambda b, h: (b, c0 // w + h))
    return pl.pallas_call(
        functools.partial(_gla_body, seq_total=seq_total),
        grid=(batch, GLA_HEADS // hps),
        in_specs=[blk(wk, COL_GQ), blk(wk, COL_GK), blk(wv, COL_GV), blk(wv, COL_GR),
                  pl.BlockSpec((seq_total, LANES), lambda b, h: (b, 0)),
                  pl.BlockSpec((LANES, wk), lambda b, h: (0, h)),
                  pl.BlockSpec((1, wk), lambda b, h: (0, h)),
                  pl.BlockSpec((1, wv), lambda b, h: (0, h))],
        out_specs=pl.BlockSpec((seq_total, wv), lambda b, h: (b, h)),
        out_shape=jax.ShapeDtypeStruct((t, W_GLA), BF16),
        scratch_shapes=[pltpu.VMEM((hps, GLA_DK, GLA_DV), F32)],
        compiler_params=_params("parallel", "parallel"),
        name="gla",
    )(proj, proj, proj, proj, gates, w2, b2, gain.reshape(1, W_GLA).astype(F32))


def _rope_tables(seq_total):
    pos = np.maximum(np.arange(seq_total) - N_PAD, 0).astype(np.float32)
    inv_freq = (1.0 / (ROPE_THETA ** (np.arange(0, MLA_ROPE, 2, dtype=np.float32) / MLA_ROPE))).astype(np.float32)
    ang = jnp.asarray(pos)[:, None] * jnp.asarray(inv_freq)[None, :]
    cos, sin = jnp.cos(ang), jnp.sin(ang)
    z = jnp.zeros_like(cos)
    cos_t = jnp.concatenate([cos, cos, z, z], axis=1)
    sin_a = jnp.concatenate([z, sin, z, z], axis=1)
    sin_b = jnp.concatenate([-sin, z, z, z], axis=1)
    return cos_t, sin_a, sin_b


def kernel(x, meta_tokens, attn_norm, w_in, mla_q_norm, mla_w_uq, mla_kv_norm, mla_w_ukv,
           gla_w_gate2, gla_b_gate, fox_b_f, out_norm_mla, out_norm_gla, out_norm_fox,
           w_out, ffn_norm, ffn_w_up, ffn_conv_w, ffn_conv_b, ffn_w_down, final_norm):
    batch, seq, d_model = x.shape
    depth = w_in.shape[0]
    d_ff = ffn_w_down.shape[1]
    seq_total = LEAD + seq
    cos_t, sin_a, sin_b = _rope_tables(seq_total)

    tm_big = _pick(seq_total, TM_BIG, LANES)
    tm_mid = _pick(seq_total, TM_MID, LANES)
    tb = _pick(seq, ATTN_BLOCK, LANES)
    tc = _pick(d_ff, 256, LANES)

    w_in_b = transpose_cast(jnp.transpose(w_in, (2, 0, 1)))

    h, hn = embed_norm(x, meta_tokens, attn_norm[0])
    for layer in range(depth):
        w_big, w_gates = relayout_cols(w_in_b, (W_IN_PLAN, W_GATE_PLAN), (N_PROJ, LANES), layer=layer)
        w_uq, = relayout_cols(mla_w_uq, (W_UQ_PLAN,), (MLA_HEADS * 2 * LANES,), layer=layer)
        if layer > 0:
            hn = rmsnorm(h, attn_norm[layer])
        proj = matmul(hn, w_big, tm=tm_big, tn=_pick(N_PROJ, 896, LANES))
        gates = matmul(hn, w_gates, tm=tm_big, tn=LANES, out_dtype=F32)

        cq = rmsnorm(proj, mla_q_norm[layer], col0=COL_CQ, width=MLA_Q_LORA)
        ckv = rmsnorm(proj, mla_kv_norm[layer], col0=COL_CKV, width=MLA_KV_LORA)
        q_raw = matmul(cq, w_uq, tm=tm_big, tn=1024)
        kv = matmul(ckv, mla_w_ukv, tm=tm_big, tn=1024, layer=layer)
        o_mla = mla_attention(q_raw, kv, proj, cos_t, sin_a, sin_b, out_norm_mla[layer],
                              batch=batch, seq_total=seq_total, tb=tb)

        w2 = jnp.zeros((LANES, W_GQK), F32).at[:GLA_GATE_RANK].set(gla_w_gate2[layer])
        o_gla = gla(proj, gates, w2.astype(BF16), gla_b_gate[layer].reshape(1, -1),
                    out_norm_gla[layer], batch=batch, seq_total=seq_total)

        aug_q, aug_k = fox_prep(gates, fox_b_f[layer], batch=batch, seq_total=seq_total)
        o_fox = fox_attention(proj, aug_q, aug_k, out_norm_fox[layer],
                              batch=batch, seq_total=seq_total, tb=tb)

        h = out_proj(o_mla, o_gla, o_fox, w_out, h, seq_total=seq_total, tm=tm_mid, tn=1024,
                     layer=layer)

        hn = rmsnorm(h, ffn_norm[layer])
        act = ffn_up(hn, ffn_w_up, ffn_conv_w, ffn_conv_b, seq_total=seq_total, tm=tm_big,
                     tc=tc, d_ff=d_ff, layer=layer)
        h = ffn_down(act, ffn_w_down, h, seq_total=seq_total, tm=tm_big, tn=1024, tk=DOWN_TK,
                     layer=layer)

    return final_rmsnorm(h, final_norm, batch, seq_total)
```

```python
import functools

import jax
import jax.numpy as jnp
import numpy as np
from jax import lax
from jax.experimental import pallas as pl
from jax.experimental.pallas import tpu as pltpu

F32 = jnp.float32
BF16 = jnp.bfloat16

N_META = 16
LEAD = 128
N_PAD = LEAD - N_META
RMS_EPS = 1e-6
NEG_INF = -1e30

MLA_V = 128
MLA_HEADS = 12
MLA_NOPE = 128
MLA_ROPE = 64
MLA_Q_LORA = 1536
MLA_KV_LORA = 512
ROPE_THETA = 10000.0

GLA_DV = 256
GLA_DK = 128
GLA_HEADS = 4
GLA_GATE_RANK = 16
GLA_TAU = 16.0
GLA_CHUNK = 64

FOX_DH = 128
FOX_HEADS = 12

W_MLA = MLA_HEADS * MLA_V
W_GLA = GLA_HEADS * GLA_DV
W_FOX = FOX_HEADS * FOX_DH
CONV_W = 3

LANES = 128
SUBLANES = 8
VMEM_LIMIT_BYTES = 56 * 1024 * 1024

TM_BIG = 1408
TM_MID = 704
ATTN_BLOCK = 512
DOWN_TK = 1024
GLA_HEADS_PER_STEP = 2
GLA_GROUP = 6
QUERY_PART = 256

HALF_ROPE = MLA_ROPE // 2
W_GQK = GLA_HEADS * GLA_DK
COL_CQ = 0
COL_CKV = COL_CQ + MLA_Q_LORA
COL_GV = COL_CKV + MLA_KV_LORA
COL_GR = COL_GV + W_GLA
COL_GQ = COL_GR + W_GLA
COL_GK = COL_GQ + W_GQK
COL_FQ = COL_GK + W_GQK
COL_FK = COL_FQ + W_FOX
COL_FV = COL_FK + W_FOX
COL_KR = COL_FV + W_FOX
N_PROJ = COL_KR + LANES
SRC_CQ = 0
SRC_CKV = SRC_CQ + MLA_Q_LORA
SRC_KR = SRC_CKV + MLA_KV_LORA
SRC_GQ = SRC_KR + MLA_ROPE
SRC_GK = SRC_GQ + W_GQK
SRC_GV = SRC_GK + W_GQK
SRC_GZ = SRC_GV + W_GLA
SRC_GR = SRC_GZ + GLA_GATE_RANK
SRC_FQ = SRC_GR + W_GLA
SRC_FK = SRC_FQ + W_FOX
SRC_FV = SRC_FK + W_FOX
SRC_FZ = SRC_FV + W_FOX
GATE_FZ = GLA_GATE_RANK
ONE_LANE = MLA_ROPE
LOG2E = 1.4426950408889634
V_ROWS = LANES + 16


def _pick(n, target, mult):
    best = None
    for d in range(mult, min(n, target) + 1, mult):
        if n % d == 0:
            best = d
    assert best is not None, (n, target, mult)
    return best


def _params(*sem):
    return pltpu.CompilerParams(dimension_semantics=sem, vmem_limit_bytes=VMEM_LIMIT_BYTES)


def _log_sigmoid(x):
    return jnp.minimum(x, 0.0) - jnp.log1p(jnp.exp(-jnp.abs(x)))


def _silu(x):
    return x / (1.0 + jnp.exp(-x))


def _wspec(w, layer, block, index, **kw):
    if layer is None:
        assert w.ndim == len(block)
        return pl.BlockSpec(block, index, **kw)
    assert w.ndim == len(block) + 1
    return pl.BlockSpec((None,) + tuple(block), lambda *g: (layer,) + tuple(index(*g)), **kw)


def _batch_row0(step, seq_total, tm):
    return (step % (seq_total // tm)) * tm


def _relayout_body(x_ref, *o_refs, plans):
    for o_ref, plan in zip(o_refs, plans):
        width = o_ref.shape[1]
        pos = 0
        for src, dst, w in plan:
            assert dst >= pos
            if dst > pos:
                o_ref[:, pos:dst] = jnp.zeros((o_ref.shape[0], dst - pos), o_ref.dtype)
            o_ref[:, dst:dst + w] = x_ref[:, src:src + w].astype(o_ref.dtype)
            pos = dst + w
        if pos < width:
            o_ref[:, pos:width] = jnp.zeros((o_ref.shape[0], width - pos), o_ref.dtype)


def relayout_cols(w, plans, widths, *, layer=None, rows=256):
    k, n = w.shape[-2:]
    tr = _pick(k, rows, 16)
    return pl.pallas_call(
        functools.partial(_relayout_body, plans=plans),
        grid=(k // tr,),
        in_specs=[_wspec(w, layer, (tr, n), lambda i: (i, 0))],
        out_specs=[pl.BlockSpec((tr, wd), lambda i: (i, 0)) for wd in widths],
        out_shape=[jax.ShapeDtypeStruct((k, wd), BF16) for wd in widths],
        compiler_params=_params("parallel"),
        name="relayout_cols",
    )(w)


def _transpose_cast_body(x_ref, o_ref):
    for layer in range(x_ref.shape[1]):
        o_ref[layer] = jnp.transpose(x_ref[:, layer, :]).astype(o_ref.dtype)


def transpose_cast(wt):
    n, depth, k = wt.shape
    nb = pl.cdiv(n, LANES)
    return pl.pallas_call(
        _transpose_cast_body,
        grid=(nb,),
        in_specs=[pl.BlockSpec((LANES, depth, k), lambda i: (i, 0, 0))],
        out_specs=pl.BlockSpec((depth, k, LANES), lambda i: (0, 0, i)),
        out_shape=jax.ShapeDtypeStruct((depth, k, nb * LANES), BF16),
        compiler_params=_params("parallel"),
        name="transpose_cast",
    )(wt)


W_IN_PLAN = sorted([
    (SRC_CQ, COL_CQ, MLA_Q_LORA), (SRC_CKV, COL_CKV, MLA_KV_LORA), (SRC_KR, COL_KR, MLA_ROPE),
    (SRC_GQ, COL_GQ, W_GQK), (SRC_GK, COL_GK, W_GQK), (SRC_GV, COL_GV, W_GLA),
    (SRC_GR, COL_GR, W_GLA), (SRC_FQ, COL_FQ, W_FOX), (SRC_FK, COL_FK, W_FOX),
    (SRC_FV, COL_FV, W_FOX)], key=lambda p: p[1])
W_GATE_PLAN = [(SRC_GZ, 0, GLA_GATE_RANK), (SRC_FZ, GATE_FZ, FOX_HEADS)]
W_UQ_PLAN = [(h * (MLA_NOPE + MLA_ROPE), h * 2 * LANES, MLA_NOPE + MLA_ROPE) for h in range(MLA_HEADS)]


def _rmsnorm_body(x_ref, g_ref, o_ref):
    x = x_ref[...].astype(F32)
    y = x * lax.rsqrt(jnp.mean(x * x, axis=-1, keepdims=True) + RMS_EPS)
    o_ref[...] = (y * g_ref[...]).astype(o_ref.dtype)


def rmsnorm(x, gain, *, col0=0, width=None, out_dtype=BF16, rows=256):
    m = x.shape[0]
    width = x.shape[1] if width is None else width
    assert col0 % width == 0
    tr = _pick(m, rows, 16)
    return pl.pallas_call(
        _rmsnorm_body,
        grid=(m // tr,),
        in_specs=[pl.BlockSpec((tr, width), lambda i: (i, col0 // width)),
                  pl.BlockSpec((1, width), lambda i: (0, 0))],
        out_specs=pl.BlockSpec((tr, width), lambda i: (i, 0)),
        out_shape=jax.ShapeDtypeStruct((m, width), out_dtype),
        compiler_params=_params("parallel"),
        name="rmsnorm",
    )(x, gain.reshape(1, width).astype(F32))


def _final_norm_body(x_ref, g_ref, o_ref):
    x = x_ref[...]
    y = x * lax.rsqrt(jnp.mean(x * x, axis=-1, keepdims=True) + RMS_EPS)
    o_ref[0] = y * g_ref[...]


def final_rmsnorm(h, gain, batch, seq_total):
    d = h.shape[1]
    tr = LEAD
    nblk = seq_total // tr
    return pl.pallas_call(
        _final_norm_body,
        grid=(batch, nblk - 1),
        in_specs=[pl.BlockSpec((tr, d), lambda b, i: (b * nblk + i + 1, 0)),
                  pl.BlockSpec((1, d), lambda b, i: (0, 0))],
        out_specs=pl.BlockSpec((1, tr, d), lambda b, i: (b, i, 0)),
        out_shape=jax.ShapeDtypeStruct((batch, seq_total - LEAD, d), F32),
        compiler_params=_params("parallel", "parallel"),
        name="final_norm",
    )(h, gain.reshape(1, d).astype(F32))


def _embed_norm_body(x_ref, meta_ref, g_ref, h_ref, hn_ref):
    lead = pl.program_id(1) == 0

    def emit(v):
        h_ref[...] = v
        hn = v * lax.rsqrt(jnp.mean(v * v, axis=-1, keepdims=True) + RMS_EPS) * g_ref[...]
        hn_ref[...] = hn.astype(hn_ref.dtype)

    @pl.when(lead)
    def _():
        d = h_ref.shape[1]
        emit(jnp.concatenate([jnp.zeros((N_PAD, d), F32), meta_ref[...]], axis=0))

    @pl.when(jnp.logical_not(lead))
    def _():
        emit(x_ref[0])


def embed_norm(x, meta_tokens, gain):
    batch, seq, d = x.shape
    assert seq % LEAD == 0
    nblk = seq // LEAD + 1
    t = batch * nblk * LEAD
    row = lambda b, i: (b * nblk + i, 0)
    return pl.pallas_call(
        _embed_norm_body,
        grid=(batch, nblk),
        in_specs=[pl.BlockSpec((1, LEAD, d), lambda b, i: (b, jnp.maximum(i - 1, 0), 0)),
                  pl.BlockSpec((N_META, d), lambda b, i: (0, 0)),
                  pl.BlockSpec((1, d), lambda b, i: (0, 0))],
        out_specs=[pl.BlockSpec((LEAD, d), row), pl.BlockSpec((LEAD, d), row)],
        out_shape=[jax.ShapeDtypeStruct((t, d), F32), jax.ShapeDtypeStruct((t, d), BF16)],
        compiler_params=_params("parallel", "arbitrary"),
        name="embed_norm",
    )(x.astype(F32), meta_tokens.astype(F32), gain.reshape(1, d).astype(F32))


def _mm_body(a_ref, w_ref, o_ref):
    o_ref[...] = jnp.dot(a_ref[...], w_ref[...].astype(BF16),
                         preferred_element_type=F32).astype(o_ref.dtype)


def matmul(a, w, *, tm, tn, out_dtype=BF16, layer=None):
    m, kdim = a.shape
    n = w.shape[-1]
    assert m % tm == 0 and n % tn == 0 and w.shape[-2] == kdim
    return pl.pallas_call(
        _mm_body,
        grid=(m // tm, n // tn),
        in_specs=[pl.BlockSpec((tm, kdim), lambda i, j: (i, 0)),
                  _wspec(w, layer, (kdim, tn), lambda i, j: (0, j))],
        out_specs=pl.BlockSpec((tm, tn), lambda i, j: (i, j)),
        out_shape=jax.ShapeDtypeStruct((m, n), out_dtype),
        compiler_params=_params("parallel", "parallel"),
        name="matmul",
    )(a, w)


def _out_proj_body(a1_ref, a2_ref, a3_ref, w_ref, r_ref, o_ref, wb_ref, *, seq_total, tm):
    i = pl.program_id(1)

    @pl.when(i == 0)
    def _():
        wb_ref[...] = w_ref[...].astype(BF16)

    k1 = a1_ref.shape[1]
    k2 = k1 + a2_ref.shape[1]
    acc = jnp.dot(a1_ref[...], wb_ref[0:k1, :], preferred_element_type=F32)
    acc += jnp.dot(a2_ref[...], wb_ref[k1:k2, :], preferred_element_type=F32)
    acc += jnp.dot(a3_ref[...], wb_ref[k2:, :], preferred_element_type=F32)
    pos = _batch_row0(i, seq_total, tm) + lax.broadcasted_iota(jnp.int32, acc.shape, 0)
    o_ref[...] = r_ref[...] + jnp.where(pos >= N_PAD, acc, 0.0)


def out_proj(a1, a2, a3, w, res, *, seq_total, tm, tn, layer=None):
    m = a1.shape[0]
    kdim, n = w.shape[-2:]
    assert a1.shape[1] + a2.shape[1] + a3.shape[1] == kdim
    assert m % tm == 0 and n % tn == 0 and seq_total % tm == 0
    a_spec = lambda a: pl.BlockSpec((tm, a.shape[1]), lambda j, i: (i, 0))
    return pl.pallas_call(
        functools.partial(_out_proj_body, seq_total=seq_total, tm=tm),
        grid=(n // tn, m // tm),
        in_specs=[a_spec(a1), a_spec(a2), a_spec(a3),
                  _wspec(w, layer, (kdim, tn), lambda j, i: (0, j), pipeline_mode=pl.Buffered(1)),
                  pl.BlockSpec((tm, tn), lambda j, i: (i, j))],
        out_specs=pl.BlockSpec((tm, tn), lambda j, i: (i, j)),
        out_shape=jax.ShapeDtypeStruct((m, n), F32),
        scratch_shapes=[pltpu.VMEM((kdim, tn), BF16)],
        compiler_params=_params("parallel", "arbitrary"),
        name="out_proj",
    )(a1, a2, a3, w, res)


CONV_CHUNK = 32


def _ffn_up_body(a_ref, wg_ref, wv_ref, cwg_ref, cwv_ref, cbg_ref, cbv_ref, o_ref,
                 wcat_ref, u_ref, halo_ref, *, seq_total, tm, tc):
    i = pl.program_id(0)
    j = pl.program_id(1)
    wcat_ref[:, 0:tc] = wg_ref[...].astype(BF16)
    wcat_ref[:, tc:2 * tc] = wv_ref[...].astype(BF16)
    row0 = _batch_row0(i, seq_total, tm)
    u = jnp.dot(a_ref[...], wcat_ref[...], preferred_element_type=F32)
    pos = row0 + lax.broadcasted_iota(jnp.int32, u.shape, 0)
    u_ref[SUBLANES:SUBLANES + tm, :] = jnp.where(pos >= N_PAD, u, 0.0)

    @pl.when(row0 == 0)
    def _():
        u_ref[0:SUBLANES, :] = jnp.zeros((SUBLANES, 2 * tc), F32)

    @pl.when(row0 > 0)
    def _():
        u_ref[0:SUBLANES, :] = halo_ref[j]

    halo_ref[j] = u_ref[tm:tm + SUBLANES, :]
    cw = jnp.concatenate([cwg_ref[...], cwv_ref[...]], axis=1)
    cb = jnp.concatenate([cbg_ref[...], cbv_ref[...]], axis=1)
    for r0 in range(0, tm, CONV_CHUNK):
        cv = cb
        for tap in range(CONV_W):
            lo = SUBLANES + r0 - (CONV_W - 1 - tap)
            cv = cv + cw[tap:tap + 1, :] * u_ref[lo:lo + CONV_CHUNK, :]
        o_ref[r0:r0 + CONV_CHUNK, :] = (_silu(cv[:, 0:tc]) * cv[:, tc:2 * tc]).astype(o_ref.dtype)


def ffn_up(hn, w_up, conv_w, conv_b, *, seq_total, tm, tc, d_ff, layer=None):
    m, kdim = hn.shape
    assert m % tm == 0 and seq_total % tm == 0 and tm % CONV_CHUNK == 0 and d_ff % tc == 0
    nj = d_ff // tc
    wspec = lambda off: _wspec(w_up, layer, (kdim, tc), lambda i, j: (0, j + off))
    cspec = lambda c, r, off: _wspec(c, layer, (r, tc), lambda i, j: (0, j + off))
    conv_b = conv_b.reshape(conv_b.shape[:-1] + (1, conv_b.shape[-1]))
    return pl.pallas_call(
        functools.partial(_ffn_up_body, seq_total=seq_total, tm=tm, tc=tc),
        grid=(m // tm, nj),
        in_specs=[pl.BlockSpec((tm, kdim), lambda i, j: (i, 0)),
                  wspec(0), wspec(nj), cspec(conv_w, CONV_W, 0), cspec(conv_w, CONV_W, nj),
                  cspec(conv_b, 1, 0), cspec(conv_b, 1, nj)],
        out_specs=pl.BlockSpec((tm, tc), lambda i, j: (i, j)),
        out_shape=jax.ShapeDtypeStruct((m, d_ff), BF16),
        scratch_shapes=[pltpu.VMEM((kdim, 2 * tc), BF16),
                        pltpu.VMEM((tm + SUBLANES, 2 * tc), F32),
                        pltpu.VMEM((nj, SUBLANES, 2 * tc), F32)],
        compiler_params=_params("arbitrary", "arbitrary"),
        name="ffn_up",
    )(hn, w_up, w_up, conv_w, conv_w, conv_b, conv_b)


def _ffn_down_body(a_ref, w_ref, r_ref, o_ref, *, seq_total, tm, tk, d_ff, nk):
    k = pl.program_id(2)

    def part(a, w):
        return jnp.dot(a, w.astype(BF16), preferred_element_type=F32)

    @pl.when(k == 0)
    def _():
        o_ref[...] = part(a_ref[...], w_ref[...])

    @pl.when(jnp.logical_and(k > 0, k < nk - 1))
    def _():
        o_ref[...] += part(a_ref[...], w_ref[...])

    @pl.when(k == nk - 1)
    def _():
        acol = k * tk + lax.broadcasted_iota(jnp.int32, a_ref.shape, 1)
        a = jnp.where(acol < d_ff, a_ref[...], jnp.zeros(a_ref.shape, a_ref.dtype))
        wrow = k * tk + lax.broadcasted_iota(jnp.int32, w_ref.shape, 0)
        w = jnp.where(wrow < d_ff, w_ref[...], 0.0)
        pos = _batch_row0(pl.program_id(0), seq_total, tm) + lax.broadcasted_iota(
            jnp.int32, o_ref.shape, 0)
        o_ref[...] = r_ref[...] + jnp.where(pos >= N_PAD, o_ref[...] + part(a, w), 0.0)


def ffn_down(act, w, res, *, seq_total, tm, tn, tk, layer=None):
    m, d_ff = act.shape
    n = w.shape[-1]
    assert m % tm == 0 and n % tn == 0 and seq_total % tm == 0 and w.shape[-2] == d_ff
    nk = pl.cdiv(d_ff, tk)
    assert nk >= 2
    return pl.pallas_call(
        functools.partial(_ffn_down_body, seq_total=seq_total, tm=tm, tk=tk, d_ff=d_ff, nk=nk),
        grid=(m // tm, n // tn, nk),
        in_specs=[pl.BlockSpec((tm, tk), lambda i, j, k: (i, k)),
                  _wspec(w, layer, (tk, tn), lambda i, j, k: (k, j)),
                  pl.BlockSpec((tm, tn), lambda i, j, k: (i, j))],
        out_specs=pl.BlockSpec((tm, tn), lambda i, j, k: (i, j)),
        out_shape=jax.ShapeDtypeStruct((m, n), F32),
        compiler_params=_params("parallel", "parallel", "arbitrary"),
        name="ffn_down",
    )(act, w, res)


def _scores(k_rows, q_parts):
    out = []
    for q in q_parts:
        s = jnp.dot(k_rows, q, preferred_element_type=F32)
        out.append((s, jnp.max(s, axis=0, keepdims=True)))
    return tuple(out)


def _causal(scored, col0s):
    out = []
    for (s, _), c0 in zip(scored, col0s):
        key = lax.broadcasted_iota(jnp.int32, s.shape, 0)
        qry = c0 + lax.broadcasted_iota(jnp.int32, s.shape, 1)
        s = jnp.where(key <= qry, s, NEG_INF)
        out.append((s, jnp.max(s, axis=0, keepdims=True)))
    return tuple(out)


def _softmax_pv(carry, scored, v_ext):
    out = []
    for (m, acc), (s, s_max) in zip(carry, scored):
        m_new = jnp.maximum(m, s_max)
        alpha = jnp.exp2(m - m_new)
        p = jnp.exp2(s - m_new).astype(BF16)
        out.append((m_new, alpha * acc + jnp.dot(v_ext, p, preferred_element_type=F32)))
    return tuple(out)


def _attn_init(widths):
    return tuple((jnp.full((1, w), NEG_INF, F32), jnp.zeros((V_ROWS, w), F32)) for w in widths)


def _attn_finish(carry, gain, o_ref, row0, col0s):
    for (_, acc), c0 in zip(carry, col0s):
        o = jnp.transpose(acc[0:LANES, :] / acc[LANES:LANES + 1, :])
        o = o * lax.rsqrt(jnp.mean(o * o, axis=-1, keepdims=True) + RMS_EPS) * gain
        o_ref[pl.ds(row0 + c0, o.shape[0]), :] = o.astype(o_ref.dtype)


def _query_parts(tb):
    part = QUERY_PART if tb % QUERY_PART == 0 else tb
    return part, tuple(range(0, tb, part))


def _attention_core(qt, qt0, ks, vt, vt0, s_scr, m_scr, gain, o_ref, *, nblk, tb):
    lead = _causal(_scores(ks[0:LEAD, :], (qt0[...],)), (0,))
    _attn_finish(_softmax_pv(_attn_init((LEAD,)), lead, vt0[...]), gain, o_ref, 0, (0,))

    part, col0s = _query_parts(tb)
    widths = (part,) * len(col0s)

    def q_block(qi, _):
        q_parts = tuple(qt[qi, :, c0:c0 + part] for c0 in col0s)

        def score(slot, kj):
            k_rows = ks[pl.ds(pl.multiple_of(LEAD + kj * tb, LANES), tb), :]
            for p, (s, s_max) in enumerate(_scores(k_rows, q_parts)):
                s_scr[slot, p] = s
                m_scr[slot, p] = s_max

        def scored(slot):
            return tuple((s_scr[slot, p], m_scr[slot, p]) for p in range(len(col0s)))

        carry = _softmax_pv(_attn_init(widths), _scores(ks[0:LEAD, :], q_parts), vt0[...])
        score(0, 0)

        def pair(t, carry):
            kj = 2 * t
            score(1, kj + 1)
            carry = _softmax_pv(carry, scored(0), vt[kj])
            score(0, kj + 2)
            return _softmax_pv(carry, scored(1), vt[kj + 1])

        carry = lax.fori_loop(0, qi // 2, pair, carry)

        def odd_tail(carry):
            score(1, qi)
            carry = _softmax_pv(carry, scored(0), vt[qi - 1])
            return _softmax_pv(carry, _causal(scored(1), col0s), vt[qi])

        def even_tail(carry):
            return _softmax_pv(carry, _causal(scored(0), col0s), vt[qi])

        carry = lax.cond(qi % 2 == 1, odd_tail, even_tail, carry)
        _attn_finish(carry, gain, o_ref, pl.multiple_of(LEAD + qi * tb, LANES), col0s)
        return 0

    lax.fori_loop(0, nblk, q_block, 0)


def _store_v_ext(vt_dst, v_rows):
    n = v_rows.shape[0]
    vt_dst[0:LANES, :] = _to_t(v_rows)
    r = lax.broadcasted_iota(jnp.int32, (V_ROWS - LANES, n), 0)
    vt_dst[LANES:V_ROWS, :] = jnp.where(r == 0, 1.0, 0.0).astype(BF16)


def _to_t(x):
    return jnp.transpose(x.astype(F32)).astype(BF16)


def _mla_attn_body(q_ref, kv_ref, kr_ref, cos_ref, sa_ref, sb_ref, g_ref, o_ref,
                   qt, qt0, ks, vt, vt0, s_scr, m_scr, *, nblk, tb):
    scale = (MLA_NOPE + MLA_ROPE) ** -0.5 * LOG2E

    def rope(x, rows):
        return (x * cos_ref[rows, :] + pltpu.roll(x, HALF_ROPE, 1) * sa_ref[rows, :]
                + pltpu.roll(x, LANES - HALF_ROPE, 1) * sb_ref[rows, :])

    def prep(r0, n, qt_dst, vt_dst):
        rows = pl.ds(r0, n)
        one = lax.broadcasted_iota(jnp.int32, (n, LANES), 1) == ONE_LANE
        qn = q_ref[rows, pl.ds(0, LANES)].astype(F32) * scale
        qp = rope(q_ref[rows, pl.ds(LANES, LANES)].astype(F32), rows) * scale
        qp = jnp.where(one, 1.0, qp)
        qt_dst[0:LANES, :] = jnp.transpose(qn).astype(BF16)
        qt_dst[LANES:2 * LANES, :] = jnp.transpose(qp).astype(BF16)
        pos = r0 + lax.broadcasted_iota(jnp.int32, (n, LANES), 0)
        kp = rope(kr_ref[rows, :].astype(F32), rows)
        kp = jnp.where(one, jnp.where(pos < N_PAD, NEG_INF, 0.0), kp)
        ks[rows, pl.ds(0, LANES)] = kv_ref[rows, pl.ds(0, LANES)]
        ks[rows, pl.ds(LANES, LANES)] = kp.astype(BF16)
        _store_v_ext(vt_dst, kv_ref[rows, pl.ds(LANES, LANES)])

    prep(0, LEAD, qt0, vt0)

    def prep_block(i, _):
        prep(pl.multiple_of(LEAD + i * tb, LANES), tb, qt.at[i], vt.at[i])
        return 0

    lax.fori_loop(0, nblk, prep_block, 0)
    _attention_core(qt, qt0, ks, vt, vt0, s_scr, m_scr, g_ref[...], o_ref, nblk=nblk, tb=tb)


def _attn_scratch(seq_total, nblk, tb):
    part, col0s = _query_parts(tb)
    return [pltpu.VMEM((nblk, 2 * LANES, tb), BF16), pltpu.VMEM((2 * LANES, LEAD), BF16),
            pltpu.VMEM((seq_total, 2 * LANES), BF16),
            pltpu.VMEM((nblk, V_ROWS, tb), BF16), pltpu.VMEM((V_ROWS, LEAD), BF16),
            pltpu.VMEM((2, len(col0s), tb, part), F32), pltpu.VMEM((2, len(col0s), 1, part), F32)]


def mla_attention(q_raw, kv, proj, cos_t, sin_a, sin_b, gain, *, batch, seq_total, tb):
    t = q_raw.shape[0]
    nblk = (seq_total - LEAD) // tb
    big = lambda w: pl.BlockSpec((seq_total, w), lambda b, h: (b, h))
    table = pl.BlockSpec((seq_total, LANES), lambda b, h: (0, 0))
    return pl.pallas_call(
        functools.partial(_mla_attn_body, nblk=nblk, tb=tb),
        grid=(batch, MLA_HEADS),
        in_specs=[big(2 * LANES), big(2 * LANES),
                  pl.BlockSpec((seq_total, LANES), lambda b, h: (b, COL_KR // LANES)),
                  table, table, table,
                  pl.BlockSpec((1, LANES), lambda b, h: (0, h))],
        out_specs=big(LANES),
        out_shape=jax.ShapeDtypeStruct((t, W_MLA), BF16),
        scratch_shapes=_attn_scratch(seq_total, nblk, tb),
        compiler_params=_params("parallel", "parallel"),
        name="mla_attention",
    )(q_raw, kv, proj, cos_t, sin_a, sin_b, gain.reshape(1, W_MLA).astype(F32))


def _fox_attn_body(q_ref, k_ref, v_ref, aq_ref, ak_ref, g_ref, o_ref,
                   qt, qt0, ks, vt, vt0, s_scr, m_scr, *, nblk, tb):
    scale = FOX_DH ** -0.5 * LOG2E

    def prep(r0, n, qt_dst, vt_dst):
        rows = pl.ds(r0, n)
        qt_dst[0:LANES, :] = jnp.transpose(q_ref[rows, :].astype(F32) * scale).astype(BF16)
        qt_dst[LANES:2 * LANES, :] = _to_t(aq_ref[rows, :])
        ks[rows, pl.ds(0, LANES)] = k_ref[rows, :]
        ks[rows, pl.ds(LANES, LANES)] = ak_ref[rows, :]
        _store_v_ext(vt_dst, v_ref[rows, :])

    prep(0, LEAD, qt0, vt0)

    def prep_block(i, _):
        prep(pl.multiple_of(LEAD + i * tb, LANES), tb, qt.at[i], vt.at[i])
        return 0

    lax.fori_loop(0, nblk, prep_block, 0)
    _attention_core(qt, qt0, ks, vt, vt0, s_scr, m_scr, g_ref[...], o_ref, nblk=nblk, tb=tb)


def fox_attention(proj, aug_q, aug_k, gain, *, batch, seq_total, tb):
    t = proj.shape[0]
    nblk = (seq_total - LEAD) // tb
    col = lambda c0: pl.BlockSpec((seq_total, LANES), lambda b, h: (b, c0 // LANES + h))
    return pl.pallas_call(
        functools.partial(_fox_attn_body, nblk=nblk, tb=tb),
        grid=(batch, FOX_HEADS),
        in_specs=[col(COL_FQ), col(COL_FK), col(COL_FV), col(0), col(0),
                  pl.BlockSpec((1, LANES), lambda b, h: (0, h))],
        out_specs=col(0),
        out_shape=jax.ShapeDtypeStruct((t, W_FOX), BF16),
        scratch_shapes=_attn_scratch(seq_total, nblk, tb),
        compiler_params=_params("parallel", "parallel"),
        name="fox_attention",
    )(proj, proj, proj, aug_q, aug_k, gain.reshape(1, W_FOX).astype(F32))


def _split3(x):
    hi = x.astype(BF16)
    r1 = x - hi.astype(F32)
    mid = r1.astype(BF16)
    lo = (r1 - mid.astype(F32)).astype(BF16)
    return hi.astype(F32), mid.astype(F32), lo.astype(F32)


def _fox_prep_body(z_ref, b_ref, aq_ref, ak_ref, carry_ref, *, tr):
    i = pl.program_id(1)

    @pl.when(i == 0)
    def _():
        carry_ref[...] = jnp.zeros_like(carry_ref)

    row = i * tr + lax.broadcasted_iota(jnp.int32, (tr, LANES), 0)
    valid = row >= N_PAD
    log_f = jnp.where(valid, _log_sigmoid(z_ref[...] + b_ref[...]), 0.0)
    r = lax.broadcasted_iota(jnp.int32, (tr, tr), 0)
    c = lax.broadcasted_iota(jnp.int32, (tr, tr), 1)
    tri = jnp.where(c <= r, 1.0, 0.0).astype(F32)
    csum = jnp.dot(tri, log_f, preferred_element_type=F32,
                   precision=lax.Precision.HIGHEST) + carry_ref[0:1, :]
    carry_ref[...] = jnp.broadcast_to(csum[tr - 1:tr, :], carry_ref.shape)

    lane = lax.broadcasted_iota(jnp.int32, (tr, LANES), 1)
    key_mask = jnp.where(valid, 0.0, NEG_INF)
    for h in range(FOX_HEADS):
        col = csum[:, GATE_FZ + h:GATE_FZ + h + 1] * LOG2E
        hi, mid, lo = _split3(col)
        aq = jnp.where(lane == 0, hi, jnp.where(lane == 1, mid, jnp.where(lane == 2, lo,
             jnp.where(lane < 7, 1.0, 0.0))))
        ak = jnp.where(lane < 3, 1.0, jnp.where(lane == 3, -hi, jnp.where(lane == 4, -mid,
             jnp.where(lane == 5, -lo, jnp.where(lane == 6, key_mask, 0.0)))))
        aq_ref[:, h * LANES:(h + 1) * LANES] = aq.astype(BF16)
        ak_ref[:, h * LANES:(h + 1) * LANES] = ak.astype(BF16)


def fox_prep(gates, fox_b, *, batch, seq_total):
    t = gates.shape[0]
    tr = _pick(seq_total, 384, LANES)
    nblk = seq_total // tr
    bias = jnp.zeros((1, LANES), F32).at[0, GATE_FZ:GATE_FZ + FOX_HEADS].set(fox_b.astype(F32))
    out = jax.ShapeDtypeStruct((t, W_FOX), BF16)
    return pl.pallas_call(
        functools.partial(_fox_prep_body, tr=tr),
        grid=(batch, nblk),
        in_specs=[pl.BlockSpec((tr, LANES), lambda b, i: (b * nblk + i, 0)),
                  pl.BlockSpec((1, LANES), lambda b, i: (0, 0))],
        out_specs=[pl.BlockSpec((tr, W_FOX), lambda b, i: (b * nblk + i, 0))] * 2,
        out_shape=[out, out],
        scratch_shapes=[pltpu.VMEM((SUBLANES, LANES), F32)],
        compiler_params=_params("parallel", "arbitrary"),
        name="fox_prep",
    )(gates, bias)


def _gla_body(q_ref, k_ref, v_ref, r_ref, z_ref, w2_ref, b2_ref, g_ref, o_ref, s_ref,
              *, seq_total):
    c = GLA_CHUNK
    nchunk = seq_total // c
    gsz = max(g for g in range(1, GLA_GROUP + 1) if nchunk % g == 0)
    rg = gsz * c
    heads = s_ref.shape[0]
    nt = (((1,), (1,)), ((), ()))
    tn = (((0,), (0,)), ((), ()))
    r_i = lax.broadcasted_iota(jnp.int32, (rg, rg), 0)
    c_i = lax.broadcasted_iota(jnp.int32, (rg, rg), 1)
    assert c & (c - 1) == 0
    lower = jnp.logical_and(c_i <= r_i, c_i >= jnp.bitwise_and(r_i, -c))
    tri = jnp.where(lower, 1.0, 0.0).astype(F32)
    s_ref[...] = jnp.zeros_like(s_ref)

    def group(i, _):
        r0 = pl.multiple_of(i * rg, rg)
        rows = pl.ds(r0, rg)
        valid = (r0 + lax.broadcasted_iota(jnp.int32, (rg, GLA_DK), 0)) >= N_PAD
        z = z_ref[rows, :].astype(BF16)
        for hh in range(heads):
            kcols = pl.ds(hh * GLA_DK, GLA_DK)
            vcols = pl.ds(hh * GLA_DV, GLA_DV)
            q = q_ref[rows, kcols].astype(F32) * (GLA_DK ** -0.5)
            k = jnp.where(valid, k_ref[rows, kcols].astype(F32), 0.0)
            v = v_ref[rows, vcols]
            logit = jnp.dot(z, w2_ref[:, kcols], preferred_element_type=F32) + b2_ref[:, kcols]
            log_a = jnp.where(valid, _log_sigmoid(logit) / GLA_TAU, 0.0)
            bc = jnp.dot(tri, log_a, preferred_element_type=F32, precision=lax.Precision.HIGHEST)
            b_last = jnp.concatenate(
                [jnp.broadcast_to(bc[(n + 1) * c - 1:(n + 1) * c, :], (c, GLA_DK)) for n in range(gsz)], axis=0)
            q_dec = (q * jnp.exp(bc)).astype(BF16)
            k_dec = (k * jnp.exp(-bc)).astype(BF16)
            k_state = (k * jnp.exp(b_last - bc)).astype(BF16)
            a = lax.dot_general(q_dec, k_dec, nt, preferred_element_type=F32)
            a = jnp.where(lower, a, 0.0).astype(BF16)
            o_intra = jnp.dot(a, v, preferred_element_type=F32)
            state = s_ref[hh]
            outs = []
            for n in range(gsz):
                sl = slice(n * c, (n + 1) * c)
                outs.append(o_intra[sl] + jnp.dot(q_dec[sl], state.astype(BF16),
                                                   preferred_element_type=F32))
                decay = jnp.exp(b_last[n * c:n * c + 1, :])
                dec_t = jnp.transpose(jnp.broadcast_to(decay, (GLA_DK, GLA_DK)))
                dec_t = jnp.concatenate([dec_t, dec_t], axis=1)
                state = state * dec_t + lax.dot_general(k_state[sl], v[sl], tn,
                                                        preferred_element_type=F32)
            s_ref[hh] = state
            o = jnp.concatenate(outs, axis=0)
            o = o * lax.rsqrt(jnp.mean(o * o, axis=-1, keepdims=True) + RMS_EPS) * g_ref[:, vcols]
            o_ref[rows, vcols] = (o * _silu(r_ref[rows, vcols].astype(F32))).astype(o_ref.dtype)
        return 0

    lax.fori_loop(0, nchunk // gsz, group, 0)


def gla(proj, gates, w2, b2, gain, *, batch, seq_total):
    t = proj.shape[0]
    hps = GLA_HEADS_PER_STEP
    assert GLA_HEADS % hps == 0
    wk, wv = hps * GLA_DK, hps * GLA_DV
    blk = lambda w, c0: pl.BlockSpec((seq_total, w), lambda b, h: (b, c0 // w + h))
    return pl.pallas_call(
        functools.partial(_gla_body, seq_total=seq_total),
        grid=(batch, GLA_HEADS // hps),
        in_specs=[blk(wk, COL_GQ), blk(wk, COL_GK), blk(wv, COL_GV), blk(wv, COL_GR),
                  pl.BlockSpec((seq_total, LANES), lambda b, h: (b, 0)),
                  pl.BlockSpec((LANES, wk), lambda b, h: (0, h)),
                  pl.BlockSpec((1, wk), lambda b, h: (0, h)),
                  pl.BlockSpec((1, wv), lambda b, h: (0, h))],
        out_specs=pl.BlockSpec((seq_total, wv), lambda b, h: (b, h)),
        out_shape=jax.ShapeDtypeStruct((t, W_GLA), BF16),
        scratch_shapes=[pltpu.VMEM((hps, GLA_DK, GLA_DV), F32)],
        compiler_params=_params("parallel", "parallel"),
        name="gla",
    )(proj, proj, proj, proj, gates, w2, b2, gain.reshape(1, W_GLA).astype(F32))


def _rope_tables(seq_total):
    pos = np.maximum(np.arange(seq_total) - N_PAD, 0).astype(np.float32)
    inv_freq = (1.0 / (ROPE_THETA ** (np.arange(0, MLA_ROPE, 2, dtype=np.float32) / MLA_ROPE))).astype(np.float32)
    ang = jnp.asarray(pos)[:, None] * jnp.asarray(inv_freq)[None, :]
    cos, sin = jnp.cos(ang), jnp.sin(ang)
    z = jnp.zeros_like(cos)
    cos_t = jnp.concatenate([cos, cos, z, z], axis=1)
    sin_a = jnp.concatenate([z, sin, z, z], axis=1)
    sin_b = jnp.concatenate([-sin, z, z, z], axis=1)
    return cos_t, sin_a, sin_b


def kernel(x, meta_tokens, attn_norm, w_in, mla_q_norm, mla_w_uq, mla_kv_norm, mla_w_ukv,
           gla_w_gate2, gla_b_gate, fox_b_f, out_norm_mla, out_norm_gla, out_norm_fox,
           w_out, ffn_norm, ffn_w_up, ffn_conv_w, ffn_conv_b, ffn_w_down, final_norm):
    batch, seq, d_model = x.shape
    depth = w_in.shape[0]
    d_ff = ffn_w_down.shape[1]
    seq_total = LEAD + seq
    cos_t, sin_a, sin_b = _rope_tables(seq_total)

    tm_big = _pick(seq_total, TM_BIG, LANES)
    tm_mid = _pick(seq_total, TM_MID, LANES)
    tb = _pick(seq, ATTN_BLOCK, LANES)
    tc = _pick(d_ff, 256, LANES)

    w_in_b = transpose_cast(jnp.transpose(w_in, (2, 0, 1)))

    h, hn = embed_norm(x, meta_tokens, attn_norm[0])
    for layer in range(depth):
        w_big, w_gates = relayout_cols(w_in_b, (W_IN_PLAN, W_GATE_PLAN), (N_PROJ, LANES), layer=layer)
        w_uq, = relayout_cols(mla_w_uq, (W_UQ_PLAN,), (MLA_HEADS * 2 * LANES,), layer=layer)
        if layer > 0:
            hn = rmsnorm(h, attn_norm[layer])
        proj = matmul(hn, w_big, tm=tm_big, tn=_pick(N_PROJ, 896, LANES))
        gates = matmul(hn, w_gates, tm=tm_big, tn=LANES, out_dtype=F32)

        cq = rmsnorm(proj, mla_q_norm[layer], col0=COL_CQ, width=MLA_Q_LORA)
        ckv = rmsnorm(proj, mla_kv_norm[layer], col0=COL_CKV, width=MLA_KV_LORA)
        q_raw = matmul(cq, w_uq, tm=tm_big, tn=1024)
        kv = matmul(ckv, mla_w_ukv, tm=tm_big, tn=1024, layer=layer)
        o_mla = mla_attention(q_raw, kv, proj, cos_t, sin_a, sin_b, out_norm_mla[layer],
                              batch=batch, seq_total=seq_total, tb=tb)

        w2 = jnp.zeros((LANES, W_GQK), F32).at[:GLA_GATE_RANK].set(gla_w_gate2[layer])
        o_gla = gla(proj, gates, w2.astype(BF16), gla_b_gate[layer].reshape(1, -1),
                    out_norm_gla[layer], batch=batch, seq_total=seq_total)

        aug_q, aug_k = fox_prep(gates, fox_b_f[layer], batch=batch, seq_total=seq_total)
        o_fox = fox_attention(proj, aug_q, aug_k, out_norm_fox[layer],
                              batch=batch, seq_total=seq_total, tb=tb)

        h = out_proj(o_mla, o_gla, o_fox, w_out, h, seq_total=seq_total, tm=tm_mid, tn=1024,
                     layer=layer)

        hn = rmsnorm(h, ffn_norm[layer])
        act = ffn_up(hn, ffn_w_up, ffn_conv_w, ffn_conv_b, seq_total=seq_total, tm=tm_big,
                     tc=tc, d_ff=d_ff, layer=layer)
        h = ffn_down(act, ffn_w_down, h, seq_total=seq_total, tm=tm_big, tn=1024, tk=DOWN_TK,
                     layer=layer)

    return final_rmsnorm(h, final_norm, batch, seq_total)
```

```python
import functools

import jax
import jax.numpy as jnp
import numpy as np
from jax import lax
from jax.experimental import pallas as pl
from jax.experimental.pallas import tpu as pltpu

F32 = jnp.float32
BF16 = jnp.bfloat16

N_META = 16
LEAD = 128
N_PAD = LEAD - N_META
RMS_EPS = 1e-6
NEG_INF = -1e30

MLA_V = 128
MLA_HEADS = 12
MLA_NOPE = 128
MLA_ROPE = 64
MLA_Q_LORA = 1536
MLA_KV_LORA = 512
ROPE_THETA = 10000.0

GLA_DV = 256
GLA_DK = 128
GLA_HEADS = 4
GLA_GATE_RANK = 16
GLA_TAU = 16.0
GLA_CHUNK = 64

FOX_DH = 128
FOX_HEADS = 12

W_MLA = MLA_HEADS * MLA_V
W_GLA = GLA_HEADS * GLA_DV
W_FOX = FOX_HEADS * FOX_DH
CONV_W = 3

LANES = 128
SUBLANES = 8
VMEM_LIMIT_BYTES = 56 * 1024 * 1024

TM_BIG = 1408
TM_MID = 704
ATTN_BLOCK = 512
DOWN_TK = 1024
GLA_HEADS_PER_STEP = 2
GLA_GROUP = 6
QUERY_PART = 256
ATTN_HEADS_PER_STEP = 2

HALF_ROPE = MLA_ROPE // 2
W_GQK = GLA_HEADS * GLA_DK
COL_CQ = 0
COL_CKV = COL_CQ + MLA_Q_LORA
COL_GV = COL_CKV + MLA_KV_LORA
COL_GR = COL_GV + W_GLA
COL_GQ = COL_GR + W_GLA
COL_GK = COL_GQ + W_GQK
COL_FQ = COL_GK + W_GQK
COL_FK = COL_FQ + W_FOX
COL_FV = COL_FK + W_FOX
COL_KR = COL_FV + W_FOX
N_PROJ = COL_KR + LANES
SRC_CQ = 0
SRC_CKV = SRC_CQ + MLA_Q_LORA
SRC_KR = SRC_CKV + MLA_KV_LORA
SRC_GQ = SRC_KR + MLA_ROPE
SRC_GK = SRC_GQ + W_GQK
SRC_GV = SRC_GK + W_GQK
SRC_GZ = SRC_GV + W_GLA
SRC_GR = SRC_GZ + GLA_GATE_RANK
SRC_FQ = SRC_GR + W_GLA
SRC_FK = SRC_FQ + W_FOX
SRC_FV = SRC_FK + W_FOX
SRC_FZ = SRC_FV + W_FOX
GATE_FZ = GLA_GATE_RANK
ONE_LANE = MLA_ROPE
LOG2E = 1.4426950408889634
V_ROWS = LANES + 16


def _pick(n, target, mult):
    best = None
    for d in range(mult, min(n, target) + 1, mult):
        if n % d == 0:
            best = d
    assert best is not None, (n, target, mult)
    return best


def _params(*sem):
    return pltpu.CompilerParams(dimension_semantics=sem, vmem_limit_bytes=VMEM_LIMIT_BYTES)


def _log_sigmoid(x):
    return jnp.minimum(x, 0.0) - jnp.log1p(jnp.exp(-jnp.abs(x)))


def _silu(x):
    return x / (1.0 + jnp.exp(-x))


def _wspec(w, layer, block, index, **kw):
    if layer is None:
        assert w.ndim == len(block)
        return pl.BlockSpec(block, index, **kw)
    assert w.ndim == len(block) + 1
    return pl.BlockSpec((None,) + tuple(block), lambda *g: (layer,) + tuple(index(*g)), **kw)


def _batch_row0(step, seq_total, tm):
    return (step % (seq_total // tm)) * tm


def _relayout_body(x_ref, *o_refs, plans):
    for o_ref, plan in zip(o_refs, plans):
        width = o_ref.shape[1]
        pos = 0
        for src, dst, w in plan:
            assert dst >= pos
            if dst > pos:
                o_ref[:, pos:dst] = jnp.zeros((o_ref.shape[0], dst - pos), o_ref.dtype)
            o_ref[:, dst:dst + w] = x_ref[:, src:src + w].astype(o_ref.dtype)
            pos = dst + w
        if pos < width:
            o_ref[:, pos:width] = jnp.zeros((o_ref.shape[0], width - pos), o_ref.dtype)


def relayout_cols(w, plans, widths, *, layer=None, rows=256):
    k, n = w.shape[-2:]
    tr = _pick(k, rows, 16)
    return pl.pallas_call(
        functools.partial(_relayout_body, plans=plans),
        grid=(k // tr,),
        in_specs=[_wspec(w, layer, (tr, n), lambda i: (i, 0))],
        out_specs=[pl.BlockSpec((tr, wd), lambda i: (i, 0)) for wd in widths],
        out_shape=[jax.ShapeDtypeStruct((k, wd), BF16) for wd in widths],
        compiler_params=_params("parallel"),
        name="relayout_cols",
    )(w)


def _transpose_cast_body(x_ref, o_ref, *, n):
    row = pl.program_id(0) * LANES + lax.broadcasted_iota(jnp.int32, (LANES, x_ref.shape[2]), 0)
    for layer in range(x_ref.shape[1]):
        x = jnp.where(row < n, x_ref[:, layer, :], 0.0)
        o_ref[layer] = jnp.transpose(x).astype(o_ref.dtype)


def transpose_cast(wt):
    n, depth, k = wt.shape
    nb = pl.cdiv(n, LANES)
    return pl.pallas_call(
        functools.partial(_transpose_cast_body, n=n),
        grid=(nb,),
        in_specs=[pl.BlockSpec((LANES, depth, k), lambda i: (i, 0, 0))],
        out_specs=pl.BlockSpec((depth, k, LANES), lambda i: (0, 0, i)),
        out_shape=jax.ShapeDtypeStruct((depth, k, nb * LANES), BF16),
        compiler_params=_params("parallel"),
        name="transpose_cast",
    )(wt)


W_IN_PLAN = sorted([
    (SRC_CQ, COL_CQ, MLA_Q_LORA), (SRC_CKV, COL_CKV, MLA_KV_LORA), (SRC_KR, COL_KR, MLA_ROPE),
    (SRC_GQ, COL_GQ, W_GQK), (SRC_GK, COL_GK, W_GQK), (SRC_GV, COL_GV, W_GLA),
    (SRC_GR, COL_GR, W_GLA), (SRC_FQ, COL_FQ, W_FOX), (SRC_FK, COL_FK, W_FOX),
    (SRC_FV, COL_FV, W_FOX)], key=lambda p: p[1])
W_GATE_PLAN = [(SRC_GZ, 0, GLA_GATE_RANK), (SRC_FZ, GATE_FZ, FOX_HEADS)]
W_UQ_PLAN = [(h * (MLA_NOPE + MLA_ROPE), h * 2 * LANES, MLA_NOPE + MLA_ROPE) for h in range(MLA_HEADS)]


def _rmsnorm_body(x_ref, g_ref, o_ref):
    x = x_ref[...].astype(F32)
    y = x * lax.rsqrt(jnp.mean(x * x, axis=-1, keepdims=True) + RMS_EPS)
    o_ref[...] = (y * g_ref[...]).astype(o_ref.dtype)


def rmsnorm(x, gain, *, col0=0, width=None, out_dtype=BF16, rows=256):
    m = x.shape[0]
    width = x.shape[1] if width is None else width
    assert col0 % width == 0
    tr = _pick(m, rows, 16)
    return pl.pallas_call(
        _rmsnorm_body,
        grid=(m // tr,),
        in_specs=[pl.BlockSpec((tr, width), lambda i: (i, col0 // width)),
                  pl.BlockSpec((1, width), lambda i: (0, 0))],
        out_specs=pl.BlockSpec((tr, width), lambda i: (i, 0)),
        out_shape=jax.ShapeDtypeStruct((m, width), out_dtype),
        compiler_params=_params("parallel"),
        name="rmsnorm",
    )(x, gain.reshape(1, width).astype(F32))


def _final_norm_body(x_ref, g_ref, o_ref):
    x = x_ref[...]
    y = x * lax.rsqrt(jnp.mean(x * x, axis=-1, keepdims=True) + RMS_EPS)
    o_ref[0] = y * g_ref[...]


def final_rmsnorm(h, gain, batch, seq_total):
    d = h.shape[1]
    tr = LEAD
    nblk = seq_total // tr
    return pl.pallas_call(
        _final_norm_body,
        grid=(batch, nblk - 1),
        in_specs=[pl.BlockSpec((tr, d), lambda b, i: (b * nblk + i + 1, 0)),
                  pl.BlockSpec((1, d), lambda b, i: (0, 0))],
        out_specs=pl.BlockSpec((1, tr, d), lambda b, i: (b, i, 0)),
        out_shape=jax.ShapeDtypeStruct((batch, seq_total - LEAD, d), F32),
        compiler_params=_params("parallel", "parallel"),
        name="final_norm",
    )(h, gain.reshape(1, d).astype(F32))


def _embed_norm_body(x_ref, meta_ref, g_ref, h_ref, hn_ref):
    lead = pl.program_id(1) == 0

    def emit(v):
        h_ref[...] = v
        hn = v * lax.rsqrt(jnp.mean(v * v, axis=-1, keepdims=True) + RMS_EPS) * g_ref[...]
        hn_ref[...] = hn.astype(hn_ref.dtype)

    @pl.when(lead)
    def _():
        d = h_ref.shape[1]
        emit(jnp.concatenate([jnp.zeros((N_PAD, d), F32), meta_ref[...]], axis=0))

    @pl.when(jnp.logical_not(lead))
    def _():
        emit(x_ref[0])


def embed_norm(x, meta_tokens, gain):
    batch, seq, d = x.shape
    assert seq % LEAD == 0
    nblk = seq // LEAD + 1
    t = batch * nblk * LEAD
    row = lambda b, i: (b * nblk + i, 0)
    return pl.pallas_call(
        _embed_norm_body,
        grid=(batch, nblk),
        in_specs=[pl.BlockSpec((1, LEAD, d), lambda b, i: (b, jnp.maximum(i - 1, 0), 0)),
                  pl.BlockSpec((N_META, d), lambda b, i: (0, 0)),
                  pl.BlockSpec((1, d), lambda b, i: (0, 0))],
        out_specs=[pl.BlockSpec((LEAD, d), row), pl.BlockSpec((LEAD, d), row)],
        out_shape=[jax.ShapeDtypeStruct((t, d), F32), jax.ShapeDtypeStruct((t, d), BF16)],
        compiler_params=_params("parallel", "arbitrary"),
        name="embed_norm",
    )(x.astype(F32), meta_tokens.astype(F32), gain.reshape(1, d).astype(F32))


def _mm_body(a_ref, w_ref, o_ref):
    o_ref[...] = jnp.dot(a_ref[...], w_ref[...].astype(BF16),
                         preferred_element_type=F32).astype(o_ref.dtype)


def matmul(a, w, *, tm, tn, out_dtype=BF16, layer=None):
    m, kdim = a.shape
    n = w.shape[-1]
    assert m % tm == 0 and n % tn == 0 and w.shape[-2] == kdim
    return pl.pallas_call(
        _mm_body,
        grid=(m // tm, n // tn),
        in_specs=[pl.BlockSpec((tm, kdim), lambda i, j: (i, 0)),
                  _wspec(w, layer, (kdim, tn), lambda i, j: (0, j))],
        out_specs=pl.BlockSpec((tm, tn), lambda i, j: (i, j)),
        out_shape=jax.ShapeDtypeStruct((m, n), out_dtype),
        compiler_params=_params("parallel", "parallel"),
        name="matmul",
    )(a, w)


def _out_proj_body(a1_ref, a2_ref, a3_ref, w_ref, r_ref, o_ref, wb_ref, *, seq_total, tm):
    i = pl.program_id(1)

    @pl.when(i == 0)
    def _():
        wb_ref[...] = w_ref[...].astype(BF16)

    k1 = a1_ref.shape[1]
    k2 = k1 + a2_ref.shape[1]
    acc = jnp.dot(a1_ref[...], wb_ref[0:k1, :], preferred_element_type=F32)
    acc += jnp.dot(a2_ref[...], wb_ref[k1:k2, :], preferred_element_type=F32)
    acc += jnp.dot(a3_ref[...], wb_ref[k2:, :], preferred_element_type=F32)
    pos = _batch_row0(i, seq_total, tm) + lax.broadcasted_iota(jnp.int32, acc.shape, 0)
    o_ref[...] = r_ref[...] + jnp.where(pos >= N_PAD, acc, 0.0)


def out_proj(a1, a2, a3, w, res, *, seq_total, tm, tn, layer=None):
    m = a1.shape[0]
    kdim, n = w.shape[-2:]
    assert a1.shape[1] + a2.shape[1] + a3.shape[1] == kdim
    assert m % tm == 0 and n % tn == 0 and seq_total % tm == 0
    a_spec = lambda a: pl.BlockSpec((tm, a.shape[1]), lambda j, i: (i, 0))
    return pl.pallas_call(
        functools.partial(_out_proj_body, seq_total=seq_total, tm=tm),
        grid=(n // tn, m // tm),
        in_specs=[a_spec(a1), a_spec(a2), a_spec(a3),
                  _wspec(w, layer, (kdim, tn), lambda j, i: (0, j), pipeline_mode=pl.Buffered(1)),
                  pl.BlockSpec((tm, tn), lambda j, i: (i, j))],
        out_specs=pl.BlockSpec((tm, tn), lambda j, i: (i, j)),
        out_shape=jax.ShapeDtypeStruct((m, n), F32),
        scratch_shapes=[pltpu.VMEM((kdim, tn), BF16)],
        compiler_params=_params("parallel", "arbitrary"),
        name="out_proj",
    )(a1, a2, a3, w, res)


CONV_CHUNK = 32


def _ffn_up_body(a_ref, wg_ref, wv_ref, cwg_ref, cwv_ref, cbg_ref, cbv_ref, o_ref,
                 wcat_ref, u_ref, halo_ref, *, seq_total, tm, tc):
    i = pl.program_id(0)
    j = pl.program_id(1)
    wcat_ref[:, 0:tc] = wg_ref[...].astype(BF16)
    wcat_ref[:, tc:2 * tc] = wv_ref[...].astype(BF16)
    row0 = _batch_row0(i, seq_total, tm)
    u = jnp.dot(a_ref[...], wcat_ref[...], preferred_element_type=F32)
    pos = row0 + lax.broadcasted_iota(jnp.int32, u.shape, 0)
    u_ref[SUBLANES:SUBLANES + tm, :] = jnp.where(pos >= N_PAD, u, 0.0)

    @pl.when(row0 == 0)
    def _():
        u_ref[0:SUBLANES, :] = jnp.zeros((SUBLANES, 2 * tc), F32)

    @pl.when(row0 > 0)
    def _():
        u_ref[0:SUBLANES, :] = halo_ref[j]

    halo_ref[j] = u_ref[tm:tm + SUBLANES, :]
    cw = jnp.concatenate([cwg_ref[...], cwv_ref[...]], axis=1)
    cb = jnp.concatenate([cbg_ref[...], cbv_ref[...]], axis=1)
    for r0 in range(0, tm, CONV_CHUNK):
        cv = cb
        for tap in range(CONV_W):
            lo = SUBLANES + r0 - (CONV_W - 1 - tap)
            cv = cv + cw[tap:tap + 1, :] * u_ref[lo:lo + CONV_CHUNK, :]
        o_ref[r0:r0 + CONV_CHUNK, :] = (_silu(cv[:, 0:tc]) * cv[:, tc:2 * tc]).astype(o_ref.dtype)


def ffn_up(hn, w_up, conv_w, conv_b, *, seq_total, tm, tc, d_ff, layer=None):
    m, kdim = hn.shape
    assert m % tm == 0 and seq_total % tm == 0 and tm % CONV_CHUNK == 0 and d_ff % tc == 0
    nj = d_ff // tc
    wspec = lambda off: _wspec(w_up, layer, (kdim, tc), lambda i, j: (0, j + off))
    cspec = lambda c, r, off: _wspec(c, layer, (r, tc), lambda i, j: (0, j + off))
    conv_b = conv_b.reshape(conv_b.shape[:-1] + (1, conv_b.shape[-1]))
    return pl.pallas_call(
        functools.partial(_ffn_up_body, seq_total=seq_total, tm=tm, tc=tc),
        grid=(m // tm, nj),
        in_specs=[pl.BlockSpec((tm, kdim), lambda i, j: (i, 0)),
                  wspec(0), wspec(nj), cspec(conv_w, CONV_W, 0), cspec(conv_w, CONV_W, nj),
                  cspec(conv_b, 1, 0), cspec(conv_b, 1, nj)],
        out_specs=pl.BlockSpec((tm, tc), lambda i, j: (i, j)),
        out_shape=jax.ShapeDtypeStruct((m, d_ff), BF16),
        scratch_shapes=[pltpu.VMEM((kdim, 2 * tc), BF16),
                        pltpu.VMEM((tm + SUBLANES, 2 * tc), F32),
                        pltpu.VMEM((nj, SUBLANES, 2 * tc), F32)],
        compiler_params=_params("arbitrary", "arbitrary"),
        name="ffn_up",
    )(hn, w_up, w_up, conv_w, conv_w, conv_b, conv_b)


def _ffn_down_body(a_ref, w_ref, r_ref, o_ref, *, seq_total, tm, tk, d_ff, nk):
    k = pl.program_id(2)

    def part(a, w):
        return jnp.dot(a, w.astype(BF16), preferred_element_type=F32)

    @pl.when(k == 0)
    def _():
        o_ref[...] = part(a_ref[...], w_ref[...])

    @pl.when(jnp.logical_and(k > 0, k < nk - 1))
    def _():
        o_ref[...] += part(a_ref[...], w_ref[...])

    @pl.when(k == nk - 1)
    def _():
        acol = k * tk + lax.broadcasted_iota(jnp.int32, a_ref.shape, 1)
        a = jnp.where(acol < d_ff, a_ref[...], jnp.zeros(a_ref.shape, a_ref.dtype))
        wrow = k * tk + lax.broadcasted_iota(jnp.int32, w_ref.shape, 0)
        w = jnp.where(wrow < d_ff, w_ref[...], 0.0)
        pos = _batch_row0(pl.program_id(0), seq_total, tm) + lax.broadcasted_iota(
            jnp.int32, o_ref.shape, 0)
        o_ref[...] = r_ref[...] + jnp.where(pos >= N_PAD, o_ref[...] + part(a, w), 0.0)


def ffn_down(act, w, res, *, seq_total, tm, tn, tk, layer=None):
    m, d_ff = act.shape
    n = w.shape[-1]
    assert m % tm == 0 and n % tn == 0 and seq_total % tm == 0 and w.shape[-2] == d_ff
    nk = pl.cdiv(d_ff, tk)
    assert nk >= 2
    return pl.pallas_call(
        functools.partial(_ffn_down_body, seq_total=seq_total, tm=tm, tk=tk, d_ff=d_ff, nk=nk),
        grid=(m // tm, n // tn, nk),
        in_specs=[pl.BlockSpec((tm, tk), lambda i, j, k: (i, k)),
                  _wspec(w, layer, (tk, tn), lambda i, j, k: (k, j)),
                  pl.BlockSpec((tm, tn), lambda i, j, k: (i, j))],
        out_specs=pl.BlockSpec((tm, tn), lambda i, j, k: (i, j)),
        out_shape=jax.ShapeDtypeStruct((m, n), F32),
        compiler_params=_params("parallel", "parallel", "arbitrary"),
        name="ffn_down",
    )(act, w, res)


def _scores(k_rows, q_parts):
    out = []
    for q in q_parts:
        s = jnp.dot(k_rows, q, preferred_element_type=F32)
        out.append((s, jnp.max(s, axis=0, keepdims=True)))
    return tuple(out)


def _causal(scored, col0s):
    out = []
    for (s, _), c0 in zip(scored, col0s):
        key = lax.broadcasted_iota(jnp.int32, s.shape, 0)
        qry = c0 + lax.broadcasted_iota(jnp.int32, s.shape, 1)
        s = jnp.where(key <= qry, s, NEG_INF)
        out.append((s, jnp.max(s, axis=0, keepdims=True)))
    return tuple(out)


def _softmax_pv(carry, scored, v_ext):
    out = []
    for (m, acc), (s, s_max) in zip(carry, scored):
        m_new = jnp.maximum(m, s_max)
        alpha = jnp.exp2(m - m_new)
        p = jnp.exp2(s - m_new).astype(BF16)
        out.append((m_new, alpha * acc + jnp.dot(v_ext, p, preferred_element_type=F32)))
    return tuple(out)


def _attn_init(widths):
    return tuple((jnp.full((1, w), NEG_INF, F32), jnp.zeros((V_ROWS, w), F32)) for w in widths)


def _attn_finish(carry, gain, o_ref, row0, col0s):
    for (_, acc), c0 in zip(carry, col0s):
        o = jnp.transpose(acc[0:LANES, :] / acc[LANES:LANES + 1, :])
        o = o * lax.rsqrt(jnp.mean(o * o, axis=-1, keepdims=True) + RMS_EPS) * gain
        o_ref[pl.ds(row0 + c0, o.shape[0]), :] = o.astype(o_ref.dtype)


def _query_parts(tb):
    part = QUERY_PART if tb % QUERY_PART == 0 else tb
    return part, tuple(range(0, tb, part))


def _attention_core(qt, qt0, ks, vt, vt0, s_scr, m_scr, g_ref, o_ref, *, nblk, tb):
    heads = range(qt.shape[0])
    gains = [g_ref[:, h * LANES:(h + 1) * LANES] for h in heads]
    outs = [o_ref.at[:, pl.ds(h * LANES, LANES)] for h in heads]

    for h in heads:
        lead = _causal(_scores(ks[h, 0:LEAD, :], (qt0[h],)), (0,))
        _attn_finish(_softmax_pv(_attn_init((LEAD,)), lead, vt0[h]), gains[h], outs[h], 0, (0,))

    part, col0s = _query_parts(tb)
    widths = (part,) * len(col0s)

    def q_block(qi, _):
        q_parts = [tuple(qt[h, qi, :, c0:c0 + part] for c0 in col0s) for h in heads]

        def score(slot, kj):
            for h in heads:
                k_rows = ks[h, pl.ds(pl.multiple_of(LEAD + kj * tb, LANES), tb), :]
                for p, (s, s_max) in enumerate(_scores(k_rows, q_parts[h])):
                    s_scr[h, slot, p] = s
                    m_scr[h, slot, p] = s_max

        def scored(h, slot):
            return tuple((s_scr[h, slot, p], m_scr[h, slot, p]) for p in range(len(col0s)))

        def consume(carry, slot, kj, causal):
            out = []
            for h in heads:
                sc = scored(h, slot)
                if causal:
                    sc = _causal(sc, col0s)
                out.append(_softmax_pv(carry[h], sc, vt[h, kj]))
            return tuple(out)

        carry = tuple(_softmax_pv(_attn_init(widths), _scores(ks[h, 0:LEAD, :], q_parts[h]), vt0[h])
                      for h in heads)
        score(0, 0)

        def pair(t, carry):
            kj = 2 * t
            score(1, kj + 1)
            carry = consume(carry, 0, kj, False)
            score(0, kj + 2)
            return consume(carry, 1, kj + 1, False)

        carry = lax.fori_loop(0, qi // 2, pair, carry)

        def odd_tail(carry):
            score(1, qi)
            return consume(consume(carry, 0, qi - 1, False), 1, qi, True)

        def even_tail(carry):
            return consume(carry, 0, qi, True)

        carry = lax.cond(qi % 2 == 1, odd_tail, even_tail, carry)
        for h in heads:
            _attn_finish(carry[h], gains[h], outs[h], pl.multiple_of(LEAD + qi * tb, LANES), col0s)
        return 0

    lax.fori_loop(0, nblk, q_block, 0)


def _store_v_ext(vt_dst, v_rows):
    n = v_rows.shape[0]
    vt_dst[0:LANES, :] = _to_t(v_rows)
    r = lax.broadcasted_iota(jnp.int32, (V_ROWS - LANES, n), 0)
    vt_dst[LANES:V_ROWS, :] = jnp.where(r == 0, 1.0, 0.0).astype(BF16)


def _to_t(x):
    return jnp.transpose(x.astype(F32)).astype(BF16)


def _mla_attn_body(q_ref, kv_ref, kr_ref, cos_ref, sa_ref, sb_ref, g_ref, o_ref,
                   qt, qt0, ks, vt, vt0, s_scr, m_scr, *, nblk, tb):
    scale = (MLA_NOPE + MLA_ROPE) ** -0.5 * LOG2E
    heads = range(qt.shape[0])

    def rope(x, rows):
        return (x * cos_ref[rows, :] + pltpu.roll(x, HALF_ROPE, 1) * sa_ref[rows, :]
                + pltpu.roll(x, LANES - HALF_ROPE, 1) * sb_ref[rows, :])

    def prep(r0, n, qt_dst, vt_dst):
        rows = pl.ds(r0, n)
        one = lax.broadcasted_iota(jnp.int32, (n, LANES), 1) == ONE_LANE
        pos = r0 + lax.broadcasted_iota(jnp.int32, (n, LANES), 0)
        kp = rope(kr_ref[rows, :].astype(F32), rows)
        kp = jnp.where(one, jnp.where(pos < N_PAD, NEG_INF, 0.0), kp).astype(BF16)
        for h in heads:
            c0 = h * 2 * LANES
            qn = q_ref[rows, pl.ds(c0, LANES)].astype(F32) * scale
            qp = rope(q_ref[rows, pl.ds(c0 + LANES, LANES)].astype(F32), rows) * scale
            qp = jnp.where(one, 1.0, qp)
            qt_dst(h)[0:LANES, :] = jnp.transpose(qn).astype(BF16)
            qt_dst(h)[LANES:2 * LANES, :] = jnp.transpose(qp).astype(BF16)
            ks[h, rows, pl.ds(0, LANES)] = kv_ref[rows, pl.ds(c0, LANES)]
            ks[h, rows, pl.ds(LANES, LANES)] = kp
            _store_v_ext(vt_dst(h), kv_ref[rows, pl.ds(c0 + LANES, LANES)])

    prep(0, LEAD, lambda h: qt0.at[h], lambda h: vt0.at[h])

    def prep_block(i, _):
        prep(pl.multiple_of(LEAD + i * tb, LANES), tb, lambda h: qt.at[h, i], lambda h: vt.at[h, i])
        return 0

    lax.fori_loop(0, nblk, prep_block, 0)
    _attention_core(qt, qt0, ks, vt, vt0, s_scr, m_scr, g_ref, o_ref, nblk=nblk, tb=tb)


def _attn_scratch(heads, seq_total, nblk, tb):
    part, col0s = _query_parts(tb)
    return [pltpu.VMEM((heads, nblk, 2 * LANES, tb), BF16), pltpu.VMEM((heads, 2 * LANES, LEAD), BF16),
            pltpu.VMEM((heads, seq_total, 2 * LANES), BF16),
            pltpu.VMEM((heads, nblk, V_ROWS, tb), BF16), pltpu.VMEM((heads, V_ROWS, LEAD), BF16),
            pltpu.VMEM((heads, 2, len(col0s), tb, part), F32),
            pltpu.VMEM((heads, 2, len(col0s), 1, part), F32)]


def mla_attention(q_raw, kv, proj, cos_t, sin_a, sin_b, gain, *, batch, seq_total, tb):
    t = q_raw.shape[0]
    nblk = (seq_total - LEAD) // tb
    hps = ATTN_HEADS_PER_STEP
    assert MLA_HEADS % hps == 0
    big = lambda w: pl.BlockSpec((seq_total, hps * w), lambda b, h: (b, h))
    table = pl.BlockSpec((seq_total, LANES), lambda b, h: (0, 0), pipeline_mode=pl.Buffered(1))
    return pl.pallas_call(
        functools.partial(_mla_attn_body, nblk=nblk, tb=tb),
        grid=(batch, MLA_HEADS // hps),
        in_specs=[big(2 * LANES), big(2 * LANES),
                  pl.BlockSpec((seq_total, LANES), lambda b, h: (b, COL_KR // LANES)),
                  table, table, table,
                  pl.BlockSpec((1, hps * LANES), lambda b, h: (0, h))],
        out_specs=big(LANES),
        out_shape=jax.ShapeDtypeStruct((t, W_MLA), BF16),
        scratch_shapes=_attn_scratch(hps, seq_total, nblk, tb),
        compiler_params=_params("parallel", "parallel"),
        name="mla_attention",
    )(q_raw, kv, proj, cos_t, sin_a, sin_b, gain.reshape(1, W_MLA).astype(F32))


def _fox_attn_body(q_ref, k_ref, v_ref, aq_ref, ak_ref, g_ref, o_ref,
                   qt, qt0, ks, vt, vt0, s_scr, m_scr, *, nblk, tb):
    scale = FOX_DH ** -0.5 * LOG2E
    heads = range(qt.shape[0])

    def prep(r0, n, qt_dst, vt_dst):
        rows = pl.ds(r0, n)
        for h in heads:
            cols = pl.ds(h * LANES, LANES)
            qt_dst(h)[0:LANES, :] = jnp.transpose(q_ref[rows, cols].astype(F32) * scale).astype(BF16)
            qt_dst(h)[LANES:2 * LANES, :] = _to_t(aq_ref[rows, cols])
            ks[h, rows, pl.ds(0, LANES)] = k_ref[rows, cols]
            ks[h, rows, pl.ds(LANES, LANES)] = ak_ref[rows, cols]
            _store_v_ext(vt_dst(h), v_ref[rows, cols])

    prep(0, LEAD, lambda h: qt0.at[h], lambda h: vt0.at[h])

    def prep_block(i, _):
        prep(pl.multiple_of(LEAD + i * tb, LANES), tb, lambda h: qt.at[h, i], lambda h: vt.at[h, i])
        return 0

    lax.fori_loop(0, nblk, prep_block, 0)
    _attention_core(qt, qt0, ks, vt, vt0, s_scr, m_scr, g_ref, o_ref, nblk=nblk, tb=tb)


def fox_attention(proj, aug_q, aug_k, gain, *, batch, seq_total, tb):
    t = proj.shape[0]
    nblk = (seq_total - LEAD) // tb
    hps = ATTN_HEADS_PER_STEP
    assert FOX_HEADS % hps == 0
    wb = hps * LANES
    col = lambda c0: pl.BlockSpec((seq_total, wb), lambda b, h: (b, c0 // wb + h))
    assert COL_FQ % wb == 0 and COL_FK % wb == 0 and COL_FV % wb == 0
    return pl.pallas_call(
        functools.partial(_fox_attn_body, nblk=nblk, tb=tb),
        grid=(batch, FOX_HEADS // hps),
        in_specs=[col(COL_FQ), col(COL_FK), col(COL_FV), col(0), col(0),
                  pl.BlockSpec((1, wb), lambda b, h: (0, h))],
        out_specs=col(0),
        out_shape=jax.ShapeDtypeStruct((t, W_FOX), BF16),
        scratch_shapes=_attn_scratch(hps, seq_total, nblk, tb),
        compiler_params=_params("parallel", "parallel"),
        name="fox_attention",
    )(proj, proj, proj, aug_q, aug_k, gain.reshape(1, W_FOX).astype(F32))


def _split3(x):
    hi = x.astype(BF16)
    r1 = x - hi.astype(F32)
    mid = r1.astype(BF16)
    lo = (r1 - mid.astype(F32)).astype(BF16)
    return hi.astype(F32), mid.astype(F32), lo.astype(F32)


def _fox_prep_body(z_ref, b_ref, aq_ref, ak_ref, carry_ref, *, tr):
    i = pl.program_id(1)

    @pl.when(i == 0)
    def _():
        carry_ref[...] = jnp.zeros_like(carry_ref)

    row = i * tr + lax.broadcasted_iota(jnp.int32, (tr, LANES), 0)
    valid = row >= N_PAD
    log_f = jnp.where(valid, _log_sigmoid(z_ref[...] + b_ref[...]), 0.0)
    r = lax.broadcasted_iota(jnp.int32, (tr, tr), 0)
    c = lax.broadcasted_iota(jnp.int32, (tr, tr), 1)
    tri = jnp.where(c <= r, 1.0, 0.0).astype(F32)
    csum = jnp.dot(tri, log_f, preferred_element_type=F32,
                   precision=lax.Precision.HIGHEST) + carry_ref[0:1, :]
    carry_ref[...] = jnp.broadcast_to(csum[tr - 1:tr, :], carry_ref.shape)

    lane = lax.broadcasted_iota(jnp.int32, (tr, LANES), 1)
    key_mask = jnp.where(valid, 0.0, NEG_INF)
    for h in range(FOX_HEADS):
        col = csum[:, GATE_FZ + h:GATE_FZ + h + 1] * LOG2E
        hi, mid, lo = _split3(col)
        aq = jnp.where(lane == 0, hi, jnp.where(lane == 1, mid, jnp.where(lane == 2, lo,
             jnp.where(lane < 7, 1.0, 0.0))))
        ak = jnp.where(lane < 3, 1.0, jnp.where(lane == 3, -hi, jnp.where(lane == 4, -mid,
             jnp.where(lane == 5, -lo, jnp.where(lane == 6, key_mask, 0.0)))))
        aq_ref[:, h * LANES:(h + 1) * LANES] = aq.astype(BF16)
        ak_ref[:, h * LANES:(h + 1) * LANES] = ak.astype(BF16)


def fox_prep(gates, fox_b, *, batch, seq_total):
    t = gates.shape[0]
    tr = _pick(seq_total, 384, LANES)
    nblk = seq_total // tr
    bias = jnp.zeros((1, LANES), F32).at[0, GATE_FZ:GATE_FZ + FOX_HEADS].set(fox_b.astype(F32))
    out = jax.ShapeDtypeStruct((t, W_FOX), BF16)
    return pl.pallas_call(
        functools.partial(_fox_prep_body, tr=tr),
        grid=(batch, nblk),
        in_specs=[pl.BlockSpec((tr, LANES), lambda b, i: (b * nblk + i, 0)),
                  pl.BlockSpec((1, LANES), lambda b, i: (0, 0))],
        out_specs=[pl.BlockSpec((tr, W_FOX), lambda b, i: (b * nblk + i, 0))] * 2,
        out_shape=[out, out],
        scratch_shapes=[pltpu.VMEM((SUBLANES, LANES), F32)],
        compiler_params=_params("parallel", "arbitrary"),
        name="fox_prep",
    )(gates, bias)


def _gla_body(q_ref, k_ref, v_ref, r_ref, z_ref, w2_ref, b2_ref, g_ref, o_ref, s_ref,
              *, seq_total):
    c = GLA_CHUNK
    nchunk = seq_total // c
    gsz = max(g for g in range(1, GLA_GROUP + 1) if nchunk % g == 0)
    rg = gsz * c
    heads = s_ref.shape[0]
    nt = (((1,), (1,)), ((), ()))
    tn = (((0,), (0,)), ((), ()))
    r_i = lax.broadcasted_iota(jnp.int32, (rg, rg), 0)
    c_i = lax.broadcasted_iota(jnp.int32, (rg, rg), 1)
    assert c & (c - 1) == 0
    lower = jnp.logical_and(c_i <= r_i, c_i >= jnp.bitwise_and(r_i, -c))
    tri = jnp.where(lower, 1.0, 0.0).astype(F32)
    s_ref[...] = jnp.zeros_like(s_ref)

    def group(i, _):
        r0 = pl.multiple_of(i * rg, rg)
        rows = pl.ds(r0, rg)
        valid = (r0 + lax.broadcasted_iota(jnp.int32, (rg, GLA_DK), 0)) >= N_PAD
        z = z_ref[rows, :].astype(BF16)
        for hh in range(heads):
            kcols = pl.ds(hh * GLA_DK, GLA_DK)
            vcols = pl.ds(hh * GLA_DV, GLA_DV)
            q = q_ref[rows, kcols].astype(F32) * (GLA_DK ** -0.5)
            k = jnp.where(valid, k_ref[rows, kcols].astype(F32), 0.0)
            v = v_ref[rows, vcols]
            logit = jnp.dot(z, w2_ref[:, kcols], preferred_element_type=F32) + b2_ref[:, kcols]
            log_a = jnp.where(valid, _log_sigmoid(logit) / GLA_TAU, 0.0)
            bc = jnp.dot(tri, log_a, preferred_element_type=F32, precision=lax.Precision.HIGHEST)
            b_last = jnp.concatenate(
                [jnp.broadcast_to(bc[(n + 1) * c - 1:(n + 1) * c, :], (c, GLA_DK)) for n in range(gsz)], axis=0)
            q_dec = (q * jnp.exp(bc)).astype(BF16)
            k_dec = (k * jnp.exp(-bc)).astype(BF16)
            k_state = (k * jnp.exp(b_last - bc)).astype(BF16)
            a = lax.dot_general(q_dec, k_dec, nt, preferred_element_type=F32)
            a = jnp.where(lower, a, 0.0).astype(BF16)
            o_intra = jnp.dot(a, v, preferred_element_type=F32)
            state = s_ref[hh]
            outs = []
            for n in range(gsz):
                sl = slice(n * c, (n + 1) * c)
                outs.append(o_intra[sl] + jnp.dot(q_dec[sl], state.astype(BF16),
                                                   preferred_element_type=F32))
                decay = jnp.exp(b_last[n * c:n * c + 1, :])
                dec_t = jnp.transpose(jnp.broadcast_to(decay, (GLA_DK, GLA_DK)))
                dec_t = jnp.concatenate([dec_t, dec_t], axis=1)
                state = state * dec_t + lax.dot_general(k_state[sl], v[sl], tn,
                                                        preferred_element_type=F32)
            s_ref[hh] = state
            o = jnp.concatenate(outs, axis=0)
            o = o * lax.rsqrt(jnp.mean(o * o, axis=-1, keepdims=True) + RMS_EPS) * g_ref[:, vcols]
            o_ref[rows, vcols] = (o * _silu(r_ref[rows, vcols].astype(F32))).astype(o_ref.dtype)
        return 0

    lax.fori_loop(0, nchunk // gsz, group, 0)


def gla(proj, gates, w2, b2, gain, *, batch, seq_total):
    t = proj.shape[0]
    hps = GLA_HEADS_PER_STEP
    assert GLA_HEADS % hps == 0
    wk, wv = hps * GLA_DK, hps * GLA_DV
    blk = lambda w, c0: pl.BlockSpec((seq_total, w), lambda b, h: (b, c0 // w + h))
    return pl.pallas_call(
        functools.partial(_gla_body, seq_total=seq_total),
        grid=(batch, GLA_HEADS // hps),
        in_specs=[blk(wk, COL_GQ), blk(wk, COL_GK), blk(wv, COL_GV), blk(wv, COL_GR),
                  pl.BlockSpec((seq_total, LANES), lambda b, h: (b, 0)),
                  pl.BlockSpec((LANES, wk), lambda b, h: (0, h)),
                  pl.BlockSpec((1, wk), lambda b, h: (0, h)),
                  pl.BlockSpec((1, wv), lambda b, h: (0, h))],
        out_specs=pl.BlockSpec((seq_total, wv), lambda b, h: (b, h)),
        out_shape=jax.ShapeDtypeStruct((t, W_GLA), BF16),
        scratch_shapes=[pltpu.VMEM((hps, GLA_DK, GLA_DV), F32)],
        compiler_params=_params("parallel", "parallel"),
        name="gla",
    )(proj, proj, proj, proj, gates, w2, b2, gain.reshape(1, W_GLA).astype(F32))


def _rope_tables(seq_total):
    pos = np.maximum(np.arange(seq_total) - N_PAD, 0).astype(np.float32)
    inv_freq = (1.0 / (ROPE_THETA ** (np.arange(0, MLA_ROPE, 2, dtype=np.float32) / MLA_ROPE))).astype(np.float32)
    ang = jnp.asarray(pos)[:, None] * jnp.asarray(inv_freq)[None, :]
    cos, sin = jnp.cos(ang), jnp.sin(ang)
    z = jnp.zeros_like(cos)
    cos_t = jnp.concatenate([cos, cos, z, z], axis=1)
    sin_a = jnp.concatenate([z, sin, z, z], axis=1)
    sin_b = jnp.concatenate([-sin, z, z, z], axis=1)
    return cos_t, sin_a, sin_b


def kernel(x, meta_tokens, attn_norm, w_in, mla_q_norm, mla_w_uq, mla_kv_norm, mla_w_ukv,
           gla_w_gate2, gla_b_gate, fox_b_f, out_norm_mla, out_norm_gla, out_norm_fox,
           w_out, ffn_norm, ffn_w_up, ffn_conv_w, ffn_conv_b, ffn_w_down, final_norm):
    batch, seq, d_model = x.shape
    depth = w_in.shape[0]
    d_ff = ffn_w_down.shape[1]
    seq_total = LEAD + seq
    cos_t, sin_a, sin_b = _rope_tables(seq_total)

    tm_big = _pick(seq_total, TM_BIG, LANES)
    tm_mid = _pick(seq_total, TM_MID, LANES)
    tb = _pick(seq, ATTN_BLOCK, LANES)
    tc = _pick(d_ff, 256, LANES)

    w_in_b = transpose_cast(jnp.transpose(w_in, (2, 0, 1)))

    h, hn = embed_norm(x, meta_tokens, attn_norm[0])
    for layer in range(depth):
        w_big, w_gates = relayout_cols(w_in_b, (W_IN_PLAN, W_GATE_PLAN), (N_PROJ, LANES), layer=layer)
        w_uq, = relayout_cols(mla_w_uq, (W_UQ_PLAN,), (MLA_HEADS * 2 * LANES,), layer=layer)
        if layer > 0:
            hn = rmsnorm(h, attn_norm[layer])
        proj = matmul(hn, w_big, tm=tm_big, tn=_pick(N_PROJ, 896, LANES))
        gates = matmul(hn, w_gates, tm=tm_big, tn=LANES, out_dtype=F32)

        cq = rmsnorm(proj, mla_q_norm[layer], col0=COL_CQ, width=MLA_Q_LORA)
        ckv = rmsnorm(proj, mla_kv_norm[layer], col0=COL_CKV, width=MLA_KV_LORA)
        q_raw = matmul(cq, w_uq, tm=tm_big, tn=1024)
        kv = matmul(ckv, mla_w_ukv, tm=tm_big, tn=1024, layer=layer)
        o_mla = mla_attention(q_raw, kv, proj, cos_t, sin_a, sin_b, out_norm_mla[layer],
                              batch=batch, seq_total=seq_total, tb=tb)

        w2 = jnp.zeros((LANES, W_GQK), F32).at[:GLA_GATE_RANK].set(gla_w_gate2[layer])
        o_gla = gla(proj, gates, w2.astype(BF16), gla_b_gate[layer].reshape(1, -1),
                    out_norm_gla[layer], batch=batch, seq_total=seq_total)

        aug_q, aug_k = fox_prep(gates, fox_b_f[layer], batch=batch, seq_total=seq_total)
        o_fox = fox_attention(proj, aug_q, aug_k, out_norm_fox[layer],
                              batch=batch, seq_total=seq_total, tb=tb)

        h = out_proj(o_mla, o_gla, o_fox, w_out, h, seq_total=seq_total, tm=tm_mid, tn=1024,
                     layer=layer)

        hn = rmsnorm(h, ffn_norm[layer])
        act = ffn_up(hn, ffn_w_up, ffn_conv_w, ffn_conv_b, seq_total=seq_total, tm=tm_big,
                     tc=tc, d_ff=d_ff, layer=layer)
        h = ffn_down(act, ffn_w_down, h, seq_total=seq_total, tm=tm_big, tn=1024, tk=DOWN_TK,
                     layer=layer)

    return final_rmsnorm(h, final_norm, batch, seq_total)
```

```python
import functools

import jax
import jax.numpy as jnp
import numpy as np
from jax import lax
from jax.experimental import pallas as pl
from jax.experimental.pallas import tpu as pltpu

F32 = jnp.float32
BF16 = jnp.bfloat16

N_META = 16
LEAD = 128
N_PAD = LEAD - N_META
RMS_EPS = 1e-6
NEG_INF = -1e30

MLA_V = 128
MLA_HEADS = 12
MLA_NOPE = 128
MLA_ROPE = 64
MLA_Q_LORA = 1536
MLA_KV_LORA = 512
ROPE_THETA = 10000.0

GLA_DV = 256
GLA_DK = 128
GLA_HEADS = 4
GLA_GATE_RANK = 16
GLA_TAU = 16.0
GLA_CHUNK = 64

FOX_DH = 128
FOX_HEADS = 12

W_MLA = MLA_HEADS * MLA_V
W_GLA = GLA_HEADS * GLA_DV
W_FOX = FOX_HEADS * FOX_DH
CONV_W = 3

LANES = 128
SUBLANES = 8
VMEM_LIMIT_BYTES = 56 * 1024 * 1024

TM_BIG = 1408
TM_MID = 704
ATTN_BLOCK = 512
DOWN_TK = 1024
GLA_HEADS_PER_STEP = 2
GLA_GROUP = 6
QUERY_PART = 256
ATTN_HEADS_PER_STEP = 2

HALF_ROPE = MLA_ROPE // 2
W_GQK = GLA_HEADS * GLA_DK
COL_CQ = 0
COL_CKV = COL_CQ + MLA_Q_LORA
COL_GV = COL_CKV + MLA_KV_LORA
COL_GR = COL_GV + W_GLA
COL_GQ = COL_GR + W_GLA
COL_GK = COL_GQ + W_GQK
COL_FQ = COL_GK + W_GQK
COL_FK = COL_FQ + W_FOX
COL_FV = COL_FK + W_FOX
COL_KR = COL_FV + W_FOX
N_PROJ = COL_KR + LANES
SRC_CQ = 0
SRC_CKV = SRC_CQ + MLA_Q_LORA
SRC_KR = SRC_CKV + MLA_KV_LORA
SRC_GQ = SRC_KR + MLA_ROPE
SRC_GK = SRC_GQ + W_GQK
SRC_GV = SRC_GK + W_GQK
SRC_GZ = SRC_GV + W_GLA
SRC_GR = SRC_GZ + GLA_GATE_RANK
SRC_FQ = SRC_GR + W_GLA
SRC_FK = SRC_FQ + W_FOX
SRC_FV = SRC_FK + W_FOX
SRC_FZ = SRC_FV + W_FOX
GATE_FZ = GLA_GATE_RANK
ONE_LANE = MLA_ROPE
LOG2E = 1.4426950408889634
V_ROWS = LANES + 16


def _pick(n, target, mult):
    best = None
    for d in range(mult, min(n, target) + 1, mult):
        if n % d == 0:
            best = d
    assert best is not None, (n, target, mult)
    return best


def _params(*sem):
    return pltpu.CompilerParams(dimension_semantics=sem, vmem_limit_bytes=VMEM_LIMIT_BYTES)


def _log_sigmoid(x):
    return jnp.minimum(x, 0.0) - jnp.log1p(jnp.exp(-jnp.abs(x)))


def _silu(x):
    return x / (1.0 + jnp.exp(-x))


def _wspec(w, layer, block, index, **kw):
    if layer is None:
        assert w.ndim == len(block)
        return pl.BlockSpec(block, index, **kw)
    assert w.ndim == len(block) + 1
    return pl.BlockSpec((None,) + tuple(block), lambda *g: (layer,) + tuple(index(*g)), **kw)


def _batch_row0(step, seq_total, tm):
    return (step % (seq_total // tm)) * tm


def _relayout_body(x_ref, *o_refs, plans):
    for o_ref, plan in zip(o_refs, plans):
        width = o_ref.shape[1]
        pos = 0
        for src, dst, w in plan:
            assert dst >= pos
            if dst > pos:
                o_ref[:, pos:dst] = jnp.zeros((o_ref.shape[0], dst - pos), o_ref.dtype)
            o_ref[:, dst:dst + w] = x_ref[:, src:src + w].astype(o_ref.dtype)
            pos = dst + w
        if pos < width:
            o_ref[:, pos:width] = jnp.zeros((o_ref.shape[0], width - pos), o_ref.dtype)


def relayout_cols(w, plans, widths, *, layer=None, rows=256):
    k, n = w.shape[-2:]
    tr = _pick(k, rows, 16)
    return pl.pallas_call(
        functools.partial(_relayout_body, plans=plans),
        grid=(k // tr,),
        in_specs=[_wspec(w, layer, (tr, n), lambda i: (i, 0))],
        out_specs=[pl.BlockSpec((tr, wd), lambda i: (i, 0)) for wd in widths],
        out_shape=[jax.ShapeDtypeStruct((k, wd), BF16) for wd in widths],
        compiler_params=_params("parallel"),
        name="relayout_cols",
    )(w)


def _transpose_cast_body(x_ref, o_ref, *, n):
    row = pl.program_id(0) * LANES + lax.broadcasted_iota(jnp.int32, (LANES, x_ref.shape[2]), 0)
    for layer in range(x_ref.shape[1]):
        x = jnp.where(row < n, x_ref[:, layer, :], 0.0)
        o_ref[layer] = jnp.transpose(x).astype(o_ref.dtype)


def transpose_cast(wt):
    n, depth, k = wt.shape
    nb = pl.cdiv(n, LANES)
    return pl.pallas_call(
        functools.partial(_transpose_cast_body, n=n),
        grid=(nb,),
        in_specs=[pl.BlockSpec((LANES, depth, k), lambda i: (i, 0, 0))],
        out_specs=pl.BlockSpec((depth, k, LANES), lambda i: (0, 0, i)),
        out_shape=jax.ShapeDtypeStruct((depth, k, nb * LANES), BF16),
        compiler_params=_params("parallel"),
        name="transpose_cast",
    )(wt)


W_IN_PLAN = sorted([
    (SRC_CQ, COL_CQ, MLA_Q_LORA), (SRC_CKV, COL_CKV, MLA_KV_LORA), (SRC_KR, COL_KR, MLA_ROPE),
    (SRC_GQ, COL_GQ, W_GQK), (SRC_GK, COL_GK, W_GQK), (SRC_GV, COL_GV, W_GLA),
    (SRC_GR, COL_GR, W_GLA), (SRC_FQ, COL_FQ, W_FOX), (SRC_FK, COL_FK, W_FOX),
    (SRC_FV, COL_FV, W_FOX)], key=lambda p: p[1])
W_GATE_PLAN = [(SRC_GZ, 0, GLA_GATE_RANK), (SRC_FZ, GATE_FZ, FOX_HEADS)]
W_UQ_PLAN = [(h * (MLA_NOPE + MLA_ROPE), h * 2 * LANES, MLA_NOPE + MLA_ROPE) for h in range(MLA_HEADS)]


def _rmsnorm_body(x_ref, g_ref, o_ref):
    x = x_ref[...].astype(F32)
    y = x * lax.rsqrt(jnp.mean(x * x, axis=-1, keepdims=True) + RMS_EPS)
    o_ref[...] = (y * g_ref[...]).astype(o_ref.dtype)


def rmsnorm(x, gain, *, col0=0, width=None, out_dtype=BF16, rows=256):
    m = x.shape[0]
    width = x.shape[1] if width is None else width
    assert col0 % width == 0
    tr = _pick(m, rows, 16)
    return pl.pallas_call(
        _rmsnorm_body,
        grid=(m // tr,),
        in_specs=[pl.BlockSpec((tr, width), lambda i: (i, col0 // width)),
                  pl.BlockSpec((1, width), lambda i: (0, 0))],
        out_specs=pl.BlockSpec((tr, width), lambda i: (i, 0)),
        out_shape=jax.ShapeDtypeStruct((m, width), out_dtype),
        compiler_params=_params("parallel"),
        name="rmsnorm",
    )(x, gain.reshape(1, width).astype(F32))


def _final_norm_body(x_ref, g_ref, o_ref):
    x = x_ref[...]
    y = x * lax.rsqrt(jnp.mean(x * x, axis=-1, keepdims=True) + RMS_EPS)
    o_ref[0] = y * g_ref[...]


def final_rmsnorm(h, gain, batch, seq_total):
    d = h.shape[1]
    tr = LEAD
    nblk = seq_total // tr
    return pl.pallas_call(
        _final_norm_body,
        grid=(batch, nblk - 1),
        in_specs=[pl.BlockSpec((tr, d), lambda b, i: (b * nblk + i + 1, 0)),
                  pl.BlockSpec((1, d), lambda b, i: (0, 0))],
        out_specs=pl.BlockSpec((1, tr, d), lambda b, i: (b, i, 0)),
        out_shape=jax.ShapeDtypeStruct((batch, seq_total - LEAD, d), F32),
        compiler_params=_params("parallel", "parallel"),
        name="final_norm",
    )(h, gain.reshape(1, d).astype(F32))


def _embed_norm_body(x_ref, meta_ref, g_ref, h_ref, hn_ref):
    lead = pl.program_id(1) == 0

    def emit(v):
        h_ref[...] = v
        hn = v * lax.rsqrt(jnp.mean(v * v, axis=-1, keepdims=True) + RMS_EPS) * g_ref[...]
        hn_ref[...] = hn.astype(hn_ref.dtype)

    @pl.when(lead)
    def _():
        d = h_ref.shape[1]
        emit(jnp.concatenate([jnp.zeros((N_PAD, d), F32), meta_ref[...]], axis=0))

    @pl.when(jnp.logical_not(lead))
    def _():
        emit(x_ref[0])


def embed_norm(x, meta_tokens, gain):
    batch, seq, d = x.shape
    assert seq % LEAD == 0
    nblk = seq // LEAD + 1
    t = batch * nblk * LEAD
    row = lambda b, i: (b * nblk + i, 0)
    return pl.pallas_call(
        _embed_norm_body,
        grid=(batch, nblk),
        in_specs=[pl.BlockSpec((1, LEAD, d), lambda b, i: (b, jnp.maximum(i - 1, 0), 0)),
                  pl.BlockSpec((N_META, d), lambda b, i: (0, 0)),
                  pl.BlockSpec((1, d), lambda b, i: (0, 0))],
        out_specs=[pl.BlockSpec((LEAD, d), row), pl.BlockSpec((LEAD, d), row)],
        out_shape=[jax.ShapeDtypeStruct((t, d), F32), jax.ShapeDtypeStruct((t, d), BF16)],
        compiler_params=_params("parallel", "arbitrary"),
        name="embed_norm",
    )(x.astype(F32), meta_tokens.astype(F32), gain.reshape(1, d).astype(F32))


def _mm_body(a_ref, w_ref, o_ref):
    o_ref[...] = jnp.dot(a_ref[...], w_ref[...].astype(BF16),
                         preferred_element_type=F32).astype(o_ref.dtype)


def matmul(a, w, *, tm, tn, out_dtype=BF16, layer=None):
    m, kdim = a.shape
    n = w.shape[-1]
    assert m % tm == 0 and n % tn == 0 and w.shape[-2] == kdim
    return pl.pallas_call(
        _mm_body,
        grid=(m // tm, n // tn),
        in_specs=[pl.BlockSpec((tm, kdim), lambda i, j: (i, 0)),
                  _wspec(w, layer, (kdim, tn), lambda i, j: (0, j))],
        out_specs=pl.BlockSpec((tm, tn), lambda i, j: (i, j)),
        out_shape=jax.ShapeDtypeStruct((m, n), out_dtype),
        compiler_params=_params("parallel", "parallel"),
        name="matmul",
    )(a, w)


def _out_proj_body(a1_ref, a2_ref, a3_ref, w_ref, r_ref, o_ref, wb_ref, *, seq_total, tm):
    i = pl.program_id(1)

    @pl.when(i == 0)
    def _():
        wb_ref[...] = w_ref[...].astype(BF16)

    k1 = a1_ref.shape[1]
    k2 = k1 + a2_ref.shape[1]
    acc = jnp.dot(a1_ref[...], wb_ref[0:k1, :], preferred_element_type=F32)
    acc += jnp.dot(a2_ref[...], wb_ref[k1:k2, :], preferred_element_type=F32)
    acc += jnp.dot(a3_ref[...], wb_ref[k2:, :], preferred_element_type=F32)
    pos = _batch_row0(i, seq_total, tm) + lax.broadcasted_iota(jnp.int32, acc.shape, 0)
    o_ref[...] = r_ref[...] + jnp.where(pos >= N_PAD, acc, 0.0)


def out_proj(a1, a2, a3, w, res, *, seq_total, tm, tn, layer=None):
    m = a1.shape[0]
    kdim, n = w.shape[-2:]
    assert a1.shape[1] + a2.shape[1] + a3.shape[1] == kdim
    assert m % tm == 0 and n % tn == 0 and seq_total % tm == 0
    a_spec = lambda a: pl.BlockSpec((tm, a.shape[1]), lambda j, i: (i, 0))
    return pl.pallas_call(
        functools.partial(_out_proj_body, seq_total=seq_total, tm=tm),
        grid=(n // tn, m // tm),
        in_specs=[a_spec(a1), a_spec(a2), a_spec(a3),
                  _wspec(w, layer, (kdim, tn), lambda j, i: (0, j), pipeline_mode=pl.Buffered(1)),
                  pl.BlockSpec((tm, tn), lambda j, i: (i, j))],
        out_specs=pl.BlockSpec((tm, tn), lambda j, i: (i, j)),
        out_shape=jax.ShapeDtypeStruct((m, n), F32),
        scratch_shapes=[pltpu.VMEM((kdim, tn), BF16)],
        compiler_params=_params("parallel", "arbitrary"),
        name="out_proj",
    )(a1, a2, a3, w, res)


CONV_CHUNK = 32
UP_ROW_CHUNK = 128
UP_K_CHUNK = 1024


def _ffn_up_body(a_ref, wg_ref, wv_ref, cwg_ref, cwv_ref, cbg_ref, cbv_ref, o_ref,
                 wcat_ref, u0_ref, u1_ref, halo_ref, *, seq_total, tm, tc, nj, nsteps):
    s = pl.program_id(0)
    cur = jnp.minimum(s, nsteps - 1)
    prev = jnp.maximum(s - 1, 0)
    row0_cur = _batch_row0(cur // nj, seq_total, tm)
    row0_prev = _batch_row0(prev // nj, seq_total, tm)
    j_prev = prev % nj

    @pl.when(s == 0)
    def _():
        u1_ref[...] = jnp.zeros_like(u1_ref)
        halo_ref[...] = jnp.zeros_like(halo_ref)

    mc = _pick(tm, UP_ROW_CHUNK, CONV_CHUNK)
    kc = _pick(a_ref.shape[1], UP_K_CHUNK, LANES)
    n_dots = (tm // mc) * (a_ref.shape[1] // kc)
    gate_rows = list(range(0, tm, CONV_CHUNK))
    per_dot = -(-len(gate_rows) // n_dots)

    def step(uc, up):
        up[0:SUBLANES, :] = jnp.where(row0_prev > 0, halo_ref[j_prev], 0.0)
        halo_ref[j_prev] = up[tm:tm + SUBLANES, :]
        wcat_ref[:, 0:tc] = wg_ref[...].astype(BF16)
        wcat_ref[:, tc:2 * tc] = wv_ref[...].astype(BF16)
        cw = jnp.concatenate([cwg_ref[...], cwv_ref[...]], axis=1)
        cb = jnp.concatenate([cbg_ref[...], cbv_ref[...]], axis=1)

        def gate_stage(r0):
            cv = cb
            for tap in range(CONV_W):
                lo = SUBLANES + r0 - (CONV_W - 1 - tap)
                cv = cv + cw[tap:tap + 1, :] * up[lo:lo + CONV_CHUNK, :]
            o_ref[r0:r0 + CONV_CHUNK, :] = (_silu(cv[:, 0:tc]) * cv[:, tc:2 * tc]).astype(o_ref.dtype)

        todo = list(gate_rows)
        nk = a_ref.shape[1] // kc
        for m0 in range(0, tm, mc):
            rows = slice(SUBLANES + m0, SUBLANES + m0 + mc)
            for ki in range(nk):
                part = jnp.dot(a_ref[m0:m0 + mc, ki * kc:(ki + 1) * kc], wcat_ref[ki * kc:(ki + 1) * kc, :],
                               preferred_element_type=F32)
                if ki > 0:
                    part = uc[rows, :] + part
                if ki == nk - 1:
                    pos = row0_cur + m0 + lax.broadcasted_iota(jnp.int32, part.shape, 0)
                    part = jnp.where(pos >= N_PAD, part, 0.0)
                uc[rows, :] = part
                for r0 in todo[:per_dot]:
                    gate_stage(r0)
                todo = todo[per_dot:]
        assert not todo

    @pl.when(s % 2 == 0)
    def _():
        step(u0_ref, u1_ref)

    @pl.when(s % 2 == 1)
    def _():
        step(u1_ref, u0_ref)


def ffn_up(hn, w_up, conv_w, conv_b, *, seq_total, tm, tc, d_ff, layer=None):
    m, kdim = hn.shape
    assert m % tm == 0 and seq_total % tm == 0 and tm % CONV_CHUNK == 0 and d_ff % tc == 0
    nj = d_ff // tc
    nsteps = (m // tm) * nj
    cur = lambda s: jnp.minimum(s, nsteps - 1)
    prev = lambda s: jnp.maximum(s - 1, 0)
    wspec = lambda off: _wspec(w_up, layer, (kdim, tc), lambda s: (0, cur(s) % nj + off))
    cspec = lambda c, r, off: _wspec(c, layer, (r, tc), lambda s: (0, prev(s) % nj + off))
    conv_b = conv_b.reshape(conv_b.shape[:-1] + (1, conv_b.shape[-1]))
    return pl.pallas_call(
        functools.partial(_ffn_up_body, seq_total=seq_total, tm=tm, tc=tc, nj=nj, nsteps=nsteps),
        grid=(nsteps + 1,),
        in_specs=[pl.BlockSpec((tm, kdim), lambda s: (cur(s) // nj, 0), pipeline_mode=pl.Buffered(1)),
                  wspec(0), wspec(nj), cspec(conv_w, CONV_W, 0), cspec(conv_w, CONV_W, nj),
                  cspec(conv_b, 1, 0), cspec(conv_b, 1, nj)],
        out_specs=pl.BlockSpec((tm, tc), lambda s: (prev(s) // nj, prev(s) % nj)),
        out_shape=jax.ShapeDtypeStruct((m, d_ff), BF16),
        scratch_shapes=[pltpu.VMEM((kdim, 2 * tc), BF16),
                        pltpu.VMEM((tm + SUBLANES, 2 * tc), F32),
                        pltpu.VMEM((tm + SUBLANES, 2 * tc), F32),
                        pltpu.VMEM((nj, SUBLANES, 2 * tc), F32)],
        compiler_params=_params("arbitrary"),
        name="ffn_up",
    )(hn, w_up, w_up, conv_w, conv_w, conv_b, conv_b)


def _ffn_down_body(a_ref, w_ref, r_ref, o_ref, *, seq_total, tm, tk, d_ff, nk):
    k = pl.program_id(2)

    def part(a, w):
        return jnp.dot(a, w.astype(BF16), preferred_element_type=F32)

    @pl.when(k == 0)
    def _():
        o_ref[...] = part(a_ref[...], w_ref[...])

    @pl.when(jnp.logical_and(k > 0, k < nk - 1))
    def _():
        o_ref[...] += part(a_ref[...], w_ref[...])

    @pl.when(k == nk - 1)
    def _():
        acol = k * tk + lax.broadcasted_iota(jnp.int32, a_ref.shape, 1)
        a = jnp.where(acol < d_ff, a_ref[...], jnp.zeros(a_ref.shape, a_ref.dtype))
        wrow = k * tk + lax.broadcasted_iota(jnp.int32, w_ref.shape, 0)
        w = jnp.where(wrow < d_ff, w_ref[...], 0.0)
        pos = _batch_row0(pl.program_id(0), seq_total, tm) + lax.broadcasted_iota(
            jnp.int32, o_ref.shape, 0)
        o_ref[...] = r_ref[...] + jnp.where(pos >= N_PAD, o_ref[...] + part(a, w), 0.0)


def ffn_down(act, w, res, *, seq_total, tm, tn, tk, layer=None):
    m, d_ff = act.shape
    n = w.shape[-1]
    assert m % tm == 0 and n % tn == 0 and seq_total % tm == 0 and w.shape[-2] == d_ff
    nk = pl.cdiv(d_ff, tk)
    assert nk >= 2
    return pl.pallas_call(
        functools.partial(_ffn_down_body, seq_total=seq_total, tm=tm, tk=tk, d_ff=d_ff, nk=nk),
        grid=(m // tm, n // tn, nk),
        in_specs=[pl.BlockSpec((tm, tk), lambda i, j, k: (i, k)),
                  _wspec(w, layer, (tk, tn), lambda i, j, k: (k, j)),
                  pl.BlockSpec((tm, tn), lambda i, j, k: (i, j))],
        out_specs=pl.BlockSpec((tm, tn), lambda i, j, k: (i, j)),
        out_shape=jax.ShapeDtypeStruct((m, n), F32),
        compiler_params=_params("parallel", "parallel", "arbitrary"),
        name="ffn_down",
    )(act, w, res)


def _scores(k_rows, q_parts):
    out = []
    for q in q_parts:
        s = jnp.dot(k_rows, q, preferred_element_type=F32)
        out.append((s, jnp.max(s, axis=0, keepdims=True)))
    return tuple(out)


def _causal(scored, col0s):
    out = []
    for (s, _), c0 in zip(scored, col0s):
        key = lax.broadcasted_iota(jnp.int32, s.shape, 0)
        qry = c0 + lax.broadcasted_iota(jnp.int32, s.shape, 1)
        s = jnp.where(key <= qry, s, NEG_INF)
        out.append((s, jnp.max(s, axis=0, keepdims=True)))
    return tuple(out)


def _softmax_pv(carry, scored, v_ext):
    out = []
    for (m, acc), (s, s_max) in zip(carry, scored):
        m_new = jnp.maximum(m, s_max)
        alpha = jnp.exp2(m - m_new)
        p = jnp.exp2(s - m_new).astype(BF16)
        out.append((m_new, alpha * acc + jnp.dot(v_ext, p, preferred_element_type=F32)))
    return tuple(out)


def _attn_init(widths):
    return tuple((jnp.full((1, w), NEG_INF, F32), jnp.zeros((V_ROWS, w), F32)) for w in widths)


def _attn_finish(carry, gain, o_ref, row0, col0s):
    for (_, acc), c0 in zip(carry, col0s):
        o = jnp.transpose(acc[0:LANES, :] / acc[LANES:LANES + 1, :])
        o = o * lax.rsqrt(jnp.mean(o * o, axis=-1, keepdims=True) + RMS_EPS) * gain
        o_ref[pl.ds(row0 + c0, o.shape[0]), :] = o.astype(o_ref.dtype)


def _query_parts(tb):
    part = QUERY_PART if tb % QUERY_PART == 0 else tb
    return part, tuple(range(0, tb, part))


def _attention_core(qt, qt0, ks, vt, vt0, s_scr, m_scr, g_ref, o_ref, *, nblk, tb):
    heads = range(qt.shape[0])
    gains = [g_ref[:, h * LANES:(h + 1) * LANES] for h in heads]
    outs = [o_ref.at[:, pl.ds(h * LANES, LANES)] for h in heads]

    for h in heads:
        lead = _causal(_scores(ks[h, 0:LEAD, :], (qt0[h],)), (0,))
        _attn_finish(_softmax_pv(_attn_init((LEAD,)), lead, vt0[h]), gains[h], outs[h], 0, (0,))

    part, col0s = _query_parts(tb)
    widths = (part,) * len(col0s)

    def q_block(qi, _):
        q_parts = [tuple(qt[h, qi, :, c0:c0 + part] for c0 in col0s) for h in heads]

        def score(slot, kj):
            for h in heads:
                k_rows = ks[h, pl.ds(pl.multiple_of(LEAD + kj * tb, LANES), tb), :]
                for p, (s, s_max) in enumerate(_scores(k_rows, q_parts[h])):
                    s_scr[h, slot, p] = s
                    m_scr[h, slot, p] = s_max

        def scored(h, slot):
            return tuple((s_scr[h, slot, p], m_scr[h, slot, p]) for p in range(len(col0s)))

        def consume(carry, slot, kj, causal):
            out = []
            for h in heads:
                sc = scored(h, slot)
                if causal:
                    sc = _causal(sc, col0s)
                out.append(_softmax_pv(carry[h], sc, vt[h, kj]))
            return tuple(out)

        carry = tuple(_softmax_pv(_attn_init(widths), _scores(ks[h, 0:LEAD, :], q_parts[h]), vt0[h])
                      for h in heads)
        score(0, 0)

        def pair(t, carry):
            kj = 2 * t
            score(1, kj + 1)
            carry = consume(carry, 0, kj, False)
            score(0, kj + 2)
            return consume(carry, 1, kj + 1, False)

        carry = lax.fori_loop(0, qi // 2, pair, carry)

        def odd_tail(carry):
            score(1, qi)
            return consume(consume(carry, 0, qi - 1, False), 1, qi, True)

        def even_tail(carry):
            return consume(carry, 0, qi, True)

        carry = lax.cond(qi % 2 == 1, odd_tail, even_tail, carry)
        for h in heads:
            _attn_finish(carry[h], gains[h], outs[h], pl.multiple_of(LEAD + qi * tb, LANES), col0s)
        return 0

    lax.fori_loop(0, nblk, q_block, 0)


def _store_v_ext(vt_dst, v_rows):
    n = v_rows.shape[0]
    vt_dst[0:LANES, :] = _to_t(v_rows)
    r = lax.broadcasted_iota(jnp.int32, (V_ROWS - LANES, n), 0)
    vt_dst[LANES:V_ROWS, :] = jnp.where(r == 0, 1.0, 0.0).astype(BF16)


def _to_t(x):
    return jnp.transpose(x.astype(F32)).astype(BF16)


def _mla_attn_body(q_ref, kv_ref, kr_ref, cos_ref, sa_ref, sb_ref, g_ref, o_ref,
                   qt, qt0, ks, vt, vt0, s_scr, m_scr, *, nblk, tb):
    scale = (MLA_NOPE + MLA_ROPE) ** -0.5 * LOG2E
    heads = range(qt.shape[0])

    def rope(x, rows):
        return (x * cos_ref[rows, :] + pltpu.roll(x, HALF_ROPE, 1) * sa_ref[rows, :]
                + pltpu.roll(x, LANES - HALF_ROPE, 1) * sb_ref[rows, :])

    def prep(r0, n, qt_dst, vt_dst):
        rows = pl.ds(r0, n)
        one = lax.broadcasted_iota(jnp.int32, (n, LANES), 1) == ONE_LANE
        pos = r0 + lax.broadcasted_iota(jnp.int32, (n, LANES), 0)
        kp = rope(kr_ref[rows, :].astype(F32), rows)
        kp = jnp.where(one, jnp.where(pos < N_PAD, NEG_INF, 0.0), kp).astype(BF16)
        for h in heads:
            c0 = h * 2 * LANES
            qn = q_ref[rows, pl.ds(c0, LANES)].astype(F32) * scale
            qp = rope(q_ref[rows, pl.ds(c0 + LANES, LANES)].astype(F32), rows) * scale
            qp = jnp.where(one, 1.0, qp)
            qt_dst(h)[0:LANES, :] = jnp.transpose(qn).astype(BF16)
            qt_dst(h)[LANES:2 * LANES, :] = jnp.transpose(qp).astype(BF16)
            ks[h, rows, pl.ds(0, LANES)] = kv_ref[rows, pl.ds(c0, LANES)]
            ks[h, rows, pl.ds(LANES, LANES)] = kp
            _store_v_ext(vt_dst(h), kv_ref[rows, pl.ds(c0 + LANES, LANES)])

    prep(0, LEAD, lambda h: qt0.at[h], lambda h: vt0.at[h])

    def prep_block(i, _):
        prep(pl.multiple_of(LEAD + i * tb, LANES), tb, lambda h: qt.at[h, i], lambda h: vt.at[h, i])
        return 0

    lax.fori_loop(0, nblk, prep_block, 0)
    _attention_core(qt, qt0, ks, vt, vt0, s_scr, m_scr, g_ref, o_ref, nblk=nblk, tb=tb)


def _attn_scratch(heads, seq_total, nblk, tb):
    part, col0s = _query_parts(tb)
    return [pltpu.VMEM((heads, nblk, 2 * LANES, tb), BF16), pltpu.VMEM((heads, 2 * LANES, LEAD), BF16),
            pltpu.VMEM((heads, seq_total, 2 * LANES), BF16),
            pltpu.VMEM((heads, nblk, V_ROWS, tb), BF16), pltpu.VMEM((heads, V_ROWS, LEAD), BF16),
            pltpu.VMEM((heads, 2, len(col0s), tb, part), F32),
            pltpu.VMEM((heads, 2, len(col0s), 1, part), F32)]


def mla_attention(q_raw, kv, proj, cos_t, sin_a, sin_b, gain, *, batch, seq_total, tb):
    t = q_raw.shape[0]
    nblk = (seq_total - LEAD) // tb
    hps = ATTN_HEADS_PER_STEP
    assert MLA_HEADS % hps == 0
    big = lambda w: pl.BlockSpec((seq_total, hps * w), lambda b, h: (b, h))
    table = pl.BlockSpec((seq_total, LANES), lambda b, h: (0, 0), pipeline_mode=pl.Buffered(1))
    return pl.pallas_call(
        functools.partial(_mla_attn_body, nblk=nblk, tb=tb),
        grid=(batch, MLA_HEADS // hps),
        in_specs=[big(2 * LANES), big(2 * LANES),
                  pl.BlockSpec((seq_total, LANES), lambda b, h: (b, COL_KR // LANES)),
                  table, table, table,
                  pl.BlockSpec((1, hps * LANES), lambda b, h: (0, h))],
        out_specs=big(LANES),
        out_shape=jax.ShapeDtypeStruct((t, W_MLA), BF16),
        scratch_shapes=_attn_scratch(hps, seq_total, nblk, tb),
        compiler_params=_params("parallel", "parallel"),
        name="mla_attention",
    )(q_raw, kv, proj, cos_t, sin_a, sin_b, gain.reshape(1, W_MLA).astype(F32))


def _fox_attn_body(q_ref, k_ref, v_ref, aq_ref, ak_ref, g_ref, o_ref,
                   qt, qt0, ks, vt, vt0, s_scr, m_scr, *, nblk, tb):
    scale = FOX_DH ** -0.5 * LOG2E
    heads = range(qt.shape[0])

    def prep(r0, n, qt_dst, vt_dst):
        rows = pl.ds(r0, n)
        for h in heads:
            cols = pl.ds(h * LANES, LANES)
            qt_dst(h)[0:LANES, :] = jnp.transpose(q_ref[rows, cols].astype(F32) * scale).astype(BF16)
            qt_dst(h)[LANES:2 * LANES, :] = _to_t(aq_ref[rows, cols])
            ks[h, rows, pl.ds(0, LANES)] = k_ref[rows, cols]
            ks[h, rows, pl.ds(LANES, LANES)] = ak_ref[rows, cols]
            _store_v_ext(vt_dst(h), v_ref[rows, cols])

    prep(0, LEAD, lambda h: qt0.at[h], lambda h: vt0.at[h])

    def prep_block(i, _):
        prep(pl.multiple_of(LEAD + i * tb, LANES), tb, lambda h: qt.at[h, i], lambda h: vt.at[h, i])
        return 0

    lax.fori_loop(0, nblk, prep_block, 0)
    _attention_core(qt, qt0, ks, vt, vt0, s_scr, m_scr, g_ref, o_ref, nblk=nblk, tb=tb)


def fox_attention(proj, aug_q, aug_k, gain, *, batch, seq_total, tb):
    t = proj.shape[0]
    nblk = (seq_total - LEAD) // tb
    hps = ATTN_HEADS_PER_STEP
    assert FOX_HEADS % hps == 0
    wb = hps * LANES
    col = lambda c0: pl.BlockSpec((seq_total, wb), lambda b, h: (b, c0 // wb + h))
    assert COL_FQ % wb == 0 and COL_FK % wb == 0 and COL_FV % wb == 0
    return pl.pallas_call(
        functools.partial(_fox_attn_body, nblk=nblk, tb=tb),
        grid=(batch, FOX_HEADS // hps),
        in_specs=[col(COL_FQ), col(COL_FK), col(COL_FV), col(0), col(0),
                  pl.BlockSpec((1, wb), lambda b, h: (0, h))],
        out_specs=col(0),
        out_shape=jax.ShapeDtypeStruct((t, W_FOX), BF16),
        scratch_shapes=_attn_scratch(hps, seq_total, nblk, tb),
        compiler_params=_params("parallel", "parallel"),
        name="fox_attention",
    )(proj, proj, proj, aug_q, aug_k, gain.reshape(1, W_FOX).astype(F32))


def _split3(x):
    hi = x.astype(BF16)
    r1 = x - hi.astype(F32)
    mid = r1.astype(BF16)
    lo = (r1 - mid.astype(F32)).astype(BF16)
    return hi.astype(F32), mid.astype(F32), lo.astype(F32)


def _fox_prep_body(z_ref, b_ref, aq_ref, ak_ref, carry_ref, *, tr):
    i = pl.program_id(1)

    @pl.when(i == 0)
    def _():
        carry_ref[...] = jnp.zeros_like(carry_ref)

    row = i * tr + lax.broadcasted_iota(jnp.int32, (tr, LANES), 0)
    valid = row >= N_PAD
    log_f = jnp.where(valid, _log_sigmoid(z_ref[...] + b_ref[...]), 0.0)
    r = lax.broadcasted_iota(jnp.int32, (tr, tr), 0)
    c = lax.broadcasted_iota(jnp.int32, (tr, tr), 1)
    tri = jnp.where(c <= r, 1.0, 0.0).astype(F32)
    csum = jnp.dot(tri, log_f, preferred_element_type=F32,
                   precision=lax.Precision.HIGHEST) + carry_ref[0:1, :]
    carry_ref[...] = jnp.broadcast_to(csum[tr - 1:tr, :], carry_ref.shape)

    lane = lax.broadcasted_iota(jnp.int32, (tr, LANES), 1)
    key_mask = jnp.where(valid, 0.0, NEG_INF)
    for h in range(FOX_HEADS):
        col = csum[:, GATE_FZ + h:GATE_FZ + h + 1] * LOG2E
        hi, mid, lo = _split3(col)
        aq = jnp.where(lane == 0, hi, jnp.where(lane == 1, mid, jnp.where(lane == 2, lo,
             jnp.where(lane < 7, 1.0, 0.0))))
        ak = jnp.where(lane < 3, 1.0, jnp.where(lane == 3, -hi, jnp.where(lane == 4, -mid,
             jnp.where(lane == 5, -lo, jnp.where(lane == 6, key_mask, 0.0)))))
        aq_ref[:, h * LANES:(h + 1) * LANES] = aq.astype(BF16)
        ak_ref[:, h * LANES:(h + 1) * LANES] = ak.astype(BF16)


def fox_prep(gates, fox_b, *, batch, seq_total):
    t = gates.shape[0]
    tr = _pick(seq_total, 384, LANES)
    nblk = seq_total // tr
    bias = jnp.zeros((1, LANES), F32).at[0, GATE_FZ:GATE_FZ + FOX_HEADS].set(fox_b.astype(F32))
    out = jax.ShapeDtypeStruct((t, W_FOX), BF16)
    return pl.pallas_call(
        functools.partial(_fox_prep_body, tr=tr),
        grid=(batch, nblk),
        in_specs=[pl.BlockSpec((tr, LANES), lambda b, i: (b * nblk + i, 0)),
                  pl.BlockSpec((1, LANES), lambda b, i: (0, 0))],
        out_specs=[pl.BlockSpec((tr, W_FOX), lambda b, i: (b * nblk + i, 0))] * 2,
        out_shape=[out, out],
        scratch_shapes=[pltpu.VMEM((SUBLANES, LANES), F32)],
        compiler_params=_params("parallel", "arbitrary"),
        name="fox_prep",
    )(gates, bias)


def _gla_body(q_ref, k_ref, v_ref, r_ref, z_ref, w2_ref, b2_ref, g_ref, o_ref, s_ref,
              *, seq_total):
    c = GLA_CHUNK
    nchunk = seq_total // c
    gsz = max(g for g in range(1, GLA_GROUP + 1) if nchunk % g == 0)
    rg = gsz * c
    heads = s_ref.shape[0]
    nt = (((1,), (1,)), ((), ()))
    tn = (((0,), (0,)), ((), ()))
    r_i = lax.broadcasted_iota(jnp.int32, (rg, rg), 0)
    c_i = lax.broadcasted_iota(jnp.int32, (rg, rg), 1)
    assert c & (c - 1) == 0
    lower = jnp.logical_and(c_i <= r_i, c_i >= jnp.bitwise_and(r_i, -c))
    tri = jnp.where(lower, 1.0, 0.0).astype(F32)
    s_ref[...] = jnp.zeros_like(s_ref)

    def group(i, _):
        r0 = pl.multiple_of(i * rg, rg)
        rows = pl.ds(r0, rg)
        valid = (r0 + lax.broadcasted_iota(jnp.int32, (rg, GLA_DK), 0)) >= N_PAD
        z = z_ref[rows, :].astype(BF16)
        for hh in range(heads):
            kcols = pl.ds(hh * GLA_DK, GLA_DK)
            vcols = pl.ds(hh * GLA_DV, GLA_DV)
            q = q_ref[rows, kcols].astype(F32) * (GLA_DK ** -0.5)
            k = jnp.where(valid, k_ref[rows, kcols].astype(F32), 0.0)
            v = v_ref[rows, vcols]
            logit = jnp.dot(z, w2_ref[:, kcols], preferred_element_type=F32) + b2_ref[:, kcols]
            log_a = jnp.where(valid, _log_sigmoid(logit) / GLA_TAU, 0.0)
            bc = jnp.dot(tri, log_a, preferred_element_type=F32, precision=lax.Precision.HIGHEST)
            b_last = jnp.concatenate(
                [jnp.broadcast_to(bc[(n + 1) * c - 1:(n + 1) * c, :], (c, GLA_DK)) for n in range(gsz)], axis=0)
            q_dec = (q * jnp.exp(bc)).astype(BF16)
            k_dec = (k * jnp.exp(-bc)).astype(BF16)
            k_state = (k * jnp.exp(b_last - bc)).astype(BF16)
            a = lax.dot_general(q_dec, k_dec, nt, preferred_element_type=F32)
            a = jnp.where(lower, a, 0.0).astype(BF16)
            o_intra = jnp.dot(a, v, preferred_element_type=F32)
            state = s_ref[hh]
            outs = []
            for n in range(gsz):
                sl = slice(n * c, (n + 1) * c)
                outs.append(o_intra[sl] + jnp.dot(q_dec[sl], state.astype(BF16),
                                                   preferred_element_type=F32))
                decay = jnp.exp(b_last[n * c:n * c + 1, :])
                dec_t = jnp.transpose(jnp.broadcast_to(decay, (GLA_DK, GLA_DK)))
                dec_t = jnp.concatenate([dec_t, dec_t], axis=1)
                state = state * dec_t + lax.dot_general(k_state[sl], v[sl], tn,
                                                        preferred_element_type=F32)
            s_ref[hh] = state
            o = jnp.concatenate(outs, axis=0)
            o = o * lax.rsqrt(jnp.mean(o * o, axis=-1, keepdims=True) + RMS_EPS) * g_ref[:, vcols]
            o_ref[rows, vcols] = (o * _silu(r_ref[rows, vcols].astype(F32))).astype(o_ref.dtype)
        return 0

    lax.fori_loop(0, nchunk // gsz, group, 0)


def gla(proj, gates, w2, b2, gain, *, batch, seq_total):
    t = proj.shape[0]
    hps = GLA_HEADS_PER_STEP
    assert GLA_HEADS % hps == 0
    wk, wv = hps * GLA_DK, hps * GLA_DV
    blk = lambda w, c0: pl.BlockSpec((seq_total, w), lambda b, h: (b, c0 // w + h))
    return pl.pallas_call(
        functools.partial(_gla_body, seq_total=seq_total),
        grid=(batch, GLA_HEADS // hps),
        in_specs=[blk(wk, COL_GQ), blk(wk, COL_GK), blk(wv, COL_GV), blk(wv, COL_GR),
                  pl.BlockSpec((seq_total, LANES), lambda b, h: (b, 0)),
                  pl.BlockSpec((LANES, wk), lambda b, h: (0, h)),
                  pl.BlockSpec((1, wk), lambda b, h: (0, h)),
                  pl.BlockSpec((1, wv), lambda b, h: (0, h))],
        out_specs=pl.BlockSpec((seq_total, wv), lambda b, h: (b, h)),
        out_shape=jax.ShapeDtypeStruct((t, W_GLA), BF16),
        scratch_shapes=[pltpu.VMEM((hps, GLA_DK, GLA_DV), F32)],
        compiler_params=_params("parallel", "parallel"),
        name="gla",
    )(proj, proj, proj, proj, gates, w2, b2, gain.reshape(1, W_GLA).astype(F32))


def _rope_tables(seq_total):
    pos = np.maximum(np.arange(seq_total) - N_PAD, 0).astype(np.float32)
    inv_freq = (1.0 / (ROPE_THETA ** (np.arange(0, MLA_ROPE, 2, dtype=np.float32) / MLA_ROPE))).astype(np.float32)
    ang = jnp.asarray(pos)[:, None] * jnp.asarray(inv_freq)[None, :]
    cos, sin = jnp.cos(ang), jnp.sin(ang)
    z = jnp.zeros_like(cos)
    cos_t = jnp.concatenate([cos, cos, z, z], axis=1)
    sin_a = jnp.concatenate([z, sin, z, z], axis=1)
    sin_b = jnp.concatenate([-sin, z, z, z], axis=1)
    return cos_t, sin_a, sin_b


def kernel(x, meta_tokens, attn_norm, w_in, mla_q_norm, mla_w_uq, mla_kv_norm, mla_w_ukv,
           gla_w_gate2, gla_b_gate, fox_b_f, out_norm_mla, out_norm_gla, out_norm_fox,
           w_out, ffn_norm, ffn_w_up, ffn_conv_w, ffn_conv_b, ffn_w_down, final_norm):
    batch, seq, d_model = x.shape
    depth = w_in.shape[0]
    d_ff = ffn_w_down.shape[1]
    seq_total = LEAD + seq
    cos_t, sin_a, sin_b = _rope_tables(seq_total)

    tm_big = _pick(seq_total, TM_BIG, LANES)
    tm_mid = _pick(seq_total, TM_MID, LANES)
    tb = _pick(seq, ATTN_BLOCK, LANES)
    tc = _pick(d_ff, 256, LANES)

    w_in_b = transpose_cast(jnp.transpose(w_in, (2, 0, 1)))

    h, hn = embed_norm(x, meta_tokens, attn_norm[0])
    for layer in range(depth):
        w_big, w_gates = relayout_cols(w_in_b, (W_IN_PLAN, W_GATE_PLAN), (N_PROJ, LANES), layer=layer)
        w_uq, = relayout_cols(mla_w_uq, (W_UQ_PLAN,), (MLA_HEADS * 2 * LANES,), layer=layer)
        if layer > 0:
            hn = rmsnorm(h, attn_norm[layer])
        proj = matmul(hn, w_big, tm=tm_big, tn=_pick(N_PROJ, 896, LANES))
        gates = matmul(hn, w_gates, tm=tm_big, tn=LANES, out_dtype=F32)

        cq = rmsnorm(proj, mla_q_norm[layer], col0=COL_CQ, width=MLA_Q_LORA)
        ckv = rmsnorm(proj, mla_kv_norm[layer], col0=COL_CKV, width=MLA_KV_LORA)
        q_raw = matmul(cq, w_uq, tm=tm_big, tn=1024)
        kv = matmul(ckv, mla_w_ukv, tm=tm_big, tn=1024, layer=layer)
        o_mla = mla_attention(q_raw, kv, proj, cos_t, sin_a, sin_b, out_norm_mla[layer],
                              batch=batch, seq_total=seq_total, tb=tb)

        w2 = jnp.zeros((LANES, W_GQK), F32).at[:GLA_GATE_RANK].set(gla_w_gate2[layer])
        o_gla = gla(proj, gates, w2.astype(BF16), gla_b_gate[layer].reshape(1, -1),
                    out_norm_gla[layer], batch=batch, seq_total=seq_total)

        aug_q, aug_k = fox_prep(gates, fox_b_f[layer], batch=batch, seq_total=seq_total)
        o_fox = fox_attention(proj, aug_q, aug_k, out_norm_fox[layer],
                              batch=batch, seq_total=seq_total, tb=tb)

        h = out_proj(o_mla, o_gla, o_fox, w_out, h, seq_total=seq_total, tm=tm_mid, tn=1024,
                     layer=layer)

        hn = rmsnorm(h, ffn_norm[layer])
        act = ffn_up(hn, ffn_w_up, ffn_conv_w, ffn_conv_b, seq_total=seq_total, tm=tm_big,
                     tc=tc, d_ff=d_ff, layer=layer)
        h = ffn_down(act, ffn_w_down, h, seq_total=seq_total, tm=tm_big, tn=1024, tk=DOWN_TK,
                     layer=layer)

    return final_rmsnorm(h, final_norm, batch, seq_total)
```

```python
import functools

import jax
import jax.numpy as jnp
import numpy as np
from jax import lax
from jax.experimental import pallas as pl
from jax.experimental.pallas import tpu as pltpu

F32 = jnp.float32
BF16 = jnp.bfloat16

N_META = 16
LEAD = 128
N_PAD = LEAD - N_META
RMS_EPS = 1e-6
NEG_INF = -1e30

MLA_V = 128
MLA_HEADS = 12
MLA_NOPE = 128
MLA_ROPE = 64
MLA_Q_LORA = 1536
MLA_KV_LORA = 512
ROPE_THETA = 10000.0

GLA_DV = 256
GLA_DK = 128
GLA_HEADS = 4
GLA_GATE_RANK = 16
GLA_TAU = 16.0
GLA_CHUNK = 64

FOX_DH = 128
FOX_HEADS = 12

W_MLA = MLA_HEADS * MLA_V
W_GLA = GLA_HEADS * GLA_DV
W_FOX = FOX_HEADS * FOX_DH
CONV_W = 3

LANES = 128
SUBLANES = 8
VMEM_LIMIT_BYTES = 56 * 1024 * 1024

TM_BIG = 1408
TM_MID = 704
ATTN_BLOCK = 512
DOWN_TK = 1024
GLA_HEADS_PER_STEP = 2
GLA_GROUP = 6
QUERY_PART = 256
ATTN_HEADS_PER_STEP = 2

HALF_ROPE = MLA_ROPE // 2
W_GQK = GLA_HEADS * GLA_DK
COL_CQ = 0
COL_CKV = COL_CQ + MLA_Q_LORA
COL_GV = COL_CKV + MLA_KV_LORA
COL_GR = COL_GV + W_GLA
COL_GQ = COL_GR + W_GLA
COL_GK = COL_GQ + W_GQK
COL_FQ = COL_GK + W_GQK
COL_FK = COL_FQ + W_FOX
COL_FV = COL_FK + W_FOX
COL_KR = COL_FV + W_FOX
MXU_COLS = 256
PROJ_TN = 3 * MXU_COLS
N_PROJ = -(-(COL_KR + LANES) // PROJ_TN) * PROJ_TN
SRC_CQ = 0
SRC_CKV = SRC_CQ + MLA_Q_LORA
SRC_KR = SRC_CKV + MLA_KV_LORA
SRC_GQ = SRC_KR + MLA_ROPE
SRC_GK = SRC_GQ + W_GQK
SRC_GV = SRC_GK + W_GQK
SRC_GZ = SRC_GV + W_GLA
SRC_GR = SRC_GZ + GLA_GATE_RANK
SRC_FQ = SRC_GR + W_GLA
SRC_FK = SRC_FQ + W_FOX
SRC_FV = SRC_FK + W_FOX
SRC_FZ = SRC_FV + W_FOX
GATE_FZ = GLA_GATE_RANK
ONE_LANE = MLA_ROPE
LOG2E = 1.4426950408889634
V_ROWS = LANES + 16


def _pick(n, target, mult):
    best = None
    for d in range(mult, min(n, target) + 1, mult):
        if n % d == 0:
            best = d
    assert best is not None, (n, target, mult)
    return best


def _params(*sem):
    return pltpu.CompilerParams(dimension_semantics=sem, vmem_limit_bytes=VMEM_LIMIT_BYTES)


def _log_sigmoid(x):
    return jnp.minimum(x, 0.0) - jnp.log1p(jnp.exp(-jnp.abs(x)))


def _silu(x):
    return x / (1.0 + jnp.exp(-x))


def _wspec(w, layer, block, index, **kw):
    if layer is None:
        assert w.ndim == len(block)
        return pl.BlockSpec(block, index, **kw)
    assert w.ndim == len(block) + 1
    return pl.BlockSpec((None,) + tuple(block), lambda *g: (layer,) + tuple(index(*g)), **kw)


def _batch_row0(step, seq_total, tm):
    return (step % (seq_total // tm)) * tm


def _relayout_body(x_ref, *o_refs, plans):
    for o_ref, plan in zip(o_refs, plans):
        width = o_ref.shape[1]
        pos = 0
        for src, dst, w in plan:
            assert dst >= pos
            if dst > pos:
                o_ref[:, pos:dst] = jnp.zeros((o_ref.shape[0], dst - pos), o_ref.dtype)
            o_ref[:, dst:dst + w] = x_ref[:, src:src + w].astype(o_ref.dtype)
            pos = dst + w
        if pos < width:
            o_ref[:, pos:width] = jnp.zeros((o_ref.shape[0], width - pos), o_ref.dtype)


def relayout_cols(w, plans, widths, *, layer=None, rows=256):
    k, n = w.shape[-2:]
    tr = _pick(k, rows, 16)
    return pl.pallas_call(
        functools.partial(_relayout_body, plans=plans),
        grid=(k // tr,),
        in_specs=[_wspec(w, layer, (tr, n), lambda i: (i, 0))],
        out_specs=[pl.BlockSpec((tr, wd), lambda i: (i, 0)) for wd in widths],
        out_shape=[jax.ShapeDtypeStruct((k, wd), BF16) for wd in widths],
        compiler_params=_params("parallel"),
        name="relayout_cols",
    )(w)


def _transpose_cast_body(x_ref, o_ref, *, n):
    row = pl.program_id(0) * LANES + lax.broadcasted_iota(jnp.int32, (LANES, x_ref.shape[2]), 0)
    for layer in range(x_ref.shape[1]):
        x = jnp.where(row < n, x_ref[:, layer, :], 0.0)
        o_ref[layer] = jnp.transpose(x).astype(o_ref.dtype)


def transpose_cast(wt):
    n, depth, k = wt.shape
    nb = pl.cdiv(n, LANES)
    return pl.pallas_call(
        functools.partial(_transpose_cast_body, n=n),
        grid=(nb,),
        in_specs=[pl.BlockSpec((LANES, depth, k), lambda i: (i, 0, 0))],
        out_specs=pl.BlockSpec((depth, k, LANES), lambda i: (0, 0, i)),
        out_shape=jax.ShapeDtypeStruct((depth, k, nb * LANES), BF16),
        compiler_params=_params("parallel"),
        name="transpose_cast",
    )(wt)


W_IN_PLAN = sorted([
    (SRC_CQ, COL_CQ, MLA_Q_LORA), (SRC_CKV, COL_CKV, MLA_KV_LORA), (SRC_KR, COL_KR, MLA_ROPE),
    (SRC_GQ, COL_GQ, W_GQK), (SRC_GK, COL_GK, W_GQK), (SRC_GV, COL_GV, W_GLA),
    (SRC_GR, COL_GR, W_GLA), (SRC_FQ, COL_FQ, W_FOX), (SRC_FK, COL_FK, W_FOX),
    (SRC_FV, COL_FV, W_FOX)], key=lambda p: p[1])
W_GATE_PLAN = [(SRC_GZ, 0, GLA_GATE_RANK), (SRC_FZ, GATE_FZ, FOX_HEADS)]
W_UQ_PLAN = [(h * (MLA_NOPE + MLA_ROPE), h * 2 * LANES, MLA_NOPE + MLA_ROPE) for h in range(MLA_HEADS)]


def _rmsnorm_body(x_ref, g_ref, o_ref):
    x = x_ref[...].astype(F32)
    y = x * lax.rsqrt(jnp.mean(x * x, axis=-1, keepdims=True) + RMS_EPS)
    o_ref[...] = (y * g_ref[...]).astype(o_ref.dtype)


def rmsnorm(x, gain, *, col0=0, width=None, out_dtype=BF16, rows=256):
    m = x.shape[0]
    width = x.shape[1] if width is None else width
    assert col0 % width == 0
    tr = _pick(m, rows, 16)
    return pl.pallas_call(
        _rmsnorm_body,
        grid=(m // tr,),
        in_specs=[pl.BlockSpec((tr, width), lambda i: (i, col0 // width)),
                  pl.BlockSpec((1, width), lambda i: (0, 0))],
        out_specs=pl.BlockSpec((tr, width), lambda i: (i, 0)),
        out_shape=jax.ShapeDtypeStruct((m, width), out_dtype),
        compiler_params=_params("parallel"),
        name="rmsnorm",
    )(x, gain.reshape(1, width).astype(F32))


def _final_norm_body(x_ref, g_ref, o_ref):
    x = x_ref[...]
    y = x * lax.rsqrt(jnp.mean(x * x, axis=-1, keepdims=True) + RMS_EPS)
    o_ref[0] = y * g_ref[...]


def final_rmsnorm(h, gain, batch, seq_total):
    d = h.shape[1]
    tr = LEAD
    nblk = seq_total // tr
    return pl.pallas_call(
        _final_norm_body,
        grid=(batch, nblk - 1),
        in_specs=[pl.BlockSpec((tr, d), lambda b, i: (b * nblk + i + 1, 0)),
                  pl.BlockSpec((1, d), lambda b, i: (0, 0))],
        out_specs=pl.BlockSpec((1, tr, d), lambda b, i: (b, i, 0)),
        out_shape=jax.ShapeDtypeStruct((batch, seq_total - LEAD, d), F32),
        compiler_params=_params("parallel", "parallel"),
        name="final_norm",
    )(h, gain.reshape(1, d).astype(F32))


def _embed_norm_body(x_ref, meta_ref, g_ref, h_ref, hn_ref):
    lead = pl.program_id(1) == 0

    def emit(v):
        h_ref[...] = v
        hn = v * lax.rsqrt(jnp.mean(v * v, axis=-1, keepdims=True) + RMS_EPS) * g_ref[...]
        hn_ref[...] = hn.astype(hn_ref.dtype)

    @pl.when(lead)
    def _():
        d = h_ref.shape[1]
        emit(jnp.concatenate([jnp.zeros((N_PAD, d), F32), meta_ref[...]], axis=0))

    @pl.when(jnp.logical_not(lead))
    def _():
        emit(x_ref[0])


def embed_norm(x, meta_tokens, gain):
    batch, seq, d = x.shape
    assert seq % LEAD == 0
    nblk = seq // LEAD + 1
    t = batch * nblk * LEAD
    row = lambda b, i: (b * nblk + i, 0)
    return pl.pallas_call(
        _embed_norm_body,
        grid=(batch, nblk),
        in_specs=[pl.BlockSpec((1, LEAD, d), lambda b, i: (b, jnp.maximum(i - 1, 0), 0)),
                  pl.BlockSpec((N_META, d), lambda b, i: (0, 0)),
                  pl.BlockSpec((1, d), lambda b, i: (0, 0))],
        out_specs=[pl.BlockSpec((LEAD, d), row), pl.BlockSpec((LEAD, d), row)],
        out_shape=[jax.ShapeDtypeStruct((t, d), F32), jax.ShapeDtypeStruct((t, d), BF16)],
        compiler_params=_params("parallel", "arbitrary"),
        name="embed_norm",
    )(x.astype(F32), meta_tokens.astype(F32), gain.reshape(1, d).astype(F32))


def _mm_body(a_ref, w_ref, o_ref):
    o_ref[...] = jnp.dot(a_ref[...], w_ref[...].astype(BF16),
                         preferred_element_type=F32).astype(o_ref.dtype)


def matmul(a, w, *, tm, tn, out_dtype=BF16, layer=None):
    m, kdim = a.shape
    n = w.shape[-1]
    assert m % tm == 0 and n % tn == 0 and w.shape[-2] == kdim
    return pl.pallas_call(
        _mm_body,
        grid=(m // tm, n // tn),
        in_specs=[pl.BlockSpec((tm, kdim), lambda i, j: (i, 0)),
                  _wspec(w, layer, (kdim, tn), lambda i, j: (0, j))],
        out_specs=pl.BlockSpec((tm, tn), lambda i, j: (i, j)),
        out_shape=jax.ShapeDtypeStruct((m, n), out_dtype),
        compiler_params=_params("parallel", "parallel"),
        name="matmul",
    )(a, w)


def _out_proj_body(a1_ref, a2_ref, a3_ref, w_ref, r_ref, o_ref, wb_ref, *, seq_total, tm):
    i = pl.program_id(1)

    @pl.when(i == 0)
    def _():
        wb_ref[...] = w_ref[...].astype(BF16)

    k1 = a1_ref.shape[1]
    k2 = k1 + a2_ref.shape[1]
    acc = jnp.dot(a1_ref[...], wb_ref[0:k1, :], preferred_element_type=F32)
    acc += jnp.dot(a2_ref[...], wb_ref[k1:k2, :], preferred_element_type=F32)
    acc += jnp.dot(a3_ref[...], wb_ref[k2:, :], preferred_element_type=F32)
    pos = _batch_row0(i, seq_total, tm) + lax.broadcasted_iota(jnp.int32, acc.shape, 0)
    o_ref[...] = r_ref[...] + jnp.where(pos >= N_PAD, acc, 0.0)


def out_proj(a1, a2, a3, w, res, *, seq_total, tm, tn, layer=None):
    m = a1.shape[0]
    kdim, n = w.shape[-2:]
    assert a1.shape[1] + a2.shape[1] + a3.shape[1] == kdim
    assert m % tm == 0 and n % tn == 0 and seq_total % tm == 0
    a_spec = lambda a: pl.BlockSpec((tm, a.shape[1]), lambda j, i: (i, 0))
    return pl.pallas_call(
        functools.partial(_out_proj_body, seq_total=seq_total, tm=tm),
        grid=(n // tn, m // tm),
        in_specs=[a_spec(a1), a_spec(a2), a_spec(a3),
                  _wspec(w, layer, (kdim, tn), lambda j, i: (0, j), pipeline_mode=pl.Buffered(1)),
                  pl.BlockSpec((tm, tn), lambda j, i: (i, j))],
        out_specs=pl.BlockSpec((tm, tn), lambda j, i: (i, j)),
        out_shape=jax.ShapeDtypeStruct((m, n), F32),
        scratch_shapes=[pltpu.VMEM((kdim, tn), BF16)],
        compiler_params=_params("parallel", "arbitrary"),
        name="out_proj",
    )(a1, a2, a3, w, res)


CONV_CHUNK = 32
UP_ROW_CHUNK = 128
UP_K_CHUNK = 1024


def _ffn_up_body(a_ref, wg_ref, wv_ref, cwg_ref, cwv_ref, cbg_ref, cbv_ref, o_ref,
                 wcat_ref, u0_ref, u1_ref, halo_ref, *, seq_total, tm, tc, nj, nsteps):
    s = pl.program_id(0)
    cur = jnp.minimum(s, nsteps - 1)
    prev = jnp.maximum(s - 1, 0)
    row0_cur = _batch_row0(cur // nj, seq_total, tm)
    row0_prev = _batch_row0(prev // nj, seq_total, tm)
    j_prev = prev % nj

    @pl.when(s == 0)
    def _():
        u1_ref[...] = jnp.zeros_like(u1_ref)
        halo_ref[...] = jnp.zeros_like(halo_ref)

    mc = _pick(tm, UP_ROW_CHUNK, CONV_CHUNK)
    kc = _pick(a_ref.shape[1], UP_K_CHUNK, LANES)
    n_dots = (tm // mc) * (a_ref.shape[1] // kc)
    gate_rows = list(range(0, tm, CONV_CHUNK))
    per_dot = -(-len(gate_rows) // n_dots)

    def step(uc, up):
        up[0:SUBLANES, :] = jnp.where(row0_prev > 0, halo_ref[j_prev], 0.0)
        halo_ref[j_prev] = up[tm:tm + SUBLANES, :]
        wcat_ref[:, 0:tc] = wg_ref[...].astype(BF16)
        wcat_ref[:, tc:2 * tc] = wv_ref[...].astype(BF16)
        cw = jnp.concatenate([cwg_ref[...], cwv_ref[...]], axis=1)
        cb = jnp.concatenate([cbg_ref[...], cbv_ref[...]], axis=1)

        def gate_stage(r0):
            cv = cb
            for tap in range(CONV_W):
                lo = SUBLANES + r0 - (CONV_W - 1 - tap)
                cv = cv + cw[tap:tap + 1, :] * up[lo:lo + CONV_CHUNK, :]
            o_ref[r0:r0 + CONV_CHUNK, :] = (_silu(cv[:, 0:tc]) * cv[:, tc:2 * tc]).astype(o_ref.dtype)

        todo = list(gate_rows)
        nk = a_ref.shape[1] // kc
        for m0 in range(0, tm, mc):
            rows = slice(SUBLANES + m0, SUBLANES + m0 + mc)
            for ki in range(nk):
                part = jnp.dot(a_ref[m0:m0 + mc, ki * kc:(ki + 1) * kc], wcat_ref[ki * kc:(ki + 1) * kc, :],
                               preferred_element_type=F32)
                if ki > 0:
                    part = uc[rows, :] + part
                if ki == nk - 1:
                    pos = row0_cur + m0 + lax.broadcasted_iota(jnp.int32, part.shape, 0)
                    part = jnp.where(pos >= N_PAD, part, 0.0)
                uc[rows, :] = part
                for r0 in todo[:per_dot]:
                    gate_stage(r0)
                todo = todo[per_dot:]
        assert not todo

    @pl.when(s % 2 == 0)
    def _():
        step(u0_ref, u1_ref)

    @pl.when(s % 2 == 1)
    def _():
        step(u1_ref, u0_ref)


def ffn_up(hn, w_up, conv_w, conv_b, *, seq_total, tm, tc, d_ff, layer=None):
    m, kdim = hn.shape
    assert m % tm == 0 and seq_total % tm == 0 and tm % CONV_CHUNK == 0 and d_ff % tc == 0
    nj = d_ff // tc
    nsteps = (m // tm) * nj
    cur = lambda s: jnp.minimum(s, nsteps - 1)
    prev = lambda s: jnp.maximum(s - 1, 0)
    wspec = lambda off: _wspec(w_up, layer, (kdim, tc), lambda s: (0, cur(s) % nj + off))
    cspec = lambda c, r, off: _wspec(c, layer, (r, tc), lambda s: (0, prev(s) % nj + off))
    conv_b = conv_b.reshape(conv_b.shape[:-1] + (1, conv_b.shape[-1]))
    return pl.pallas_call(
        functools.partial(_ffn_up_body, seq_total=seq_total, tm=tm, tc=tc, nj=nj, nsteps=nsteps),
        grid=(nsteps + 1,),
        in_specs=[pl.BlockSpec((tm, kdim), lambda s: (cur(s) // nj, 0), pipeline_mode=pl.Buffered(1)),
                  wspec(0), wspec(nj), cspec(conv_w, CONV_W, 0), cspec(conv_w, CONV_W, nj),
                  cspec(conv_b, 1, 0), cspec(conv_b, 1, nj)],
        out_specs=pl.BlockSpec((tm, tc), lambda s: (prev(s) // nj, prev(s) % nj)),
        out_shape=jax.ShapeDtypeStruct((m, d_ff), BF16),
        scratch_shapes=[pltpu.VMEM((kdim, 2 * tc), BF16),
                        pltpu.VMEM((tm + SUBLANES, 2 * tc), F32),
                        pltpu.VMEM((tm + SUBLANES, 2 * tc), F32),
                        pltpu.VMEM((nj, SUBLANES, 2 * tc), F32)],
        compiler_params=_params("arbitrary"),
        name="ffn_up",
    )(hn, w_up, w_up, conv_w, conv_w, conv_b, conv_b)


def _ffn_down_body(a_ref, w_ref, r_ref, o_ref, *, seq_total, tm, tk, d_ff, nk):
    k = pl.program_id(2)

    def part(a, w):
        return jnp.dot(a, w.astype(BF16), preferred_element_type=F32)

    @pl.when(k == 0)
    def _():
        o_ref[...] = part(a_ref[...], w_ref[...])

    @pl.when(jnp.logical_and(k > 0, k < nk - 1))
    def _():
        o_ref[...] += part(a_ref[...], w_ref[...])

    @pl.when(k == nk - 1)
    def _():
        acol = k * tk + lax.broadcasted_iota(jnp.int32, a_ref.shape, 1)
        a = jnp.where(acol < d_ff, a_ref[...], jnp.zeros(a_ref.shape, a_ref.dtype))
        wrow = k * tk + lax.broadcasted_iota(jnp.int32, w_ref.shape, 0)
        w = jnp.where(wrow < d_ff, w_ref[...], 0.0)
        pos = _batch_row0(pl.program_id(0), seq_total, tm) + lax.broadcasted_iota(
            jnp.int32, o_ref.shape, 0)
        o_ref[...] = r_ref[...] + jnp.where(pos >= N_PAD, o_ref[...] + part(a, w), 0.0)


def ffn_down(act, w, res, *, seq_total, tm, tn, tk, layer=None):
    m, d_ff = act.shape
    n = w.shape[-1]
    assert m % tm == 0 and n % tn == 0 and seq_total % tm == 0 and w.shape[-2] == d_ff
    nk = pl.cdiv(d_ff, tk)
    assert nk >= 2
    return pl.pallas_call(
        functools.partial(_ffn_down_body, seq_total=seq_total, tm=tm, tk=tk, d_ff=d_ff, nk=nk),
        grid=(m // tm, n // tn, nk),
        in_specs=[pl.BlockSpec((tm, tk), lambda i, j, k: (i, k)),
                  _wspec(w, layer, (tk, tn), lambda i, j, k: (k, j)),
                  pl.BlockSpec((tm, tn), lambda i, j, k: (i, j))],
        out_specs=pl.BlockSpec((tm, tn), lambda i, j, k: (i, j)),
        out_shape=jax.ShapeDtypeStruct((m, n), F32),
        compiler_params=_params("parallel", "parallel", "arbitrary"),
        name="ffn_down",
    )(act, w, res)


def _scores(k_rows, q_parts):
    out = []
    for q in q_parts:
        s = jnp.dot(k_rows, q, preferred_element_type=F32)
        out.append((s, jnp.max(s, axis=0, keepdims=True)))
    return tuple(out)


def _causal(scored, col0s):
    out = []
    for (s, _), c0 in zip(scored, col0s):
        key = lax.broadcasted_iota(jnp.int32, s.shape, 0)
        qry = c0 + lax.broadcasted_iota(jnp.int32, s.shape, 1)
        s = jnp.where(key <= qry, s, NEG_INF)
        out.append((s, jnp.max(s, axis=0, keepdims=True)))
    return tuple(out)


def _softmax_pv(carry, scored, v_ext):
    out = []
    for (m, acc), (s, s_max) in zip(carry, scored):
        m_new = jnp.maximum(m, s_max)
        alpha = jnp.exp2(m - m_new)
        p = jnp.exp2(s - m_new).astype(BF16)
        out.append((m_new, alpha * acc + jnp.dot(v_ext, p, preferred_element_type=F32)))
    return tuple(out)


def _attn_init(widths):
    return tuple((jnp.full((1, w), NEG_INF, F32), jnp.zeros((V_ROWS, w), F32)) for w in widths)


def _attn_finish(carry, gain, o_ref, row0, col0s):
    for (_, acc), c0 in zip(carry, col0s):
        o = jnp.transpose(acc[0:LANES, :] / acc[LANES:LANES + 1, :])
        o = o * lax.rsqrt(jnp.mean(o * o, axis=-1, keepdims=True) + RMS_EPS) * gain
        o_ref[pl.ds(row0 + c0, o.shape[0]), :] = o.astype(o_ref.dtype)


def _query_parts(tb):
    part = QUERY_PART if tb % QUERY_PART == 0 else tb
    return part, tuple(range(0, tb, part))


def _attention_core(qt, qt0, ks, vt, vt0, s_scr, m_scr, g_ref, o_ref, *, nblk, tb):
    heads = range(qt.shape[0])
    gains = [g_ref[:, h * LANES:(h + 1) * LANES] for h in heads]
    outs = [o_ref.at[:, pl.ds(h * LANES, LANES)] for h in heads]

    for h in heads:
        lead = _causal(_scores(ks[h, 0:LEAD, :], (qt0[h],)), (0,))
        _attn_finish(_softmax_pv(_attn_init((LEAD,)), lead, vt0[h]), gains[h], outs[h], 0, (0,))

    part, col0s = _query_parts(tb)
    widths = (part,) * len(col0s)

    def q_block(qi, _):
        q_parts = [tuple(qt[h, qi, :, c0:c0 + part] for c0 in col0s) for h in heads]

        def score(slot, kj):
            for h in heads:
                k_rows = ks[h, pl.ds(pl.multiple_of(LEAD + kj * tb, LANES), tb), :]
                for p, (s, s_max) in enumerate(_scores(k_rows, q_parts[h])):
                    s_scr[h, slot, p] = s
                    m_scr[h, slot, p] = s_max

        def scored(h, slot):
            return tuple((s_scr[h, slot, p], m_scr[h, slot, p]) for p in range(len(col0s)))

        def consume(carry, slot, kj, causal):
            out = []
            for h in heads:
                sc = scored(h, slot)
                if causal:
                    sc = _causal(sc, col0s)
                out.append(_softmax_pv(carry[h], sc, vt[h, kj]))
            return tuple(out)

        carry = tuple(_softmax_pv(_attn_init(widths), _scores(ks[h, 0:LEAD, :], q_parts[h]), vt0[h])
                      for h in heads)
        score(0, 0)

        def pair(t, carry):
            kj = 2 * t
            score(1, kj + 1)
            carry = consume(carry, 0, kj, False)
            score(0, kj + 2)
            return consume(carry, 1, kj + 1, False)

        carry = lax.fori_loop(0, qi // 2, pair, carry)

        def odd_tail(carry):
            score(1, qi)
            return consume(consume(carry, 0, qi - 1, False), 1, qi, True)

        def even_tail(carry):
            return consume(carry, 0, qi, True)

        carry = lax.cond(qi % 2 == 1, odd_tail, even_tail, carry)
        for h in heads:
            _attn_finish(carry[h], gains[h], outs[h], pl.multiple_of(LEAD + qi * tb, LANES), col0s)
        return 0

    lax.fori_loop(0, nblk, q_block, 0)


def _store_v_ext(vt_dst, v_rows):
    n = v_rows.shape[0]
    vt_dst[0:LANES, :] = _to_t(v_rows)
    r = lax.broadcasted_iota(jnp.int32, (V_ROWS - LANES, n), 0)
    vt_dst[LANES:V_ROWS, :] = jnp.where(r == 0, 1.0, 0.0).astype(BF16)


def _to_t(x):
    return jnp.transpose(x.astype(F32)).astype(BF16)


def _mla_attn_body(q_ref, kv_ref, kr_ref, cos_ref, sa_ref, sb_ref, g_ref, o_ref,
                   qt, qt0, ks, vt, vt0, s_scr, m_scr, *, nblk, tb):
    scale = (MLA_NOPE + MLA_ROPE) ** -0.5 * LOG2E
    heads = range(qt.shape[0])

    def rope(x, rows):
        return (x * cos_ref[rows, :] + pltpu.roll(x, HALF_ROPE, 1) * sa_ref[rows, :]
                + pltpu.roll(x, LANES - HALF_ROPE, 1) * sb_ref[rows, :])

    def prep(r0, n, qt_dst, vt_dst):
        rows = pl.ds(r0, n)
        one = lax.broadcasted_iota(jnp.int32, (n, LANES), 1) == ONE_LANE
        pos = r0 + lax.broadcasted_iota(jnp.int32, (n, LANES), 0)
        kp = rope(kr_ref[rows, :].astype(F32), rows)
        kp = jnp.where(one, jnp.where(pos < N_PAD, NEG_INF, 0.0), kp).astype(BF16)
        for h in heads:
            c0 = h * 2 * LANES
            qn = q_ref[rows, pl.ds(c0, LANES)].astype(F32) * scale
            qp = rope(q_ref[rows, pl.ds(c0 + LANES, LANES)].astype(F32), rows) * scale
            qp = jnp.where(one, 1.0, qp)
            qt_dst(h)[0:LANES, :] = jnp.transpose(qn).astype(BF16)
            qt_dst(h)[LANES:2 * LANES, :] = jnp.transpose(qp).astype(BF16)
            ks[h, rows, pl.ds(0, LANES)] = kv_ref[rows, pl.ds(c0, LANES)]
            ks[h, rows, pl.ds(LANES, LANES)] = kp
            _store_v_ext(vt_dst(h), kv_ref[rows, pl.ds(c0 + LANES, LANES)])

    prep(0, LEAD, lambda h: qt0.at[h], lambda h: vt0.at[h])

    def prep_block(i, _):
        prep(pl.multiple_of(LEAD + i * tb, LANES), tb, lambda h: qt.at[h, i], lambda h: vt.at[h, i])
        return 0

    lax.fori_loop(0, nblk, prep_block, 0)
    _attention_core(qt, qt0, ks, vt, vt0, s_scr, m_scr, g_ref, o_ref, nblk=nblk, tb=tb)


def _attn_scratch(heads, seq_total, nblk, tb):
    part, col0s = _query_parts(tb)
    return [pltpu.VMEM((heads, nblk, 2 * LANES, tb), BF16), pltpu.VMEM((heads, 2 * LANES, LEAD), BF16),
            pltpu.VMEM((heads, seq_total, 2 * LANES), BF16),
            pltpu.VMEM((heads, nblk, V_ROWS, tb), BF16), pltpu.VMEM((heads, V_ROWS, LEAD), BF16),
            pltpu.VMEM((heads, 2, len(col0s), tb, part), F32),
            pltpu.VMEM((heads, 2, len(col0s), 1, part), F32)]


def mla_attention(q_raw, kv, proj, cos_t, sin_a, sin_b, gain, *, batch, seq_total, tb):
    t = q_raw.shape[0]
    nblk = (seq_total - LEAD) // tb
    hps = ATTN_HEADS_PER_STEP
    assert MLA_HEADS % hps == 0
    big = lambda w: pl.BlockSpec((seq_total, hps * w), lambda b, h: (b, h))
    table = pl.BlockSpec((seq_total, LANES), lambda b, h: (0, 0), pipeline_mode=pl.Buffered(1))
    return pl.pallas_call(
        functools.partial(_mla_attn_body, nblk=nblk, tb=tb),
        grid=(batch, MLA_HEADS // hps),
        in_specs=[big(2 * LANES), big(2 * LANES),
                  pl.BlockSpec((seq_total, LANES), lambda b, h: (b, COL_KR // LANES)),
                  table, table, table,
                  pl.BlockSpec((1, hps * LANES), lambda b, h: (0, h))],
        out_specs=big(LANES),
        out_shape=jax.ShapeDtypeStruct((t, W_MLA), BF16),
        scratch_shapes=_attn_scratch(hps, seq_total, nblk, tb),
        compiler_params=_params("parallel", "parallel"),
        name="mla_attention",
    )(q_raw, kv, proj, cos_t, sin_a, sin_b, gain.reshape(1, W_MLA).astype(F32))


def _fox_attn_body(q_ref, k_ref, v_ref, aq_ref, ak_ref, g_ref, o_ref,
                   qt, qt0, ks, vt, vt0, s_scr, m_scr, *, nblk, tb):
    scale = FOX_DH ** -0.5 * LOG2E
    heads = range(qt.shape[0])

    def prep(r0, n, qt_dst, vt_dst):
        rows = pl.ds(r0, n)
        for h in heads:
            cols = pl.ds(h * LANES, LANES)
            qt_dst(h)[0:LANES, :] = jnp.transpose(q_ref[rows, cols].astype(F32) * scale).astype(BF16)
            qt_dst(h)[LANES:2 * LANES, :] = _to_t(aq_ref[rows, cols])
            ks[h, rows, pl.ds(0, LANES)] = k_ref[rows, cols]
            ks[h, rows, pl.ds(LANES, LANES)] = ak_ref[rows, cols]
            _store_v_ext(vt_dst(h), v_ref[rows, cols])

    prep(0, LEAD, lambda h: qt0.at[h], lambda h: vt0.at[h])

    def prep_block(i, _):
        prep(pl.multiple_of(LEAD + i * tb, LANES), tb, lambda h: qt.at[h, i], lambda h: vt.at[h, i])
        return 0

    lax.fori_loop(0, nblk, prep_block, 0)
    _attention_core(qt, qt0, ks, vt, vt0, s_scr, m_scr, g_ref, o_ref, nblk=nblk, tb=tb)


def fox_attention(proj, aug_q, aug_k, gain, *, batch, seq_total, tb):
    t = proj.shape[0]
    nblk = (seq_total - LEAD) // tb
    hps = ATTN_HEADS_PER_STEP
    assert FOX_HEADS % hps == 0
    wb = hps * LANES
    col = lambda c0: pl.BlockSpec((seq_total, wb), lambda b, h: (b, c0 // wb + h))
    assert COL_FQ % wb == 0 and COL_FK % wb == 0 and COL_FV % wb == 0
    return pl.pallas_call(
        functools.partial(_fox_attn_body, nblk=nblk, tb=tb),
        grid=(batch, FOX_HEADS // hps),
        in_specs=[col(COL_FQ), col(COL_FK), col(COL_FV), col(0), col(0),
                  pl.BlockSpec((1, wb), lambda b, h: (0, h))],
        out_specs=col(0),
        out_shape=jax.ShapeDtypeStruct((t, W_FOX), BF16),
        scratch_shapes=_attn_scratch(hps, seq_total, nblk, tb),
        compiler_params=_params("parallel", "parallel"),
        name="fox_attention",
    )(proj, proj, proj, aug_q, aug_k, gain.reshape(1, W_FOX).astype(F32))


def _split3(x):
    hi = x.astype(BF16)
    r1 = x - hi.astype(F32)
    mid = r1.astype(BF16)
    lo = (r1 - mid.astype(F32)).astype(BF16)
    return hi.astype(F32), mid.astype(F32), lo.astype(F32)


def _fox_prep_body(z_ref, b_ref, aq_ref, ak_ref, carry_ref, *, tr):
    i = pl.program_id(1)

    @pl.when(i == 0)
    def _():
        carry_ref[...] = jnp.zeros_like(carry_ref)

    row = i * tr + lax.broadcasted_iota(jnp.int32, (tr, LANES), 0)
    valid = row >= N_PAD
    log_f = jnp.where(valid, _log_sigmoid(z_ref[...] + b_ref[...]), 0.0)
    r = lax.broadcasted_iota(jnp.int32, (tr, tr), 0)
    c = lax.broadcasted_iota(jnp.int32, (tr, tr), 1)
    tri = jnp.where(c <= r, 1.0, 0.0).astype(F32)
    csum = jnp.dot(tri, log_f, preferred_element_type=F32,
                   precision=lax.Precision.HIGHEST) + carry_ref[0:1, :]
    carry_ref[...] = jnp.broadcast_to(csum[tr - 1:tr, :], carry_ref.shape)

    lane = lax.broadcasted_iota(jnp.int32, (tr, LANES), 1)
    key_mask = jnp.where(valid, 0.0, NEG_INF)
    for h in range(FOX_HEADS):
        col = csum[:, GATE_FZ + h:GATE_FZ + h + 1] * LOG2E
        hi, mid, lo = _split3(col)
        aq = jnp.where(lane == 0, hi, jnp.where(lane == 1, mid, jnp.where(lane == 2, lo,
             jnp.where(lane < 7, 1.0, 0.0))))
        ak = jnp.where(lane < 3, 1.0, jnp.where(lane == 3, -hi, jnp.where(lane == 4, -mid,
             jnp.where(lane == 5, -lo, jnp.where(lane == 6, key_mask, 0.0)))))
        aq_ref[:, h * LANES:(h + 1) * LANES] = aq.astype(BF16)
        ak_ref[:, h * LANES:(h + 1) * LANES] = ak.astype(BF16)


def fox_prep(gates, fox_b, *, batch, seq_total):
    t = gates.shape[0]
    tr = _pick(seq_total, 384, LANES)
    nblk = seq_total // tr
    bias = jnp.zeros((1, LANES), F32).at[0, GATE_FZ:GATE_FZ + FOX_HEADS].set(fox_b.astype(F32))
    out = jax.ShapeDtypeStruct((t, W_FOX), BF16)
    return pl.pallas_call(
        functools.partial(_fox_prep_body, tr=tr),
        grid=(batch, nblk),
        in_specs=[pl.BlockSpec((tr, LANES), lambda b, i: (b * nblk + i, 0)),
                  pl.BlockSpec((1, LANES), lambda b, i: (0, 0))],
        out_specs=[pl.BlockSpec((tr, W_FOX), lambda b, i: (b * nblk + i, 0))] * 2,
        out_shape=[out, out],
        scratch_shapes=[pltpu.VMEM((SUBLANES, LANES), F32)],
        compiler_params=_params("parallel", "arbitrary"),
        name="fox_prep",
    )(gates, bias)


def _gla_body(q_ref, k_ref, v_ref, r_ref, z_ref, w2_ref, b2_ref, g_ref, o_ref, s_ref,
              *, seq_total):
    c = GLA_CHUNK
    nchunk = seq_total // c
    gsz = max(g for g in range(1, GLA_GROUP + 1) if nchunk % g == 0)
    rg = gsz * c
    heads = s_ref.shape[0]
    nt = (((1,), (1,)), ((), ()))
    tn = (((0,), (0,)), ((), ()))
    r_i = lax.broadcasted_iota(jnp.int32, (rg, rg), 0)
    c_i = lax.broadcasted_iota(jnp.int32, (rg, rg), 1)
    assert c & (c - 1) == 0
    lower = jnp.logical_and(c_i <= r_i, c_i >= jnp.bitwise_and(r_i, -c))
    tri = jnp.where(lower, 1.0, 0.0).astype(F32)
    s_ref[...] = jnp.zeros_like(s_ref)

    def group(i, _):
        r0 = pl.multiple_of(i * rg, rg)
        rows = pl.ds(r0, rg)
        valid = (r0 + lax.broadcasted_iota(jnp.int32, (rg, GLA_DK), 0)) >= N_PAD
        z = z_ref[rows, :].astype(BF16)
        for hh in range(heads):
            kcols = pl.ds(hh * GLA_DK, GLA_DK)
            vcols = pl.ds(hh * GLA_DV, GLA_DV)
            q = q_ref[rows, kcols].astype(F32) * (GLA_DK ** -0.5)
            k = jnp.where(valid, k_ref[rows, kcols].astype(F32), 0.0)
            v = v_ref[rows, vcols]
            logit = jnp.dot(z, w2_ref[:, kcols], preferred_element_type=F32) + b2_ref[:, kcols]
            log_a = jnp.where(valid, _log_sigmoid(logit) / GLA_TAU, 0.0)
            bc = jnp.dot(tri, log_a, preferred_element_type=F32, precision=lax.Precision.HIGHEST)
            b_last = jnp.concatenate(
                [jnp.broadcast_to(bc[(n + 1) * c - 1:(n + 1) * c, :], (c, GLA_DK)) for n in range(gsz)], axis=0)
            q_dec = (q * jnp.exp(bc)).astype(BF16)
            k_dec = (k * jnp.exp(-bc)).astype(BF16)
            k_state = (k * jnp.exp(b_last - bc)).astype(BF16)
            a = lax.dot_general(q_dec, k_dec, nt, preferred_element_type=F32)
            a = jnp.where(lower, a, 0.0).astype(BF16)
            o_intra = jnp.dot(a, v, preferred_element_type=F32)
            state = s_ref[hh]
            outs = []
            for n in range(gsz):
                sl = slice(n * c, (n + 1) * c)
                outs.append(o_intra[sl] + jnp.dot(q_dec[sl], state.astype(BF16),
                                                   preferred_element_type=F32))
                decay = jnp.exp(b_last[n * c:n * c + 1, :])
                dec_t = jnp.transpose(jnp.broadcast_to(decay, (GLA_DK, GLA_DK)))
                dec_t = jnp.concatenate([dec_t, dec_t], axis=1)
                state = state * dec_t + lax.dot_general(k_state[sl], v[sl], tn,
                                                        preferred_element_type=F32)
            s_ref[hh] = state
            o = jnp.concatenate(outs, axis=0)
            o = o * lax.rsqrt(jnp.mean(o * o, axis=-1, keepdims=True) + RMS_EPS) * g_ref[:, vcols]
            o_ref[rows, vcols] = (o * _silu(r_ref[rows, vcols].astype(F32))).astype(o_ref.dtype)
        return 0

    lax.fori_loop(0, nchunk // gsz, group, 0)


def gla(proj, gates, w2, b2, gain, *, batch, seq_total):
    t = proj.shape[0]
    hps = GLA_HEADS_PER_STEP
    assert GLA_HEADS % hps == 0
    wk, wv = hps * GLA_DK, hps * GLA_DV
    blk = lambda w, c0: pl.BlockSpec((seq_total, w), lambda b, h: (b, c0 // w + h))
    return pl.pallas_call(
        functools.partial(_gla_body, seq_total=seq_total),
        grid=(batch, GLA_HEADS // hps),
        in_specs=[blk(wk, COL_GQ), blk(wk, COL_GK), blk(wv, COL_GV), blk(wv, COL_GR),
                  pl.BlockSpec((seq_total, LANES), lambda b, h: (b, 0)),
                  pl.BlockSpec((LANES, wk), lambda b, h: (0, h)),
                  pl.BlockSpec((1, wk), lambda b, h: (0, h)),
                  pl.BlockSpec((1, wv), lambda b, h: (0, h))],
        out_specs=pl.BlockSpec((seq_total, wv), lambda b, h: (b, h)),
        out_shape=jax.ShapeDtypeStruct((t, W_GLA), BF16),
        scratch_shapes=[pltpu.VMEM((hps, GLA_DK, GLA_DV), F32)],
        compiler_params=_params("parallel", "parallel"),
        name="gla",
    )(proj, proj, proj, proj, gates, w2, b2, gain.reshape(1, W_GLA).astype(F32))


def _rope_tables(seq_total):
    pos = np.maximum(np.arange(seq_total) - N_PAD, 0).astype(np.float32)
    inv_freq = (1.0 / (ROPE_THETA ** (np.arange(0, MLA_ROPE, 2, dtype=np.float32) / MLA_ROPE))).astype(np.float32)
    ang = jnp.asarray(pos)[:, None] * jnp.asarray(inv_freq)[None, :]
    cos, sin = jnp.cos(ang), jnp.sin(ang)
    z = jnp.zeros_like(cos)
    cos_t = jnp.concatenate([cos, cos, z, z], axis=1)
    sin_a = jnp.concatenate([z, sin, z, z], axis=1)
    sin_b = jnp.concatenate([-sin, z, z, z], axis=1)
    return cos_t, sin_a, sin_b


def kernel(x, meta_tokens, attn_norm, w_in, mla_q_norm, mla_w_uq, mla_kv_norm, mla_w_ukv,
           gla_w_gate2, gla_b_gate, fox_b_f, out_norm_mla, out_norm_gla, out_norm_fox,
           w_out, ffn_norm, ffn_w_up, ffn_conv_w, ffn_conv_b, ffn_w_down, final_norm):
    batch, seq, d_model = x.shape
    depth = w_in.shape[0]
    d_ff = ffn_w_down.shape[1]
    seq_total = LEAD + seq
    cos_t, sin_a, sin_b = _rope_tables(seq_total)

    tm_big = _pick(seq_total, TM_BIG, LANES)
    tm_mid = _pick(seq_total, TM_MID, LANES)
    tb = _pick(seq, ATTN_BLOCK, LANES)
    tc = _pick(d_ff, 256, LANES)

    w_in_b = transpose_cast(jnp.transpose(w_in, (2, 0, 1)))

    h, hn = embed_norm(x, meta_tokens, attn_norm[0])
    for layer in range(depth):
        w_big, w_gates = relayout_cols(w_in_b, (W_IN_PLAN, W_GATE_PLAN), (N_PROJ, LANES), layer=layer)
        w_uq, = relayout_cols(mla_w_uq, (W_UQ_PLAN,), (MLA_HEADS * 2 * LANES,), layer=layer)
        if layer > 0:
            hn = rmsnorm(h, attn_norm[layer])
        proj = matmul(hn, w_big, tm=tm_big, tn=PROJ_TN)
        gates = matmul(hn, w_gates, tm=tm_big, tn=LANES, out_dtype=F32)

        cq = rmsnorm(proj, mla_q_norm[layer], col0=COL_CQ, width=MLA_Q_LORA)
        ckv = rmsnorm(proj, mla_kv_norm[layer], col0=COL_CKV, width=MLA_KV_LORA)
        q_raw = matmul(cq, w_uq, tm=tm_big, tn=1024)
        kv = matmul(ckv, mla_w_ukv, tm=tm_big, tn=1024, layer=layer)
        o_mla = mla_attention(q_raw, kv, proj, cos_t, sin_a, sin_b, out_norm_mla[layer],
                              batch=batch, seq_total=seq_total, tb=tb)

        w2 = jnp.zeros((LANES, W_GQK), F32).at[:GLA_GATE_RANK].set(gla_w_gate2[layer])
        o_gla = gla(proj, gates, w2.astype(BF16), gla_b_gate[layer].reshape(1, -1),
                    out_norm_gla[layer], batch=batch, seq_total=seq_total)

        aug_q, aug_k = fox_prep(gates, fox_b_f[layer], batch=batch, seq_total=seq_total)
        o_fox = fox_attention(proj, aug_q, aug_k, out_norm_fox[layer],
                              batch=batch, seq_total=seq_total, tb=tb)

        h = out_proj(o_mla, o_gla, o_fox, w_out, h, seq_total=seq_total, tm=tm_mid, tn=1024,
                     layer=layer)

        hn = rmsnorm(h, ffn_norm[layer])
        act = ffn_up(hn, ffn_w_up, ffn_conv_w, ffn_conv_b, seq_total=seq_total, tm=tm_big,
                     tc=tc, d_ff=d_ff, layer=layer)
        h = ffn_down(act, ffn_w_down, h, seq_total=seq_total, tm=tm_big, tn=1024, tk=DOWN_TK,
                     layer=layer)

    return final_rmsnorm(h, final_norm, batch, seq_total)
```

```python
import functools

import jax
import jax.numpy as jnp
import numpy as np
from jax import lax
from jax.experimental import pallas as pl
from jax.experimental.pallas import tpu as pltpu

F32 = jnp.float32
BF16 = jnp.bfloat16

N_META = 16
LEAD = 128
N_PAD = LEAD - N_META
RMS_EPS = 1e-6
NEG_INF = -1e30

MLA_V = 128
MLA_HEADS = 12
MLA_NOPE = 128
MLA_ROPE = 64
MLA_Q_LORA = 1536
MLA_KV_LORA = 512
ROPE_THETA = 10000.0

GLA_DV = 256
GLA_DK = 128
GLA_HEADS = 4
GLA_GATE_RANK = 16
GLA_TAU = 16.0
GLA_CHUNK = 64

FOX_DH = 128
FOX_HEADS = 12

W_MLA = MLA_HEADS * MLA_V
W_GLA = GLA_HEADS * GLA_DV
W_FOX = FOX_HEADS * FOX_DH
CONV_W = 3

LANES = 128
SUBLANES = 8
VMEM_LIMIT_BYTES = 56 * 1024 * 1024

TM_BIG = 1408
TM_MID = 704
ATTN_BLOCK = 512
DOWN_TK = 1024
GLA_HEADS_PER_STEP = 2
GLA_GROUP = 6
QUERY_PART = 256
ATTN_HEADS_PER_STEP = 2

HALF_ROPE = MLA_ROPE // 2
W_GQK = GLA_HEADS * GLA_DK
COL_CQ = 0
COL_CKV = COL_CQ + MLA_Q_LORA
COL_GV = COL_CKV + MLA_KV_LORA
COL_GR = COL_GV + W_GLA
COL_GQ = COL_GR + W_GLA
COL_GK = COL_GQ + W_GQK
COL_FQ = COL_GK + W_GQK
COL_FK = COL_FQ + W_FOX
COL_FV = COL_FK + W_FOX
COL_KR = COL_FV + W_FOX
MXU_COLS = 256
PROJ_TN = 3 * MXU_COLS
N_PROJ = -(-(COL_KR + LANES) // PROJ_TN) * PROJ_TN
SRC_CQ = 0
SRC_CKV = SRC_CQ + MLA_Q_LORA
SRC_KR = SRC_CKV + MLA_KV_LORA
SRC_GQ = SRC_KR + MLA_ROPE
SRC_GK = SRC_GQ + W_GQK
SRC_GV = SRC_GK + W_GQK
SRC_GZ = SRC_GV + W_GLA
SRC_GR = SRC_GZ + GLA_GATE_RANK
SRC_FQ = SRC_GR + W_GLA
SRC_FK = SRC_FQ + W_FOX
SRC_FV = SRC_FK + W_FOX
SRC_FZ = SRC_FV + W_FOX
GATE_FZ = GLA_GATE_RANK
ONE_LANE = MLA_ROPE
LOG2E = 1.4426950408889634
V_ROWS = LANES + 16


def _pick(n, target, mult):
    best = None
    for d in range(mult, min(n, target) + 1, mult):
        if n % d == 0:
            best = d
    assert best is not None, (n, target, mult)
    return best


def _params(*sem):
    return pltpu.CompilerParams(dimension_semantics=sem, vmem_limit_bytes=VMEM_LIMIT_BYTES)


def _log_sigmoid(x):
    return jnp.minimum(x, 0.0) - jnp.log1p(jnp.exp(-jnp.abs(x)))


def _silu(x):
    return x / (1.0 + jnp.exp(-x))


def _wspec(w, layer, block, index, **kw):
    if layer is None:
        assert w.ndim == len(block)
        return pl.BlockSpec(block, index, **kw)
    assert w.ndim == len(block) + 1
    return pl.BlockSpec((None,) + tuple(block), lambda *g: (layer,) + tuple(index(*g)), **kw)


def _batch_row0(step, seq_total, tm):
    return (step % (seq_total // tm)) * tm


def _relayout_body(x_ref, *o_refs, plans):
    for o_ref, plan in zip(o_refs, plans):
        width = o_ref.shape[1]
        pos = 0
        for src, dst, w in plan:
            assert dst >= pos
            if dst > pos:
                o_ref[:, pos:dst] = jnp.zeros((o_ref.shape[0], dst - pos), o_ref.dtype)
            o_ref[:, dst:dst + w] = x_ref[:, src:src + w].astype(o_ref.dtype)
            pos = dst + w
        if pos < width:
            o_ref[:, pos:width] = jnp.zeros((o_ref.shape[0], width - pos), o_ref.dtype)


def relayout_cols(w, plans, widths, *, layer=None, rows=256):
    k, n = w.shape[-2:]
    tr = _pick(k, rows, 16)
    return pl.pallas_call(
        functools.partial(_relayout_body, plans=plans),
        grid=(k // tr,),
        in_specs=[_wspec(w, layer, (tr, n), lambda i: (i, 0))],
        out_specs=[pl.BlockSpec((tr, wd), lambda i: (i, 0)) for wd in widths],
        out_shape=[jax.ShapeDtypeStruct((k, wd), BF16) for wd in widths],
        compiler_params=_params("parallel"),
        name="relayout_cols",
    )(w)


def _transpose_cast_body(x_ref, o_ref, *, n):
    row = pl.program_id(0) * LANES + lax.broadcasted_iota(jnp.int32, (LANES, x_ref.shape[2]), 0)
    for layer in range(x_ref.shape[1]):
        x = jnp.where(row < n, x_ref[:, layer, :], 0.0)
        o_ref[layer] = jnp.transpose(x).astype(o_ref.dtype)


def transpose_cast(wt):
    n, depth, k = wt.shape
    nb = pl.cdiv(n, LANES)
    return pl.pallas_call(
        functools.partial(_transpose_cast_body, n=n),
        grid=(nb,),
        in_specs=[pl.BlockSpec((LANES, depth, k), lambda i: (i, 0, 0))],
        out_specs=pl.BlockSpec((depth, k, LANES), lambda i: (0, 0, i)),
        out_shape=jax.ShapeDtypeStruct((depth, k, nb * LANES), BF16),
        compiler_params=_params("parallel"),
        name="transpose_cast",
    )(wt)


W_IN_PLAN = sorted([
    (SRC_CQ, COL_CQ, MLA_Q_LORA), (SRC_CKV, COL_CKV, MLA_KV_LORA), (SRC_KR, COL_KR, MLA_ROPE),
    (SRC_GQ, COL_GQ, W_GQK), (SRC_GK, COL_GK, W_GQK), (SRC_GV, COL_GV, W_GLA),
    (SRC_GR, COL_GR, W_GLA), (SRC_FQ, COL_FQ, W_FOX), (SRC_FK, COL_FK, W_FOX),
    (SRC_FV, COL_FV, W_FOX)], key=lambda p: p[1])
W_GATE_PLAN = [(SRC_GZ, 0, GLA_GATE_RANK), (SRC_FZ, GATE_FZ, FOX_HEADS)]
W_UQ_PLAN = [(h * (MLA_NOPE + MLA_ROPE), h * 2 * LANES, MLA_NOPE + MLA_ROPE) for h in range(MLA_HEADS)]


def _rmsnorm_body(x_ref, g_ref, o_ref):
    x = x_ref[...].astype(F32)
    y = x * lax.rsqrt(jnp.mean(x * x, axis=-1, keepdims=True) + RMS_EPS)
    o_ref[...] = (y * g_ref[...]).astype(o_ref.dtype)


def rmsnorm(x, gain, *, out_dtype=BF16, rows=256):
    m, width = x.shape
    tr = _pick(m, rows, 16)
    return pl.pallas_call(
        _rmsnorm_body,
        grid=(m // tr,),
        in_specs=[pl.BlockSpec((tr, width), lambda i: (i, 0)),
                  pl.BlockSpec((1, width), lambda i: (0, 0))],
        out_specs=pl.BlockSpec((tr, width), lambda i: (i, 0)),
        out_shape=jax.ShapeDtypeStruct((m, width), out_dtype),
        compiler_params=_params("parallel"),
        name="rmsnorm",
    )(x, gain.reshape(1, width).astype(F32))


def _final_norm_body(x_ref, g_ref, o_ref):
    x = x_ref[...]
    y = x * lax.rsqrt(jnp.mean(x * x, axis=-1, keepdims=True) + RMS_EPS)
    o_ref[0] = y * g_ref[...]


def final_rmsnorm(h, gain, batch, seq_total):
    d = h.shape[1]
    tr = LEAD
    nblk = seq_total // tr
    return pl.pallas_call(
        _final_norm_body,
        grid=(batch, nblk - 1),
        in_specs=[pl.BlockSpec((tr, d), lambda b, i: (b * nblk + i + 1, 0)),
                  pl.BlockSpec((1, d), lambda b, i: (0, 0))],
        out_specs=pl.BlockSpec((1, tr, d), lambda b, i: (b, i, 0)),
        out_shape=jax.ShapeDtypeStruct((batch, seq_total - LEAD, d), F32),
        compiler_params=_params("parallel", "parallel"),
        name="final_norm",
    )(h, gain.reshape(1, d).astype(F32))


def _embed_norm_body(x_ref, meta_ref, g_ref, h_ref, hn_ref):
    lead = pl.program_id(1) == 0

    def emit(v):
        h_ref[...] = v
        hn = v * lax.rsqrt(jnp.mean(v * v, axis=-1, keepdims=True) + RMS_EPS) * g_ref[...]
        hn_ref[...] = hn.astype(hn_ref.dtype)

    @pl.when(lead)
    def _():
        d = h_ref.shape[1]
        emit(jnp.concatenate([jnp.zeros((N_PAD, d), F32), meta_ref[...]], axis=0))

    @pl.when(jnp.logical_not(lead))
    def _():
        emit(x_ref[0])


def embed_norm(x, meta_tokens, gain):
    batch, seq, d = x.shape
    assert seq % LEAD == 0
    nblk = seq // LEAD + 1
    t = batch * nblk * LEAD
    row = lambda b, i: (b * nblk + i, 0)
    return pl.pallas_call(
        _embed_norm_body,
        grid=(batch, nblk),
        in_specs=[pl.BlockSpec((1, LEAD, d), lambda b, i: (b, jnp.maximum(i - 1, 0), 0)),
                  pl.BlockSpec((N_META, d), lambda b, i: (0, 0)),
                  pl.BlockSpec((1, d), lambda b, i: (0, 0))],
        out_specs=[pl.BlockSpec((LEAD, d), row), pl.BlockSpec((LEAD, d), row)],
        out_shape=[jax.ShapeDtypeStruct((t, d), F32), jax.ShapeDtypeStruct((t, d), BF16)],
        compiler_params=_params("parallel", "arbitrary"),
        name="embed_norm",
    )(x.astype(F32), meta_tokens.astype(F32), gain.reshape(1, d).astype(F32))


def _mm_body(a_ref, w_ref, o_ref):
    o_ref[...] = jnp.dot(a_ref[...], w_ref[...].astype(BF16),
                         preferred_element_type=F32).astype(o_ref.dtype)


def matmul(a, w, *, tm, tn, out_dtype=BF16, layer=None):
    m, kdim = a.shape
    n = w.shape[-1]
    assert m % tm == 0 and n % tn == 0 and w.shape[-2] == kdim
    return pl.pallas_call(
        _mm_body,
        grid=(m // tm, n // tn),
        in_specs=[pl.BlockSpec((tm, kdim), lambda i, j: (i, 0)),
                  _wspec(w, layer, (kdim, tn), lambda i, j: (0, j))],
        out_specs=pl.BlockSpec((tm, tn), lambda i, j: (i, j)),
        out_shape=jax.ShapeDtypeStruct((m, n), out_dtype),
        compiler_params=_params("parallel", "parallel"),
        name="matmul",
    )(a, w)


def _norm_mm_body(x_ref, g_ref, w_ref, o_ref, a_ref):
    @pl.when(pl.program_id(1) == 0)
    def _():
        x = x_ref[...].astype(F32)
        y = x * lax.rsqrt(jnp.mean(x * x, axis=-1, keepdims=True) + RMS_EPS)
        a_ref[...] = (y * g_ref[...]).astype(BF16)

    o_ref[...] = jnp.dot(a_ref[...], w_ref[...].astype(BF16),
                         preferred_element_type=F32).astype(o_ref.dtype)


def norm_matmul(x, gain, w, *, col0, width, tm, tn, layer=None):
    m = x.shape[0]
    n = w.shape[-1]
    assert m % tm == 0 and n % tn == 0 and w.shape[-2] == width and col0 % width == 0
    return pl.pallas_call(
        _norm_mm_body,
        grid=(m // tm, n // tn),
        in_specs=[pl.BlockSpec((tm, width), lambda i, j: (i, col0 // width)),
                  pl.BlockSpec((1, width), lambda i, j: (0, 0)),
                  _wspec(w, layer, (width, tn), lambda i, j: (0, j))],
        out_specs=pl.BlockSpec((tm, tn), lambda i, j: (i, j)),
        out_shape=jax.ShapeDtypeStruct((m, n), BF16),
        scratch_shapes=[pltpu.VMEM((tm, width), BF16)],
        compiler_params=_params("parallel", "arbitrary"),
        name="norm_matmul",
    )(x, gain.reshape(1, width).astype(F32), w)


def _out_proj_body(a1_ref, a2_ref, a3_ref, w_ref, r_ref, o_ref, wb_ref, *, seq_total, tm):
    i = pl.program_id(1)

    @pl.when(i == 0)
    def _():
        wb_ref[...] = w_ref[...].astype(BF16)

    k1 = a1_ref.shape[1]
    k2 = k1 + a2_ref.shape[1]
    acc = jnp.dot(a1_ref[...], wb_ref[0:k1, :], preferred_element_type=F32)
    acc += jnp.dot(a2_ref[...], wb_ref[k1:k2, :], preferred_element_type=F32)
    acc += jnp.dot(a3_ref[...], wb_ref[k2:, :], preferred_element_type=F32)
    pos = _batch_row0(i, seq_total, tm) + lax.broadcasted_iota(jnp.int32, acc.shape, 0)
    o_ref[...] = r_ref[...] + jnp.where(pos >= N_PAD, acc, 0.0)


def out_proj(a1, a2, a3, w, res, *, seq_total, tm, tn, layer=None):
    m = a1.shape[0]
    kdim, n = w.shape[-2:]
    assert a1.shape[1] + a2.shape[1] + a3.shape[1] == kdim
    assert m % tm == 0 and n % tn == 0 and seq_total % tm == 0
    a_spec = lambda a: pl.BlockSpec((tm, a.shape[1]), lambda j, i: (i, 0))
    return pl.pallas_call(
        functools.partial(_out_proj_body, seq_total=seq_total, tm=tm),
        grid=(n // tn, m // tm),
        in_specs=[a_spec(a1), a_spec(a2), a_spec(a3),
                  _wspec(w, layer, (kdim, tn), lambda j, i: (0, j), pipeline_mode=pl.Buffered(1)),
                  pl.BlockSpec((tm, tn), lambda j, i: (i, j))],
        out_specs=pl.BlockSpec((tm, tn), lambda j, i: (i, j)),
        out_shape=jax.ShapeDtypeStruct((m, n), F32),
        scratch_shapes=[pltpu.VMEM((kdim, tn), BF16)],
        compiler_params=_params("parallel", "arbitrary"),
        name="out_proj",
    )(a1, a2, a3, w, res)


CONV_CHUNK = 32
UP_ROW_CHUNK = 128
UP_K_CHUNK = 1024


def _ffn_up_body(a_ref, wg_ref, wv_ref, cwg_ref, cwv_ref, cbg_ref, cbv_ref, o_ref,
                 wcat_ref, u0_ref, u1_ref, halo_ref, *, seq_total, tm, tc, nj, nsteps):
    s = pl.program_id(0)
    cur = jnp.minimum(s, nsteps - 1)
    prev = jnp.maximum(s - 1, 0)
    row0_cur = _batch_row0(cur // nj, seq_total, tm)
    row0_prev = _batch_row0(prev // nj, seq_total, tm)
    j_prev = prev % nj

    @pl.when(s == 0)
    def _():
        u1_ref[...] = jnp.zeros_like(u1_ref)
        halo_ref[...] = jnp.zeros_like(halo_ref)

    mc = _pick(tm, UP_ROW_CHUNK, CONV_CHUNK)
    kc = _pick(a_ref.shape[1], UP_K_CHUNK, LANES)
    n_dots = (tm // mc) * (a_ref.shape[1] // kc)
    gate_rows = list(range(0, tm, CONV_CHUNK))
    per_dot = -(-len(gate_rows) // n_dots)

    def step(uc, up):
        up[0:SUBLANES, :] = jnp.where(row0_prev > 0, halo_ref[j_prev], 0.0)
        halo_ref[j_prev] = up[tm:tm + SUBLANES, :]
        wcat_ref[:, 0:tc] = wg_ref[...].astype(BF16)
        wcat_ref[:, tc:2 * tc] = wv_ref[...].astype(BF16)
        cw = jnp.concatenate([cwg_ref[...], cwv_ref[...]], axis=1)
        cb = jnp.concatenate([cbg_ref[...], cbv_ref[...]], axis=1)

        def gate_stage(r0):
            cv = cb
            for tap in range(CONV_W):
                lo = SUBLANES + r0 - (CONV_W - 1 - tap)
                cv = cv + cw[tap:tap + 1, :] * up[lo:lo + CONV_CHUNK, :]
            o_ref[r0:r0 + CONV_CHUNK, :] = (_silu(cv[:, 0:tc]) * cv[:, tc:2 * tc]).astype(o_ref.dtype)

        todo = list(gate_rows)
        nk = a_ref.shape[1] // kc
        for m0 in range(0, tm, mc):
            rows = slice(SUBLANES + m0, SUBLANES + m0 + mc)
            for ki in range(nk):
                part = jnp.dot(a_ref[m0:m0 + mc, ki * kc:(ki + 1) * kc], wcat_ref[ki * kc:(ki + 1) * kc, :],
                               preferred_element_type=F32)
                if ki > 0:
                    part = uc[rows, :] + part
                if ki == nk - 1:
                    pos = row0_cur + m0 + lax.broadcasted_iota(jnp.int32, part.shape, 0)
                    part = jnp.where(pos >= N_PAD, part, 0.0)
                uc[rows, :] = part
                for r0 in todo[:per_dot]:
                    gate_stage(r0)
                todo = todo[per_dot:]
        assert not todo

    @pl.when(s % 2 == 0)
    def _():
        step(u0_ref, u1_ref)

    @pl.when(s % 2 == 1)
    def _():
        step(u1_ref, u0_ref)


def ffn_up(hn, w_up, conv_w, conv_b, *, seq_total, tm, tc, d_ff, layer=None):
    m, kdim = hn.shape
    assert m % tm == 0 and seq_total % tm == 0 and tm % CONV_CHUNK == 0 and d_ff % tc == 0
    nj = d_ff // tc
    nsteps = (m // tm) * nj
    cur = lambda s: jnp.minimum(s, nsteps - 1)
    prev = lambda s: jnp.maximum(s - 1, 0)
    wspec = lambda off: _wspec(w_up, layer, (kdim, tc), lambda s: (0, cur(s) % nj + off))
    cspec = lambda c, r, off: _wspec(c, layer, (r, tc), lambda s: (0, prev(s) % nj + off))
    conv_b = conv_b.reshape(conv_b.shape[:-1] + (1, conv_b.shape[-1]))
    return pl.pallas_call(
        functools.partial(_ffn_up_body, seq_total=seq_total, tm=tm, tc=tc, nj=nj, nsteps=nsteps),
        grid=(nsteps + 1,),
        in_specs=[pl.BlockSpec((tm, kdim), lambda s: (cur(s) // nj, 0), pipeline_mode=pl.Buffered(1)),
                  wspec(0), wspec(nj), cspec(conv_w, CONV_W, 0), cspec(conv_w, CONV_W, nj),
                  cspec(conv_b, 1, 0), cspec(conv_b, 1, nj)],
        out_specs=pl.BlockSpec((tm, tc), lambda s: (prev(s) // nj, prev(s) % nj)),
        out_shape=jax.ShapeDtypeStruct((m, d_ff), BF16),
        scratch_shapes=[pltpu.VMEM((kdim, 2 * tc), BF16),
                        pltpu.VMEM((tm + SUBLANES, 2 * tc), F32),
                        pltpu.VMEM((tm + SUBLANES, 2 * tc), F32),
                        pltpu.VMEM((nj, SUBLANES, 2 * tc), F32)],
        compiler_params=_params("arbitrary"),
        name="ffn_up",
    )(hn, w_up, w_up, conv_w, conv_w, conv_b, conv_b)


def _ffn_down_body(a_ref, w_ref, r_ref, o_ref, *, seq_total, tm, tk, d_ff, nk):
    k = pl.program_id(2)

    def part(a, w):
        return jnp.dot(a, w.astype(BF16), preferred_element_type=F32)

    @pl.when(k == 0)
    def _():
        o_ref[...] = part(a_ref[...], w_ref[...])

    @pl.when(jnp.logical_and(k > 0, k < nk - 1))
    def _():
        o_ref[...] += part(a_ref[...], w_ref[...])

    @pl.when(k == nk - 1)
    def _():
        acol = k * tk + lax.broadcasted_iota(jnp.int32, a_ref.shape, 1)
        a = jnp.where(acol < d_ff, a_ref[...], jnp.zeros(a_ref.shape, a_ref.dtype))
        wrow = k * tk + lax.broadcasted_iota(jnp.int32, w_ref.shape, 0)
        w = jnp.where(wrow < d_ff, w_ref[...], 0.0)
        pos = _batch_row0(pl.program_id(0), seq_total, tm) + lax.broadcasted_iota(
            jnp.int32, o_ref.shape, 0)
        o_ref[...] = r_ref[...] + jnp.where(pos >= N_PAD, o_ref[...] + part(a, w), 0.0)


def ffn_down(act, w, res, *, seq_total, tm, tn, tk, layer=None):
    m, d_ff = act.shape
    n = w.shape[-1]
    assert m % tm == 0 and n % tn == 0 and seq_total % tm == 0 and w.shape[-2] == d_ff
    nk = pl.cdiv(d_ff, tk)
    assert nk >= 2
    return pl.pallas_call(
        functools.partial(_ffn_down_body, seq_total=seq_total, tm=tm, tk=tk, d_ff=d_ff, nk=nk),
        grid=(m // tm, n // tn, nk),
        in_specs=[pl.BlockSpec((tm, tk), lambda i, j, k: (i, k)),
                  _wspec(w, layer, (tk, tn), lambda i, j, k: (k, j)),
                  pl.BlockSpec((tm, tn), lambda i, j, k: (i, j))],
        out_specs=pl.BlockSpec((tm, tn), lambda i, j, k: (i, j)),
        out_shape=jax.ShapeDtypeStruct((m, n), F32),
        compiler_params=_params("parallel", "parallel", "arbitrary"),
        name="ffn_down",
    )(act, w, res)


def _scores(k_rows, q_parts):
    out = []
    for q in q_parts:
        s = jnp.dot(k_rows, q, preferred_element_type=F32)
        out.append((s, jnp.max(s, axis=0, keepdims=True)))
    return tuple(out)


def _causal(scored, col0s):
    out = []
    for (s, _), c0 in zip(scored, col0s):
        key = lax.broadcasted_iota(jnp.int32, s.shape, 0)
        qry = c0 + lax.broadcasted_iota(jnp.int32, s.shape, 1)
        s = jnp.where(key <= qry, s, NEG_INF)
        out.append((s, jnp.max(s, axis=0, keepdims=True)))
    return tuple(out)


def _softmax_pv(carry, scored, v_ext):
    out = []
    for (m, acc), (s, s_max) in zip(carry, scored):
        m_new = jnp.maximum(m, s_max)
        alpha = jnp.exp2(m - m_new)
        p = jnp.exp2(s - m_new).astype(BF16)
        out.append((m_new, alpha * acc + jnp.dot(v_ext, p, preferred_element_type=F32)))
    return tuple(out)


def _attn_init(widths):
    return tuple((jnp.full((1, w), NEG_INF, F32), jnp.zeros((V_ROWS, w), F32)) for w in widths)


def _attn_finish(carry, gain, o_ref, row0, col0s):
    for (_, acc), c0 in zip(carry, col0s):
        o = jnp.transpose(acc[0:LANES, :] / acc[LANES:LANES + 1, :])
        o = o * lax.rsqrt(jnp.mean(o * o, axis=-1, keepdims=True) + RMS_EPS) * gain
        o_ref[pl.ds(row0 + c0, o.shape[0]), :] = o.astype(o_ref.dtype)


def _query_parts(tb):
    part = QUERY_PART if tb % QUERY_PART == 0 else tb
    return part, tuple(range(0, tb, part))


def _attention_core(qt, qt0, ks, vt, vt0, s_scr, m_scr, g_ref, o_ref, *, nblk, tb):
    heads = range(qt.shape[0])
    gains = [g_ref[:, h * LANES:(h + 1) * LANES] for h in heads]
    outs = [o_ref.at[:, pl.ds(h * LANES, LANES)] for h in heads]

    for h in heads:
        lead = _causal(_scores(ks[h, 0:LEAD, :], (qt0[h],)), (0,))
        _attn_finish(_softmax_pv(_attn_init((LEAD,)), lead, vt0[h]), gains[h], outs[h], 0, (0,))

    part, col0s = _query_parts(tb)
    widths = (part,) * len(col0s)

    def q_block(qi, _):
        q_parts = [tuple(qt[h, qi, :, c0:c0 + part] for c0 in col0s) for h in heads]

        def score(slot, kj):
            for h in heads:
                k_rows = ks[h, pl.ds(pl.multiple_of(LEAD + kj * tb, LANES), tb), :]
                for p, (s, s_max) in enumerate(_scores(k_rows, q_parts[h])):
                    s_scr[h, slot, p] = s
                    m_scr[h, slot, p] = s_max

        def scored(h, slot):
            return tuple((s_scr[h, slot, p], m_scr[h, slot, p]) for p in range(len(col0s)))

        def consume(carry, slot, kj, causal):
            out = []
            for h in heads:
                sc = scored(h, slot)
                if causal:
                    sc = _causal(sc, col0s)
                out.append(_softmax_pv(carry[h], sc, vt[h, kj]))
            return tuple(out)

        carry = tuple(_softmax_pv(_attn_init(widths), _scores(ks[h, 0:LEAD, :], q_parts[h]), vt0[h])
                      for h in heads)
        score(0, 0)

        def pair(t, carry):
            kj = 2 * t
            score(1, kj + 1)
            carry = consume(carry, 0, kj, False)
            score(0, kj + 2)
            return consume(carry, 1, kj + 1, False)

        carry = lax.fori_loop(0, qi // 2, pair, carry)

        def odd_tail(carry):
            score(1, qi)
            return consume(consume(carry, 0, qi - 1, False), 1, qi, True)

        def even_tail(carry):
            return consume(carry, 0, qi, True)

        carry = lax.cond(qi % 2 == 1, odd_tail, even_tail, carry)
        for h in heads:
            _attn_finish(carry[h], gains[h], outs[h], pl.multiple_of(LEAD + qi * tb, LANES), col0s)
        return 0

    lax.fori_loop(0, nblk, q_block, 0)


def _store_v_ext(vt_dst, v_rows):
    n = v_rows.shape[0]
    vt_dst[0:LANES, :] = _to_t(v_rows)
    r = lax.broadcasted_iota(jnp.int32, (V_ROWS - LANES, n), 0)
    vt_dst[LANES:V_ROWS, :] = jnp.where(r == 0, 1.0, 0.0).astype(BF16)


def _to_t(x):
    return jnp.transpose(x.astype(F32)).astype(BF16)


def _mla_attn_body(q_ref, kv_ref, kr_ref, cos_ref, sa_ref, sb_ref, g_ref, o_ref,
                   qt, qt0, ks, vt, vt0, s_scr, m_scr, *, nblk, tb):
    scale = (MLA_NOPE + MLA_ROPE) ** -0.5 * LOG2E
    heads = range(qt.shape[0])

    def rope(x, rows):
        return (x * cos_ref[rows, :] + pltpu.roll(x, HALF_ROPE, 1) * sa_ref[rows, :]
                + pltpu.roll(x, LANES - HALF_ROPE, 1) * sb_ref[rows, :])

    def prep(r0, n, qt_dst, vt_dst):
        rows = pl.ds(r0, n)
        one = lax.broadcasted_iota(jnp.int32, (n, LANES), 1) == ONE_LANE
        pos = r0 + lax.broadcasted_iota(jnp.int32, (n, LANES), 0)
        kp = rope(kr_ref[rows, :].astype(F32), rows)
        kp = jnp.where(one, jnp.where(pos < N_PAD, NEG_INF, 0.0), kp).astype(BF16)
        for h in heads:
            c0 = h * 2 * LANES
            qn = q_ref[rows, pl.ds(c0, LANES)].astype(F32) * scale
            qp = rope(q_ref[rows, pl.ds(c0 + LANES, LANES)].astype(F32), rows) * scale
            qp = jnp.where(one, 1.0, qp)
            qt_dst(h)[0:LANES, :] = jnp.transpose(qn).astype(BF16)
            qt_dst(h)[LANES:2 * LANES, :] = jnp.transpose(qp).astype(BF16)
            ks[h, rows, pl.ds(0, LANES)] = kv_ref[rows, pl.ds(c0, LANES)]
            ks[h, rows, pl.ds(LANES, LANES)] = kp
            _store_v_ext(vt_dst(h), kv_ref[rows, pl.ds(c0 + LANES, LANES)])

    prep(0, LEAD, lambda h: qt0.at[h], lambda h: vt0.at[h])

    def prep_block(i, _):
        prep(pl.multiple_of(LEAD + i * tb, LANES), tb, lambda h: qt.at[h, i], lambda h: vt.at[h, i])
        return 0

    lax.fori_loop(0, nblk, prep_block, 0)
    _attention_core(qt, qt0, ks, vt, vt0, s_scr, m_scr, g_ref, o_ref, nblk=nblk, tb=tb)


def _attn_scratch(heads, seq_total, nblk, tb):
    part, col0s = _query_parts(tb)
    return [pltpu.VMEM((heads, nblk, 2 * LANES, tb), BF16), pltpu.VMEM((heads, 2 * LANES, LEAD), BF16),
            pltpu.VMEM((heads, seq_total, 2 * LANES), BF16),
            pltpu.VMEM((heads, nblk, V_ROWS, tb), BF16), pltpu.VMEM((heads, V_ROWS, LEAD), BF16),
            pltpu.VMEM((heads, 2, len(col0s), tb, part), F32),
            pltpu.VMEM((heads, 2, len(col0s), 1, part), F32)]


def mla_attention(q_raw, kv, proj, cos_t, sin_a, sin_b, gain, *, batch, seq_total, tb):
    t = q_raw.shape[0]
    nblk = (seq_total - LEAD) // tb
    hps = ATTN_HEADS_PER_STEP
    assert MLA_HEADS % hps == 0
    big = lambda w: pl.BlockSpec((seq_total, hps * w), lambda b, h: (b, h))
    table = pl.BlockSpec((seq_total, LANES), lambda b, h: (0, 0), pipeline_mode=pl.Buffered(1))
    return pl.pallas_call(
        functools.partial(_mla_attn_body, nblk=nblk, tb=tb),
        grid=(batch, MLA_HEADS // hps),
        in_specs=[big(2 * LANES), big(2 * LANES),
                  pl.BlockSpec((seq_total, LANES), lambda b, h: (b, COL_KR // LANES)),
                  table, table, table,
                  pl.BlockSpec((1, hps * LANES), lambda b, h: (0, h))],
        out_specs=big(LANES),
        out_shape=jax.ShapeDtypeStruct((t, W_MLA), BF16),
        scratch_shapes=_attn_scratch(hps, seq_total, nblk, tb),
        compiler_params=_params("parallel", "parallel"),
        name="mla_attention",
    )(q_raw, kv, proj, cos_t, sin_a, sin_b, gain.reshape(1, W_MLA).astype(F32))


def _fox_attn_body(q_ref, k_ref, v_ref, aq_ref, ak_ref, g_ref, o_ref,
                   qt, qt0, ks, vt, vt0, s_scr, m_scr, *, nblk, tb):
    scale = FOX_DH ** -0.5 * LOG2E
    heads = range(qt.shape[0])

    def prep(r0, n, qt_dst, vt_dst):
        rows = pl.ds(r0, n)
        for h in heads:
            cols = pl.ds(h * LANES, LANES)
            qt_dst(h)[0:LANES, :] = jnp.transpose(q_ref[rows, cols].astype(F32) * scale).astype(BF16)
            qt_dst(h)[LANES:2 * LANES, :] = _to_t(aq_ref[rows, cols])
            ks[h, rows, pl.ds(0, LANES)] = k_ref[rows, cols]
            ks[h, rows, pl.ds(LANES, LANES)] = ak_ref[rows, cols]
            _store_v_ext(vt_dst(h), v_ref[rows, cols])

    prep(0, LEAD, lambda h: qt0.at[h], lambda h: vt0.at[h])

    def prep_block(i, _):
        prep(pl.multiple_of(LEAD + i * tb, LANES), tb, lambda h: qt.at[h, i], lambda h: vt.at[h, i])
        return 0

    lax.fori_loop(0, nblk, prep_block, 0)
    _attention_core(qt, qt0, ks, vt, vt0, s_scr, m_scr, g_ref, o_ref, nblk=nblk, tb=tb)


def fox_attention(proj, aug_q, aug_k, gain, *, batch, seq_total, tb):
    t = proj.shape[0]
    nblk = (seq_total - LEAD) // tb
    hps = ATTN_HEADS_PER_STEP
    assert FOX_HEADS % hps == 0
    wb = hps * LANES
    col = lambda c0: pl.BlockSpec((seq_total, wb), lambda b, h: (b, c0 // wb + h))
    assert COL_FQ % wb == 0 and COL_FK % wb == 0 and COL_FV % wb == 0
    return pl.pallas_call(
        functools.partial(_fox_attn_body, nblk=nblk, tb=tb),
        grid=(batch, FOX_HEADS // hps),
        in_specs=[col(COL_FQ), col(COL_FK), col(COL_FV), col(0), col(0),
                  pl.BlockSpec((1, wb), lambda b, h: (0, h))],
        out_specs=col(0),
        out_shape=jax.ShapeDtypeStruct((t, W_FOX), BF16),
        scratch_shapes=_attn_scratch(hps, seq_total, nblk, tb),
        compiler_params=_params("parallel", "parallel"),
        name="fox_attention",
    )(proj, proj, proj, aug_q, aug_k, gain.reshape(1, W_FOX).astype(F32))


def _split3(x):
    hi = x.astype(BF16)
    r1 = x - hi.astype(F32)
    mid = r1.astype(BF16)
    lo = (r1 - mid.astype(F32)).astype(BF16)
    return hi.astype(F32), mid.astype(F32), lo.astype(F32)


def _fox_prep_body(z_ref, b_ref, aq_ref, ak_ref, carry_ref, *, tr):
    i = pl.program_id(1)

    @pl.when(i == 0)
    def _():
        carry_ref[...] = jnp.zeros_like(carry_ref)

    row = i * tr + lax.broadcasted_iota(jnp.int32, (tr, LANES), 0)
    valid = row >= N_PAD
    log_f = jnp.where(valid, _log_sigmoid(z_ref[...] + b_ref[...]), 0.0)
    r = lax.broadcasted_iota(jnp.int32, (tr, tr), 0)
    c = lax.broadcasted_iota(jnp.int32, (tr, tr), 1)
    tri = jnp.where(c <= r, 1.0, 0.0).astype(F32)
    csum = jnp.dot(tri, log_f, preferred_element_type=F32,
                   precision=lax.Precision.HIGHEST) + carry_ref[0:1, :]
    carry_ref[...] = jnp.broadcast_to(csum[tr - 1:tr, :], carry_ref.shape)

    lane = lax.broadcasted_iota(jnp.int32, (tr, LANES), 1)
    key_mask = jnp.where(valid, 0.0, NEG_INF)
    for h in range(FOX_HEADS):
        col = csum[:, GATE_FZ + h:GATE_FZ + h + 1] * LOG2E
        hi, mid, lo = _split3(col)
        aq = jnp.where(lane == 0, hi, jnp.where(lane == 1, mid, jnp.where(lane == 2, lo,
             jnp.where(lane < 7, 1.0, 0.0))))
        ak = jnp.where(lane < 3, 1.0, jnp.where(lane == 3, -hi, jnp.where(lane == 4, -mid,
             jnp.where(lane == 5, -lo, jnp.where(lane == 6, key_mask, 0.0)))))
        aq_ref[:, h * LANES:(h + 1) * LANES] = aq.astype(BF16)
        ak_ref[:, h * LANES:(h + 1) * LANES] = ak.astype(BF16)


def fox_prep(gates, fox_b, *, batch, seq_total):
    t = gates.shape[0]
    tr = _pick(seq_total, 384, LANES)
    nblk = seq_total // tr
    bias = jnp.zeros((1, LANES), F32).at[0, GATE_FZ:GATE_FZ + FOX_HEADS].set(fox_b.astype(F32))
    out = jax.ShapeDtypeStruct((t, W_FOX), BF16)
    return pl.pallas_call(
        functools.partial(_fox_prep_body, tr=tr),
        grid=(batch, nblk),
        in_specs=[pl.BlockSpec((tr, LANES), lambda b, i: (b * nblk + i, 0)),
                  pl.BlockSpec((1, LANES), lambda b, i: (0, 0))],
        out_specs=[pl.BlockSpec((tr, W_FOX), lambda b, i: (b * nblk + i, 0))] * 2,
        out_shape=[out, out],
        scratch_shapes=[pltpu.VMEM((SUBLANES, LANES), F32)],
        compiler_params=_params("parallel", "arbitrary"),
        name="fox_prep",
    )(gates, bias)


def _gla_body(q_ref, k_ref, v_ref, r_ref, z_ref, w2_ref, b2_ref, g_ref, o_ref, s_ref,
              *, seq_total):
    c = GLA_CHUNK
    nchunk = seq_total // c
    gsz = max(g for g in range(1, GLA_GROUP + 1) if nchunk % g == 0)
    rg = gsz * c
    heads = s_ref.shape[0]
    nt = (((1,), (1,)), ((), ()))
    tn = (((0,), (0,)), ((), ()))
    r_i = lax.broadcasted_iota(jnp.int32, (rg, rg), 0)
    c_i = lax.broadcasted_iota(jnp.int32, (rg, rg), 1)
    assert c & (c - 1) == 0
    lower = jnp.logical_and(c_i <= r_i, c_i >= jnp.bitwise_and(r_i, -c))
    tri = jnp.where(lower, 1.0, 0.0).astype(F32)
    s_ref[...] = jnp.zeros_like(s_ref)

    def group(i, _):
        r0 = pl.multiple_of(i * rg, rg)
        rows = pl.ds(r0, rg)
        valid = (r0 + lax.broadcasted_iota(jnp.int32, (rg, GLA_DK), 0)) >= N_PAD
        z = z_ref[rows, :].astype(BF16)
        for hh in range(heads):
            kcols = pl.ds(hh * GLA_DK, GLA_DK)
            vcols = pl.ds(hh * GLA_DV, GLA_DV)
            q = q_ref[rows, kcols].astype(F32) * (GLA_DK ** -0.5)
            k = jnp.where(valid, k_ref[rows, kcols].astype(F32), 0.0)
            v = v_ref[rows, vcols]
            logit = jnp.dot(z, w2_ref[:, kcols], preferred_element_type=F32) + b2_ref[:, kcols]
            log_a = jnp.where(valid, _log_sigmoid(logit) / GLA_TAU, 0.0)
            bc = jnp.dot(tri, log_a, preferred_element_type=F32, precision=lax.Precision.HIGHEST)
            b_last = jnp.concatenate(
                [jnp.broadcast_to(bc[(n + 1) * c - 1:(n + 1) * c, :], (c, GLA_DK)) for n in range(gsz)], axis=0)
            q_dec = (q * jnp.exp(bc)).astype(BF16)
            k_dec = (k * jnp.exp(-bc)).astype(BF16)
            k_state = (k * jnp.exp(b_last - bc)).astype(BF16)
            a = lax.dot_general(q_dec, k_dec, nt, preferred_element_type=F32)
            a = jnp.where(lower, a, 0.0).astype(BF16)
            o_intra = jnp.dot(a, v, preferred_element_type=F32)
            state = s_ref[hh]
            outs = []
            for n in range(gsz):
                sl = slice(n * c, (n + 1) * c)
                outs.append(o_intra[sl] + jnp.dot(q_dec[sl], state.astype(BF16),
                                                   preferred_element_type=F32))
                decay = jnp.exp(b_last[n * c:n * c + 1, :])
                dec_t = jnp.transpose(jnp.broadcast_to(decay, (GLA_DK, GLA_DK)))
                dec_t = jnp.concatenate([dec_t, dec_t], axis=1)
                state = state * dec_t + lax.dot_general(k_state[sl], v[sl], tn,
                                                        preferred_element_type=F32)
            s_ref[hh] = state
            o = jnp.concatenate(outs, axis=0)
            o = o * lax.rsqrt(jnp.mean(o * o, axis=-1, keepdims=True) + RMS_EPS) * g_ref[:, vcols]
            o_ref[rows, vcols] = (o * _silu(r_ref[rows, vcols].astype(F32))).astype(o_ref.dtype)
        return 0

    lax.fori_loop(0, nchunk // gsz, group, 0)


def gla(proj, gates, w2, b2, gain, *, batch, seq_total):
    t = proj.shape[0]
    hps = GLA_HEADS_PER_STEP
    assert GLA_HEADS % hps == 0
    wk, wv = hps * GLA_DK, hps * GLA_DV
    blk = lambda w, c0: pl.BlockSpec((seq_total, w), lambda b, h: (b, c0 // w + h))
    return pl.pallas_call(
        functools.partial(_gla_body, seq_total=seq_total),
        grid=(batch, GLA_HEADS // hps),
        in_specs=[blk(wk, COL_GQ), blk(wk, COL_GK), blk(wv, COL_GV), blk(wv, COL_GR),
                  pl.BlockSpec((seq_total, LANES), lambda b, h: (b, 0)),
                  pl.BlockSpec((LANES, wk), lambda b, h: (0, h)),
                  pl.BlockSpec((1, wk), lambda b, h: (0, h)),
                  pl.BlockSpec((1, wv), lambda b, h: (0, h))],
        out_specs=pl.BlockSpec((seq_total, wv), lambda b, h: (b, h)),
        out_shape=jax.ShapeDtypeStruct((t, W_GLA), BF16),
        scratch_shapes=[pltpu.VMEM((hps, GLA_DK, GLA_DV), F32)],
        compiler_params=_params("parallel", "parallel"),
        name="gla",
    )(proj, proj, proj, proj, gates, w2, b2, gain.reshape(1, W_GLA).astype(F32))


def _rope_tables(seq_total):
    pos = np.maximum(np.arange(seq_total) - N_PAD, 0).astype(np.float32)
    inv_freq = (1.0 / (ROPE_THETA ** (np.arange(0, MLA_ROPE, 2, dtype=np.float32) / MLA_ROPE))).astype(np.float32)
    ang = jnp.asarray(pos)[:, None] * jnp.asarray(inv_freq)[None, :]
    cos, sin = jnp.cos(ang), jnp.sin(ang)
    z = jnp.zeros_like(cos)
    cos_t = jnp.concatenate([cos, cos, z, z], axis=1)
    sin_a = jnp.concatenate([z, sin, z, z], axis=1)
    sin_b = jnp.concatenate([-sin, z, z, z], axis=1)
    return cos_t, sin_a, sin_b


def kernel(x, meta_tokens, attn_norm, w_in, mla_q_norm, mla_w_uq, mla_kv_norm, mla_w_ukv,
           gla_w_gate2, gla_b_gate, fox_b_f, out_norm_mla, out_norm_gla, out_norm_fox,
           w_out, ffn_norm, ffn_w_up, ffn_conv_w, ffn_conv_b, ffn_w_down, final_norm):
    batch, seq, d_model = x.shape
    depth = w_in.shape[0]
    d_ff = ffn_w_down.shape[1]
    seq_total = LEAD + seq
    cos_t, sin_a, sin_b = _rope_tables(seq_total)

    tm_big = _pick(seq_total, TM_BIG, LANES)
    tm_mid = _pick(seq_total, TM_MID, LANES)
    tb = _pick(seq, ATTN_BLOCK, LANES)
    tc = _pick(d_ff, 256, LANES)

    w_in_b = transpose_cast(jnp.transpose(w_in, (2, 0, 1)))

    h, hn = embed_norm(x, meta_tokens, attn_norm[0])
    for layer in range(depth):
        w_big, w_gates = relayout_cols(w_in_b, (W_IN_PLAN, W_GATE_PLAN), (N_PROJ, LANES), layer=layer)
        w_uq, = relayout_cols(mla_w_uq, (W_UQ_PLAN,), (MLA_HEADS * 2 * LANES,), layer=layer)
        if layer > 0:
            hn = rmsnorm(h, attn_norm[layer])
        proj = matmul(hn, w_big, tm=tm_big, tn=PROJ_TN)
        gates = matmul(hn, w_gates, tm=tm_big, tn=LANES, out_dtype=F32)

        q_raw = norm_matmul(proj, mla_q_norm[layer], w_uq, col0=COL_CQ, width=MLA_Q_LORA,
                            tm=tm_big, tn=1024)
        kv = norm_matmul(proj, mla_kv_norm[layer], mla_w_ukv, col0=COL_CKV, width=MLA_KV_LORA,
                         tm=tm_big, tn=1024, layer=layer)
        o_mla = mla_attention(q_raw, kv, proj, cos_t, sin_a, sin_b, out_norm_mla[layer],
                              batch=batch, seq_total=seq_total, tb=tb)

        w2 = jnp.zeros((LANES, W_GQK), F32).at[:GLA_GATE_RANK].set(gla_w_gate2[layer])
        o_gla = gla(proj, gates, w2.astype(BF16), gla_b_gate[layer].reshape(1, -1),
                    out_norm_gla[layer], batch=batch, seq_total=seq_total)

        aug_q, aug_k = fox_prep(gates, fox_b_f[layer], batch=batch, seq_total=seq_total)
        o_fox = fox_attention(proj, aug_q, aug_k, out_norm_fox[layer],
                              batch=batch, seq_total=seq_total, tb=tb)

        h = out_proj(o_mla, o_gla, o_fox, w_out, h, seq_total=seq_total, tm=tm_mid, tn=1024,
                     layer=layer)

        hn = rmsnorm(h, ffn_norm[layer])
        act = ffn_up(hn, ffn_w_up, ffn_conv_w, ffn_conv_b, seq_total=seq_total, tm=tm_big,
                     tc=tc, d_ff=d_ff, layer=layer)
        h = ffn_down(act, ffn_w_down, h, seq_total=seq_total, tm=tm_big, tn=1024, tk=DOWN_TK,
                     layer=layer)

    return final_rmsnorm(h, final_norm, batch, seq_total)
```

```python
import functools

import jax
import jax.numpy as jnp
import numpy as np
from jax import lax
from jax.experimental import pallas as pl
from jax.experimental.pallas import tpu as pltpu

F32 = jnp.float32
BF16 = jnp.bfloat16

N_META = 16
LEAD = 128
N_PAD = LEAD - N_META
RMS_EPS = 1e-6
NEG_INF = -1e30

MLA_V = 128
MLA_HEADS = 12
MLA_NOPE = 128
MLA_ROPE = 64
MLA_Q_LORA = 1536
MLA_KV_LORA = 512
ROPE_THETA = 10000.0

GLA_DV = 256
GLA_DK = 128
GLA_HEADS = 4
GLA_GATE_RANK = 16
GLA_TAU = 16.0
GLA_CHUNK = 64

FOX_DH = 128
FOX_HEADS = 12

W_MLA = MLA_HEADS * MLA_V
W_GLA = GLA_HEADS * GLA_DV
W_FOX = FOX_HEADS * FOX_DH
CONV_W = 3

LANES = 128
SUBLANES = 8
VMEM_LIMIT_BYTES = 56 * 1024 * 1024

TM_BIG = 1408
TM_MID = 704
ATTN_BLOCK = 512
DOWN_TK = 1024
GLA_HEADS_PER_STEP = 2
GLA_GROUP = 6
QUERY_PART = 256
ATTN_HEADS_PER_STEP = 2

HALF_ROPE = MLA_ROPE // 2
W_GQK = GLA_HEADS * GLA_DK
COL_CQ = 0
COL_CKV = COL_CQ + MLA_Q_LORA
COL_GV = COL_CKV + MLA_KV_LORA
COL_GR = COL_GV + W_GLA
COL_GQ = COL_GR + W_GLA
COL_GK = COL_GQ + W_GQK
COL_FQ = COL_GK + W_GQK
COL_FK = COL_FQ + W_FOX
COL_FV = COL_FK + W_FOX
COL_KR = COL_FV + W_FOX
MXU_COLS = 256
PROJ_TN = 3 * MXU_COLS
N_PROJ = -(-(COL_KR + LANES) // PROJ_TN) * PROJ_TN
SRC_CQ = 0
SRC_CKV = SRC_CQ + MLA_Q_LORA
SRC_KR = SRC_CKV + MLA_KV_LORA
SRC_GQ = SRC_KR + MLA_ROPE
SRC_GK = SRC_GQ + W_GQK
SRC_GV = SRC_GK + W_GQK
SRC_GZ = SRC_GV + W_GLA
SRC_GR = SRC_GZ + GLA_GATE_RANK
SRC_FQ = SRC_GR + W_GLA
SRC_FK = SRC_FQ + W_FOX
SRC_FV = SRC_FK + W_FOX
SRC_FZ = SRC_FV + W_FOX
GATE_FZ = GLA_GATE_RANK
ONE_LANE = MLA_ROPE
LOG2E = 1.4426950408889634
V_ROWS = LANES + 16


def _pick(n, target, mult):
    best = None
    for d in range(mult, min(n, target) + 1, mult):
        if n % d == 0:
            best = d
    assert best is not None, (n, target, mult)
    return best


def _params(*sem):
    return pltpu.CompilerParams(dimension_semantics=sem, vmem_limit_bytes=VMEM_LIMIT_BYTES)


def _log_sigmoid(x):
    return jnp.minimum(x, 0.0) - jnp.log1p(jnp.exp(-jnp.abs(x)))


def _silu(x):
    return x / (1.0 + jnp.exp(-x))


def _wspec(w, layer, block, index, **kw):
    if layer is None:
        assert w.ndim == len(block)
        return pl.BlockSpec(block, index, **kw)
    assert w.ndim == len(block) + 1
    return pl.BlockSpec((None,) + tuple(block), lambda *g: (layer,) + tuple(index(*g)), **kw)


def _batch_row0(step, seq_total, tm):
    return (step % (seq_total // tm)) * tm


def _relayout_body(x_ref, *o_refs, plans):
    for o_ref, plan in zip(o_refs, plans):
        width = o_ref.shape[1]
        pos = 0
        for src, dst, w in plan:
            assert dst >= pos
            if dst > pos:
                o_ref[:, pos:dst] = jnp.zeros((o_ref.shape[0], dst - pos), o_ref.dtype)
            o_ref[:, dst:dst + w] = x_ref[:, src:src + w].astype(o_ref.dtype)
            pos = dst + w
        if pos < width:
            o_ref[:, pos:width] = jnp.zeros((o_ref.shape[0], width - pos), o_ref.dtype)


def relayout_cols(w, plans, widths, *, layer=None, rows=256):
    k, n = w.shape[-2:]
    tr = _pick(k, rows, 16)
    return pl.pallas_call(
        functools.partial(_relayout_body, plans=plans),
        grid=(k // tr,),
        in_specs=[_wspec(w, layer, (tr, n), lambda i: (i, 0))],
        out_specs=[pl.BlockSpec((tr, wd), lambda i: (i, 0)) for wd in widths],
        out_shape=[jax.ShapeDtypeStruct((k, wd), BF16) for wd in widths],
        compiler_params=_params("parallel"),
        name="relayout_cols",
    )(w)


def _transpose_cast_body(x_ref, o_ref, *, n):
    row = pl.program_id(0) * LANES + lax.broadcasted_iota(jnp.int32, (LANES, x_ref.shape[2]), 0)
    for layer in range(x_ref.shape[1]):
        x = jnp.where(row < n, x_ref[:, layer, :], 0.0)
        o_ref[layer] = jnp.transpose(x).astype(o_ref.dtype)


def transpose_cast(wt):
    n, depth, k = wt.shape
    nb = pl.cdiv(n, LANES)
    return pl.pallas_call(
        functools.partial(_transpose_cast_body, n=n),
        grid=(nb,),
        in_specs=[pl.BlockSpec((LANES, depth, k), lambda i: (i, 0, 0))],
        out_specs=pl.BlockSpec((depth, k, LANES), lambda i: (0, 0, i)),
        out_shape=jax.ShapeDtypeStruct((depth, k, nb * LANES), BF16),
        compiler_params=_params("parallel"),
        name="transpose_cast",
    )(wt)


W_IN_PLAN = sorted([
    (SRC_CQ, COL_CQ, MLA_Q_LORA), (SRC_CKV, COL_CKV, MLA_KV_LORA), (SRC_KR, COL_KR, MLA_ROPE),
    (SRC_GQ, COL_GQ, W_GQK), (SRC_GK, COL_GK, W_GQK), (SRC_GV, COL_GV, W_GLA),
    (SRC_GR, COL_GR, W_GLA), (SRC_FQ, COL_FQ, W_FOX), (SRC_FK, COL_FK, W_FOX),
    (SRC_FV, COL_FV, W_FOX)], key=lambda p: p[1])
W_GATE_PLAN = [(SRC_GZ, 0, GLA_GATE_RANK), (SRC_FZ, GATE_FZ, FOX_HEADS)]
W_UQ_PLAN = [(h * (MLA_NOPE + MLA_ROPE), h * 2 * LANES, MLA_NOPE + MLA_ROPE) for h in range(MLA_HEADS)]


def _rmsnorm_body(x_ref, g_ref, o_ref):
    x = x_ref[...].astype(F32)
    y = x * lax.rsqrt(jnp.mean(x * x, axis=-1, keepdims=True) + RMS_EPS)
    o_ref[...] = (y * g_ref[...]).astype(o_ref.dtype)


def rmsnorm(x, gain, *, out_dtype=BF16, rows=256):
    m, width = x.shape
    tr = _pick(m, rows, 16)
    return pl.pallas_call(
        _rmsnorm_body,
        grid=(m // tr,),
        in_specs=[pl.BlockSpec((tr, width), lambda i: (i, 0)),
                  pl.BlockSpec((1, width), lambda i: (0, 0))],
        out_specs=pl.BlockSpec((tr, width), lambda i: (i, 0)),
        out_shape=jax.ShapeDtypeStruct((m, width), out_dtype),
        compiler_params=_params("parallel"),
        name="rmsnorm",
    )(x, gain.reshape(1, width).astype(F32))


def _final_norm_body(x_ref, g_ref, o_ref):
    x = x_ref[...]
    y = x * lax.rsqrt(jnp.mean(x * x, axis=-1, keepdims=True) + RMS_EPS)
    o_ref[0] = y * g_ref[...]


def final_rmsnorm(h, gain, batch, seq_total):
    d = h.shape[1]
    tr = LEAD
    nblk = seq_total // tr
    return pl.pallas_call(
        _final_norm_body,
        grid=(batch, nblk - 1),
        in_specs=[pl.BlockSpec((tr, d), lambda b, i: (b * nblk + i + 1, 0)),
                  pl.BlockSpec((1, d), lambda b, i: (0, 0))],
        out_specs=pl.BlockSpec((1, tr, d), lambda b, i: (b, i, 0)),
        out_shape=jax.ShapeDtypeStruct((batch, seq_total - LEAD, d), F32),
        compiler_params=_params("parallel", "parallel"),
        name="final_norm",
    )(h, gain.reshape(1, d).astype(F32))


def _embed_norm_body(x_ref, meta_ref, g_ref, h_ref, hn_ref):
    lead = pl.program_id(1) == 0

    def emit(v):
        h_ref[...] = v
        hn = v * lax.rsqrt(jnp.mean(v * v, axis=-1, keepdims=True) + RMS_EPS) * g_ref[...]
        hn_ref[...] = hn.astype(hn_ref.dtype)

    @pl.when(lead)
    def _():
        d = h_ref.shape[1]
        emit(jnp.concatenate([jnp.zeros((N_PAD, d), F32), meta_ref[...]], axis=0))

    @pl.when(jnp.logical_not(lead))
    def _():
        emit(x_ref[0])


def embed_norm(x, meta_tokens, gain):
    batch, seq, d = x.shape
    assert seq % LEAD == 0
    nblk = seq // LEAD + 1
    t = batch * nblk * LEAD
    row = lambda b, i: (b * nblk + i, 0)
    return pl.pallas_call(
        _embed_norm_body,
        grid=(batch, nblk),
        in_specs=[pl.BlockSpec((1, LEAD, d), lambda b, i: (b, jnp.maximum(i - 1, 0), 0)),
                  pl.BlockSpec((N_META, d), lambda b, i: (0, 0)),
                  pl.BlockSpec((1, d), lambda b, i: (0, 0))],
        out_specs=[pl.BlockSpec((LEAD, d), row), pl.BlockSpec((LEAD, d), row)],
        out_shape=[jax.ShapeDtypeStruct((t, d), F32), jax.ShapeDtypeStruct((t, d), BF16)],
        compiler_params=_params("parallel", "arbitrary"),
        name="embed_norm",
    )(x.astype(F32), meta_tokens.astype(F32), gain.reshape(1, d).astype(F32))


def _proj_body(a_ref, w_ref, wg_ref, o_ref, g_ref):
    o_ref[...] = jnp.dot(a_ref[...], w_ref[...], preferred_element_type=F32).astype(o_ref.dtype)

    @pl.when(pl.program_id(1) == 0)
    def _():
        g_ref[...] = jnp.dot(a_ref[...], wg_ref[...], preferred_element_type=F32)


def proj_matmul(a, w, wg, *, tm, tn):
    m, kdim = a.shape
    n, ng = w.shape[1], wg.shape[1]
    assert m % tm == 0 and n % tn == 0
    return pl.pallas_call(
        _proj_body,
        grid=(m // tm, n // tn),
        in_specs=[pl.BlockSpec((tm, kdim), lambda i, j: (i, 0), pipeline_mode=pl.Buffered(1)),
                  pl.BlockSpec((kdim, tn), lambda i, j: (0, j)),
                  pl.BlockSpec((kdim, ng), lambda i, j: (0, 0), pipeline_mode=pl.Buffered(1))],
        out_specs=[pl.BlockSpec((tm, tn), lambda i, j: (i, j)),
                   pl.BlockSpec((tm, ng), lambda i, j: (i, 0))],
        out_shape=[jax.ShapeDtypeStruct((m, n), BF16), jax.ShapeDtypeStruct((m, ng), F32)],
        compiler_params=_params("parallel", "arbitrary"),
        name="proj_matmul",
    )(a, w, wg)


def _norm_mm_body(x_ref, g_ref, w_ref, o_ref, a_ref):
    @pl.when(pl.program_id(1) == 0)
    def _():
        x = x_ref[...].astype(F32)
        y = x * lax.rsqrt(jnp.mean(x * x, axis=-1, keepdims=True) + RMS_EPS)
        a_ref[...] = (y * g_ref[...]).astype(BF16)

    o_ref[...] = jnp.dot(a_ref[...], w_ref[...].astype(BF16),
                         preferred_element_type=F32).astype(o_ref.dtype)


def norm_matmul(x, gain, w, *, col0, width, tm, tn, layer=None):
    m = x.shape[0]
    n = w.shape[-1]
    assert m % tm == 0 and n % tn == 0 and w.shape[-2] == width and col0 % width == 0
    return pl.pallas_call(
        _norm_mm_body,
        grid=(m // tm, n // tn),
        in_specs=[pl.BlockSpec((tm, width), lambda i, j: (i, col0 // width)),
                  pl.BlockSpec((1, width), lambda i, j: (0, 0)),
                  _wspec(w, layer, (width, tn), lambda i, j: (0, j))],
        out_specs=pl.BlockSpec((tm, tn), lambda i, j: (i, j)),
        out_shape=jax.ShapeDtypeStruct((m, n), BF16),
        scratch_shapes=[pltpu.VMEM((tm, width), BF16)],
        compiler_params=_params("parallel", "arbitrary"),
        name="norm_matmul",
    )(x, gain.reshape(1, width).astype(F32), w)


def _out_proj_body(a1_ref, a2_ref, a3_ref, w_ref, r_ref, o_ref, wb_ref, *, seq_total, tm):
    i = pl.program_id(1)

    @pl.when(i == 0)
    def _():
        wb_ref[...] = w_ref[...].astype(BF16)

    k1 = a1_ref.shape[1]
    k2 = k1 + a2_ref.shape[1]
    acc = jnp.dot(a1_ref[...], wb_ref[0:k1, :], preferred_element_type=F32)
    acc += jnp.dot(a2_ref[...], wb_ref[k1:k2, :], preferred_element_type=F32)
    acc += jnp.dot(a3_ref[...], wb_ref[k2:, :], preferred_element_type=F32)
    pos = _batch_row0(i, seq_total, tm) + lax.broadcasted_iota(jnp.int32, acc.shape, 0)
    o_ref[...] = r_ref[...] + jnp.where(pos >= N_PAD, acc, 0.0)


def out_proj(a1, a2, a3, w, res, *, seq_total, tm, tn, layer=None):
    m = a1.shape[0]
    kdim, n = w.shape[-2:]
    assert a1.shape[1] + a2.shape[1] + a3.shape[1] == kdim
    assert m % tm == 0 and n % tn == 0 and seq_total % tm == 0
    a_spec = lambda a: pl.BlockSpec((tm, a.shape[1]), lambda j, i: (i, 0))
    return pl.pallas_call(
        functools.partial(_out_proj_body, seq_total=seq_total, tm=tm),
        grid=(n // tn, m // tm),
        in_specs=[a_spec(a1), a_spec(a2), a_spec(a3),
                  _wspec(w, layer, (kdim, tn), lambda j, i: (0, j), pipeline_mode=pl.Buffered(1)),
                  pl.BlockSpec((tm, tn), lambda j, i: (i, j))],
        out_specs=pl.BlockSpec((tm, tn), lambda j, i: (i, j)),
        out_shape=jax.ShapeDtypeStruct((m, n), F32),
        scratch_shapes=[pltpu.VMEM((kdim, tn), BF16)],
        compiler_params=_params("parallel", "arbitrary"),
        name="out_proj",
    )(a1, a2, a3, w, res)


CONV_CHUNK = 32
UP_ROW_CHUNK = 128
UP_K_CHUNK = 1024


def _ffn_up_body(a_ref, wg_ref, wv_ref, cwg_ref, cwv_ref, cbg_ref, cbv_ref, o_ref,
                 wcat_ref, u0_ref, u1_ref, halo_ref, *, seq_total, tm, tc, nj, nsteps):
    s = pl.program_id(0)
    cur = jnp.minimum(s, nsteps - 1)
    prev = jnp.maximum(s - 1, 0)
    row0_cur = _batch_row0(cur // nj, seq_total, tm)
    row0_prev = _batch_row0(prev // nj, seq_total, tm)
    j_prev = prev % nj

    @pl.when(s == 0)
    def _():
        u1_ref[...] = jnp.zeros_like(u1_ref)
        halo_ref[...] = jnp.zeros_like(halo_ref)

    mc = _pick(tm, UP_ROW_CHUNK, CONV_CHUNK)
    kc = _pick(a_ref.shape[1], UP_K_CHUNK, LANES)
    n_dots = (tm // mc) * (a_ref.shape[1] // kc)
    gate_rows = list(range(0, tm, CONV_CHUNK))
    per_dot = -(-len(gate_rows) // n_dots)

    def step(uc, up):
        up[0:SUBLANES, :] = jnp.where(row0_prev > 0, halo_ref[j_prev], 0.0)
        halo_ref[j_prev] = up[tm:tm + SUBLANES, :]
        wcat_ref[:, 0:tc] = wg_ref[...].astype(BF16)
        wcat_ref[:, tc:2 * tc] = wv_ref[...].astype(BF16)
        cw = jnp.concatenate([cwg_ref[...], cwv_ref[...]], axis=1)
        cb = jnp.concatenate([cbg_ref[...], cbv_ref[...]], axis=1)

        def gate_stage(r0):
            cv = cb
            for tap in range(CONV_W):
                lo = SUBLANES + r0 - (CONV_W - 1 - tap)
                cv = cv + cw[tap:tap + 1, :] * up[lo:lo + CONV_CHUNK, :]
            o_ref[r0:r0 + CONV_CHUNK, :] = (_silu(cv[:, 0:tc]) * cv[:, tc:2 * tc]).astype(o_ref.dtype)

        todo = list(gate_rows)
        nk = a_ref.shape[1] // kc
        for m0 in range(0, tm, mc):
            rows = slice(SUBLANES + m0, SUBLANES + m0 + mc)
            for ki in range(nk):
                part = jnp.dot(a_ref[m0:m0 + mc, ki * kc:(ki + 1) * kc], wcat_ref[ki * kc:(ki + 1) * kc, :],
                               preferred_element_type=F32)
                if ki > 0:
                    part = uc[rows, :] + part
                if ki == nk - 1:
                    pos = row0_cur + m0 + lax.broadcasted_iota(jnp.int32, part.shape, 0)
                    part = jnp.where(pos >= N_PAD, part, 0.0)
                uc[rows, :] = part
                for r0 in todo[:per_dot]:
                    gate_stage(r0)
                todo = todo[per_dot:]
        assert not todo

    @pl.when(s % 2 == 0)
    def _():
        step(u0_ref, u1_ref)

    @pl.when(s % 2 == 1)
    def _():
        step(u1_ref, u0_ref)


def ffn_up(hn, w_up, conv_w, conv_b, *, seq_total, tm, tc, d_ff, layer=None):
    m, kdim = hn.shape
    assert m % tm == 0 and seq_total % tm == 0 and tm % CONV_CHUNK == 0 and d_ff % tc == 0
    nj = d_ff // tc
    nsteps = (m // tm) * nj
    cur = lambda s: jnp.minimum(s, nsteps - 1)
    prev = lambda s: jnp.maximum(s - 1, 0)
    wspec = lambda off: _wspec(w_up, layer, (kdim, tc), lambda s: (0, cur(s) % nj + off))
    cspec = lambda c, r, off: _wspec(c, layer, (r, tc), lambda s: (0, prev(s) % nj + off))
    conv_b = conv_b.reshape(conv_b.shape[:-1] + (1, conv_b.shape[-1]))
    return pl.pallas_call(
        functools.partial(_ffn_up_body, seq_total=seq_total, tm=tm, tc=tc, nj=nj, nsteps=nsteps),
        grid=(nsteps + 1,),
        in_specs=[pl.BlockSpec((tm, kdim), lambda s: (cur(s) // nj, 0), pipeline_mode=pl.Buffered(1)),
                  wspec(0), wspec(nj), cspec(conv_w, CONV_W, 0), cspec(conv_w, CONV_W, nj),
                  cspec(conv_b, 1, 0), cspec(conv_b, 1, nj)],
        out_specs=pl.BlockSpec((tm, tc), lambda s: (prev(s) // nj, prev(s) % nj)),
        out_shape=jax.ShapeDtypeStruct((m, d_ff), BF16),
        scratch_shapes=[pltpu.VMEM((kdim, 2 * tc), BF16),
                        pltpu.VMEM((tm + SUBLANES, 2 * tc), F32),
                        pltpu.VMEM((tm + SUBLANES, 2 * tc), F32),
                        pltpu.VMEM((nj, SUBLANES, 2 * tc), F32)],
        compiler_params=_params("arbitrary"),
        name="ffn_up",
    )(hn, w_up, w_up, conv_w, conv_w, conv_b, conv_b)


def _ffn_down_body(a_ref, w_ref, r_ref, o_ref, *, seq_total, tm, tk, d_ff, nk):
    k = pl.program_id(2)

    def part(a, w):
        return jnp.dot(a, w.astype(BF16), preferred_element_type=F32)

    @pl.when(k == 0)
    def _():
        o_ref[...] = part(a_ref[...], w_ref[...])

    @pl.when(jnp.logical_and(k > 0, k < nk - 1))
    def _():
        o_ref[...] += part(a_ref[...], w_ref[...])

    @pl.when(k == nk - 1)
    def _():
        acol = k * tk + lax.broadcasted_iota(jnp.int32, a_ref.shape, 1)
        a = jnp.where(acol < d_ff, a_ref[...], jnp.zeros(a_ref.shape, a_ref.dtype))
        wrow = k * tk + lax.broadcasted_iota(jnp.int32, w_ref.shape, 0)
        w = jnp.where(wrow < d_ff, w_ref[...], 0.0)
        pos = _batch_row0(pl.program_id(0), seq_total, tm) + lax.broadcasted_iota(
            jnp.int32, o_ref.shape, 0)
        o_ref[...] = r_ref[...] + jnp.where(pos >= N_PAD, o_ref[...] + part(a, w), 0.0)


def ffn_down(act, w, res, *, seq_total, tm, tn, tk, layer=None):
    m, d_ff = act.shape
    n = w.shape[-1]
    assert m % tm == 0 and n % tn == 0 and seq_total % tm == 0 and w.shape[-2] == d_ff
    nk = pl.cdiv(d_ff, tk)
    assert nk >= 2
    return pl.pallas_call(
        functools.partial(_ffn_down_body, seq_total=seq_total, tm=tm, tk=tk, d_ff=d_ff, nk=nk),
        grid=(m // tm, n // tn, nk),
        in_specs=[pl.BlockSpec((tm, tk), lambda i, j, k: (i, k)),
                  _wspec(w, layer, (tk, tn), lambda i, j, k: (k, j)),
                  pl.BlockSpec((tm, tn), lambda i, j, k: (i, j))],
        out_specs=pl.BlockSpec((tm, tn), lambda i, j, k: (i, j)),
        out_shape=jax.ShapeDtypeStruct((m, n), F32),
        compiler_params=_params("parallel", "parallel", "arbitrary"),
        name="ffn_down",
    )(act, w, res)


def _scores(k_rows, q_parts):
    out = []
    for q in q_parts:
        s = jnp.dot(k_rows, q, preferred_element_type=F32)
        out.append((s, jnp.max(s, axis=0, keepdims=True)))
    return tuple(out)


def _causal(scored, col0s):
    out = []
    for (s, _), c0 in zip(scored, col0s):
        s = s[0:c0 + s.shape[1], :]
        key = lax.broadcasted_iota(jnp.int32, s.shape, 0)
        qry = c0 + lax.broadcasted_iota(jnp.int32, s.shape, 1)
        s = jnp.where(key <= qry, s, NEG_INF)
        out.append((s, jnp.max(s, axis=0, keepdims=True)))
    return tuple(out)


def _softmax_pv(carry, scored, v_ext):
    out = []
    for (m, acc), (s, s_max) in zip(carry, scored):
        m_new = jnp.maximum(m, s_max)
        alpha = jnp.exp2(m - m_new)
        p = jnp.exp2(s - m_new).astype(BF16)
        v = v_ext[:, 0:s.shape[0]]
        out.append((m_new, alpha * acc + jnp.dot(v, p, preferred_element_type=F32)))
    return tuple(out)


def _attn_init(widths):
    return tuple((jnp.full((1, w), NEG_INF, F32), jnp.zeros((V_ROWS, w), F32)) for w in widths)


def _attn_finish(carry, gain, o_ref, row0, col0s):
    for (_, acc), c0 in zip(carry, col0s):
        o = jnp.transpose(acc[0:LANES, :] / acc[LANES:LANES + 1, :])
        o = o * lax.rsqrt(jnp.mean(o * o, axis=-1, keepdims=True) + RMS_EPS) * gain
        o_ref[pl.ds(row0 + c0, o.shape[0]), :] = o.astype(o_ref.dtype)


def _query_parts(tb):
    part = QUERY_PART if tb % QUERY_PART == 0 else tb
    return part, tuple(range(0, tb, part))


def _attention_core(qt, qt0, ks, vt, vt0, s_scr, m_scr, g_ref, o_ref, *, nblk, tb):
    heads = range(qt.shape[0])
    gains = [g_ref[:, h * LANES:(h + 1) * LANES] for h in heads]
    outs = [o_ref.at[:, pl.ds(h * LANES, LANES)] for h in heads]

    for h in heads:
        lead = _causal(_scores(ks[h, 0:LEAD, :], (qt0[h],)), (0,))
        _attn_finish(_softmax_pv(_attn_init((LEAD,)), lead, vt0[h]), gains[h], outs[h], 0, (0,))

    part, col0s = _query_parts(tb)
    widths = (part,) * len(col0s)

    def q_block(qi, _):
        q_parts = [tuple(qt[h, qi, :, c0:c0 + part] for c0 in col0s) for h in heads]

        def score(slot, kj):
            for h in heads:
                k_rows = ks[h, pl.ds(pl.multiple_of(LEAD + kj * tb, LANES), tb), :]
                for p, (s, s_max) in enumerate(_scores(k_rows, q_parts[h])):
                    s_scr[h, slot, p] = s
                    m_scr[h, slot, p] = s_max

        def scored(h, slot):
            return tuple((s_scr[h, slot, p], m_scr[h, slot, p]) for p in range(len(col0s)))

        def consume(carry, slot, kj, causal):
            out = []
            for h in heads:
                sc = scored(h, slot)
                if causal:
                    sc = _causal(sc, col0s)
                out.append(_softmax_pv(carry[h], sc, vt[h, kj]))
            return tuple(out)

        carry = tuple(_softmax_pv(_attn_init(widths), _scores(ks[h, 0:LEAD, :], q_parts[h]), vt0[h])
                      for h in heads)
        score(0, 0)

        def pair(t, carry):
            kj = 2 * t
            score(1, kj + 1)
            carry = consume(carry, 0, kj, False)
            score(0, kj + 2)
            return consume(carry, 1, kj + 1, False)

        carry = lax.fori_loop(0, qi // 2, pair, carry)

        def odd_tail(carry):
            score(1, qi)
            return consume(consume(carry, 0, qi - 1, False), 1, qi, True)

        def even_tail(carry):
            return consume(carry, 0, qi, True)

        carry = lax.cond(qi % 2 == 1, odd_tail, even_tail, carry)
        for h in heads:
            _attn_finish(carry[h], gains[h], outs[h], pl.multiple_of(LEAD + qi * tb, LANES), col0s)
        return 0

    lax.fori_loop(0, nblk, q_block, 0)


def _store_v_ext(vt_dst, v_rows):
    n = v_rows.shape[0]
    vt_dst[0:LANES, :] = _to_t(v_rows)
    r = lax.broadcasted_iota(jnp.int32, (V_ROWS - LANES, n), 0)
    vt_dst[LANES:V_ROWS, :] = jnp.where(r == 0, 1.0, 0.0).astype(BF16)


def _to_t(x):
    return jnp.transpose(x.astype(F32)).astype(BF16)


def _mla_attn_body(q_ref, kv_ref, kr_ref, cos_ref, sa_ref, sb_ref, g_ref, o_ref,
                   qt, qt0, ks, vt, vt0, s_scr, m_scr, *, nblk, tb):
    scale = (MLA_NOPE + MLA_ROPE) ** -0.5 * LOG2E
    heads = range(qt.shape[0])

    def rope(x, rows):
        return (x * cos_ref[rows, :] + pltpu.roll(x, HALF_ROPE, 1) * sa_ref[rows, :]
                + pltpu.roll(x, LANES - HALF_ROPE, 1) * sb_ref[rows, :])

    def prep(r0, n, qt_dst, vt_dst):
        rows = pl.ds(r0, n)
        one = lax.broadcasted_iota(jnp.int32, (n, LANES), 1) == ONE_LANE
        pos = r0 + lax.broadcasted_iota(jnp.int32, (n, LANES), 0)
        kp = rope(kr_ref[rows, :].astype(F32), rows)
        kp = jnp.where(one, jnp.where(pos < N_PAD, NEG_INF, 0.0), kp).astype(BF16)
        for h in heads:
            c0 = h * 2 * LANES
            qn = q_ref[rows, pl.ds(c0, LANES)].astype(F32) * scale
            qp = rope(q_ref[rows, pl.ds(c0 + LANES, LANES)].astype(F32), rows) * scale
            qp = jnp.where(one, 1.0, qp)
            qt_dst(h)[0:LANES, :] = jnp.transpose(qn).astype(BF16)
            qt_dst(h)[LANES:2 * LANES, :] = jnp.transpose(qp).astype(BF16)
            ks[h, rows, pl.ds(0, LANES)] = kv_ref[rows, pl.ds(c0, LANES)]
            ks[h, rows, pl.ds(LANES, LANES)] = kp
            _store_v_ext(vt_dst(h), kv_ref[rows, pl.ds(c0 + LANES, LANES)])

    prep(0, LEAD, lambda h: qt0.at[h], lambda h: vt0.at[h])

    def prep_block(i, _):
        prep(pl.multiple_of(LEAD + i * tb, LANES), tb, lambda h: qt.at[h, i], lambda h: vt.at[h, i])
        return 0

    lax.fori_loop(0, nblk, prep_block, 0)
    _attention_core(qt, qt0, ks, vt, vt0, s_scr, m_scr, g_ref, o_ref, nblk=nblk, tb=tb)


def _attn_scratch(heads, seq_total, nblk, tb):
    part, col0s = _query_parts(tb)
    return [pltpu.VMEM((heads, nblk, 2 * LANES, tb), BF16), pltpu.VMEM((heads, 2 * LANES, LEAD), BF16),
            pltpu.VMEM((heads, seq_total, 2 * LANES), BF16),
            pltpu.VMEM((heads, nblk, V_ROWS, tb), BF16), pltpu.VMEM((heads, V_ROWS, LEAD), BF16),
            pltpu.VMEM((heads, 2, len(col0s), tb, part), F32),
            pltpu.VMEM((heads, 2, len(col0s), 1, part), F32)]


def mla_attention(q_raw, kv, proj, cos_t, sin_a, sin_b, gain, *, batch, seq_total, tb):
    t = q_raw.shape[0]
    nblk = (seq_total - LEAD) // tb
    hps = ATTN_HEADS_PER_STEP
    assert MLA_HEADS % hps == 0
    big = lambda w: pl.BlockSpec((seq_total, hps * w), lambda b, h: (b, h))
    table = pl.BlockSpec((seq_total, LANES), lambda b, h: (0, 0), pipeline_mode=pl.Buffered(1))
    return pl.pallas_call(
        functools.partial(_mla_attn_body, nblk=nblk, tb=tb),
        grid=(batch, MLA_HEADS // hps),
        in_specs=[big(2 * LANES), big(2 * LANES),
                  pl.BlockSpec((seq_total, LANES), lambda b, h: (b, COL_KR // LANES)),
                  table, table, table,
                  pl.BlockSpec((1, hps * LANES), lambda b, h: (0, h))],
        out_specs=big(LANES),
        out_shape=jax.ShapeDtypeStruct((t, W_MLA), BF16),
        scratch_shapes=_attn_scratch(hps, seq_total, nblk, tb),
        compiler_params=_params("parallel", "parallel"),
        name="mla_attention",
    )(q_raw, kv, proj, cos_t, sin_a, sin_b, gain.reshape(1, W_MLA).astype(F32))


def _fox_attn_body(q_ref, k_ref, v_ref, aq_ref, ak_ref, g_ref, o_ref,
                   qt, qt0, ks, vt, vt0, s_scr, m_scr, *, nblk, tb):
    scale = FOX_DH ** -0.5 * LOG2E
    heads = range(qt.shape[0])

    def prep(r0, n, qt_dst, vt_dst):
        rows = pl.ds(r0, n)
        for h in heads:
            cols = pl.ds(h * LANES, LANES)
            qt_dst(h)[0:LANES, :] = jnp.transpose(q_ref[rows, cols].astype(F32) * scale).astype(BF16)
            qt_dst(h)[LANES:2 * LANES, :] = _to_t(aq_ref[rows, cols])
            ks[h, rows, pl.ds(0, LANES)] = k_ref[rows, cols]
            ks[h, rows, pl.ds(LANES, LANES)] = ak_ref[rows, cols]
            _store_v_ext(vt_dst(h), v_ref[rows, cols])

    prep(0, LEAD, lambda h: qt0.at[h], lambda h: vt0.at[h])

    def prep_block(i, _):
        prep(pl.multiple_of(LEAD + i * tb, LANES), tb, lambda h: qt.at[h, i], lambda h: vt.at[h, i])
        return 0

    lax.fori_loop(0, nblk, prep_block, 0)
    _attention_core(qt, qt0, ks, vt, vt0, s_scr, m_scr, g_ref, o_ref, nblk=nblk, tb=tb)


def fox_attention(proj, aug_q, aug_k, gain, *, batch, seq_total, tb):
    t = proj.shape[0]
    nblk = (seq_total - LEAD) // tb
    hps = ATTN_HEADS_PER_STEP
    assert FOX_HEADS % hps == 0
    wb = hps * LANES
    col = lambda c0: pl.BlockSpec((seq_total, wb), lambda b, h: (b, c0 // wb + h))
    assert COL_FQ % wb == 0 and COL_FK % wb == 0 and COL_FV % wb == 0
    return pl.pallas_call(
        functools.partial(_fox_attn_body, nblk=nblk, tb=tb),
        grid=(batch, FOX_HEADS // hps),
        in_specs=[col(COL_FQ), col(COL_FK), col(COL_FV), col(0), col(0),
                  pl.BlockSpec((1, wb), lambda b, h: (0, h))],
        out_specs=col(0),
        out_shape=jax.ShapeDtypeStruct((t, W_FOX), BF16),
        scratch_shapes=_attn_scratch(hps, seq_total, nblk, tb),
        compiler_params=_params("parallel", "parallel"),
        name="fox_attention",
    )(proj, proj, proj, aug_q, aug_k, gain.reshape(1, W_FOX).astype(F32))


def _split3(x):
    hi = x.astype(BF16)
    r1 = x - hi.astype(F32)
    mid = r1.astype(BF16)
    lo = (r1 - mid.astype(F32)).astype(BF16)
    return hi.astype(F32), mid.astype(F32), lo.astype(F32)


def _fox_prep_body(z_ref, b_ref, aq_ref, ak_ref, carry_ref, *, tr):
    i = pl.program_id(1)

    @pl.when(i == 0)
    def _():
        carry_ref[...] = jnp.zeros_like(carry_ref)

    row = i * tr + lax.broadcasted_iota(jnp.int32, (tr, LANES), 0)
    valid = row >= N_PAD
    log_f = jnp.where(valid, _log_sigmoid(z_ref[...] + b_ref[...]), 0.0)
    r = lax.broadcasted_iota(jnp.int32, (tr, tr), 0)
    c = lax.broadcasted_iota(jnp.int32, (tr, tr), 1)
    tri = jnp.where(c <= r, 1.0, 0.0).astype(F32)
    csum = jnp.dot(tri, log_f, preferred_element_type=F32,
                   precision=lax.Precision.HIGHEST) + carry_ref[0:1, :]
    carry_ref[...] = jnp.broadcast_to(csum[tr - 1:tr, :], carry_ref.shape)

    lane = lax.broadcasted_iota(jnp.int32, (tr, LANES), 1)
    key_mask = jnp.where(valid, 0.0, NEG_INF)
    for h in range(FOX_HEADS):
        col = csum[:, GATE_FZ + h:GATE_FZ + h + 1] * LOG2E
        hi, mid, lo = _split3(col)
        aq = jnp.where(lane == 0, hi, jnp.where(lane == 1, mid, jnp.where(lane == 2, lo,
             jnp.where(lane < 7, 1.0, 0.0))))
        ak = jnp.where(lane < 3, 1.0, jnp.where(lane == 3, -hi, jnp.where(lane == 4, -mid,
             jnp.where(lane == 5, -lo, jnp.where(lane == 6, key_mask, 0.0)))))
        aq_ref[:, h * LANES:(h + 1) * LANES] = aq.astype(BF16)
        ak_ref[:, h * LANES:(h + 1) * LANES] = ak.astype(BF16)


def fox_prep(gates, fox_b, *, batch, seq_total):
    t = gates.shape[0]
    tr = _pick(seq_total, 384, LANES)
    nblk = seq_total // tr
    bias = jnp.zeros((1, LANES), F32).at[0, GATE_FZ:GATE_FZ + FOX_HEADS].set(fox_b.astype(F32))
    out = jax.ShapeDtypeStruct((t, W_FOX), BF16)
    return pl.pallas_call(
        functools.partial(_fox_prep_body, tr=tr),
        grid=(batch, nblk),
        in_specs=[pl.BlockSpec((tr, LANES), lambda b, i: (b * nblk + i, 0)),
                  pl.BlockSpec((1, LANES), lambda b, i: (0, 0))],
        out_specs=[pl.BlockSpec((tr, W_FOX), lambda b, i: (b * nblk + i, 0))] * 2,
        out_shape=[out, out],
        scratch_shapes=[pltpu.VMEM((SUBLANES, LANES), F32)],
        compiler_params=_params("parallel", "arbitrary"),
        name="fox_prep",
    )(gates, bias)


def _gla_body(q_ref, k_ref, v_ref, r_ref, z_ref, w2_ref, b2_ref, g_ref, o_ref, s_ref,
              *, seq_total):
    c = GLA_CHUNK
    nchunk = seq_total // c
    gsz = max(g for g in range(1, GLA_GROUP + 1) if nchunk % g == 0)
    rg = gsz * c
    heads = s_ref.shape[0]
    nt = (((1,), (1,)), ((), ()))
    tn = (((0,), (0,)), ((), ()))
    r_i = lax.broadcasted_iota(jnp.int32, (rg, rg), 0)
    c_i = lax.broadcasted_iota(jnp.int32, (rg, rg), 1)
    assert c & (c - 1) == 0
    lower = jnp.logical_and(c_i <= r_i, c_i >= jnp.bitwise_and(r_i, -c))
    tri = jnp.where(lower, 1.0, 0.0).astype(F32)
    s_ref[...] = jnp.zeros_like(s_ref)

    def group(i, _):
        r0 = pl.multiple_of(i * rg, rg)
        rows = pl.ds(r0, rg)
        valid = (r0 + lax.broadcasted_iota(jnp.int32, (rg, GLA_DK), 0)) >= N_PAD
        z = z_ref[rows, :].astype(BF16)
        for hh in range(heads):
            kcols = pl.ds(hh * GLA_DK, GLA_DK)
            vcols = pl.ds(hh * GLA_DV, GLA_DV)
            q = q_ref[rows, kcols].astype(F32) * (GLA_DK ** -0.5)
            k = jnp.where(valid, k_ref[rows, kcols].astype(F32), 0.0)
            v = v_ref[rows, vcols]
            logit = jnp.dot(z, w2_ref[:, kcols], preferred_element_type=F32) + b2_ref[:, kcols]
            log_a = jnp.where(valid, _log_sigmoid(logit) / GLA_TAU, 0.0)
            bc = jnp.dot(tri, log_a, preferred_element_type=F32, precision=lax.Precision.HIGHEST)
            b_last = jnp.concatenate(
                [jnp.broadcast_to(bc[(n + 1) * c - 1:(n + 1) * c, :], (c, GLA_DK)) for n in range(gsz)], axis=0)
            q_dec = (q * jnp.exp(bc)).astype(BF16)
            k_dec = (k * jnp.exp(-bc)).astype(BF16)
            k_state = (k * jnp.exp(b_last - bc)).astype(BF16)
            a = lax.dot_general(q_dec, k_dec, nt, preferred_element_type=F32)
            a = jnp.where(lower, a, 0.0).astype(BF16)
            o_intra = jnp.dot(a, v, preferred_element_type=F32)
            state = s_ref[hh]
            outs = []
            for n in range(gsz):
                sl = slice(n * c, (n + 1) * c)
                outs.append(o_intra[sl] + jnp.dot(q_dec[sl], state.astype(BF16),
                                                   preferred_element_type=F32))
                decay = jnp.exp(b_last[n * c:n * c + 1, :])
                dec_t = jnp.transpose(jnp.broadcast_to(decay, (GLA_DK, GLA_DK)))
                dec_t = jnp.concatenate([dec_t, dec_t], axis=1)
                state = state * dec_t + lax.dot_general(k_state[sl], v[sl], tn,
                                                        preferred_element_type=F32)
            s_ref[hh] = state
            o = jnp.concatenate(outs, axis=0)
            o = o * lax.rsqrt(jnp.mean(o * o, axis=-1, keepdims=True) + RMS_EPS) * g_ref[:, vcols]
            o_ref[rows, vcols] = (o * _silu(r_ref[rows, vcols].astype(F32))).astype(o_ref.dtype)
        return 0

    lax.fori_loop(0, nchunk // gsz, group, 0)


def gla(proj, gates, w2, b2, gain, *, batch, seq_total):
    t = proj.shape[0]
    hps = GLA_HEADS_PER_STEP
    assert GLA_HEADS % hps == 0
    wk, wv = hps * GLA_DK, hps * GLA_DV
    blk = lambda w, c0: pl.BlockSpec((seq_total, w), lambda b, h: (b, c0 // w + h))
    return pl.pallas_call(
        functools.partial(_gla_body, seq_total=seq_total),
        grid=(batch, GLA_HEADS // hps),
        in_specs=[blk(wk, COL_GQ), blk(wk, COL_GK), blk(wv, COL_GV), blk(wv, COL_GR),
                  pl.BlockSpec((seq_total, LANES), lambda b, h: (b, 0)),
                  pl.BlockSpec((LANES, wk), lambda b, h: (0, h)),
                  pl.BlockSpec((1, wk), lambda b, h: (0, h)),
                  pl.BlockSpec((1, wv), lambda b, h: (0, h))],
        out_specs=pl.BlockSpec((seq_total, wv), lambda b, h: (b, h)),
        out_shape=jax.ShapeDtypeStruct((t, W_GLA), BF16),
        scratch_shapes=[pltpu.VMEM((hps, GLA_DK, GLA_DV), F32)],
        compiler_params=_params("parallel", "parallel"),
        name="gla",
    )(proj, proj, proj, proj, gates, w2, b2, gain.reshape(1, W_GLA).astype(F32))


def _rope_tables(seq_total):
    pos = np.maximum(np.arange(seq_total) - N_PAD, 0).astype(np.float32)
    inv_freq = (1.0 / (ROPE_THETA ** (np.arange(0, MLA_ROPE, 2, dtype=np.float32) / MLA_ROPE))).astype(np.float32)
    ang = jnp.asarray(pos)[:, None] * jnp.asarray(inv_freq)[None, :]
    cos, sin = jnp.cos(ang), jnp.sin(ang)
    z = jnp.zeros_like(cos)
    cos_t = jnp.concatenate([cos, cos, z, z], axis=1)
    sin_a = jnp.concatenate([z, sin, z, z], axis=1)
    sin_b = jnp.concatenate([-sin, z, z, z], axis=1)
    return cos_t, sin_a, sin_b


def kernel(x, meta_tokens, attn_norm, w_in, mla_q_norm, mla_w_uq, mla_kv_norm, mla_w_ukv,
           gla_w_gate2, gla_b_gate, fox_b_f, out_norm_mla, out_norm_gla, out_norm_fox,
           w_out, ffn_norm, ffn_w_up, ffn_conv_w, ffn_conv_b, ffn_w_down, final_norm):
    batch, seq, d_model = x.shape
    depth = w_in.shape[0]
    d_ff = ffn_w_down.shape[1]
    seq_total = LEAD + seq
    cos_t, sin_a, sin_b = _rope_tables(seq_total)

    tm_big = _pick(seq_total, TM_BIG, LANES)
    tm_mid = _pick(seq_total, TM_MID, LANES)
    tb = _pick(seq, ATTN_BLOCK, LANES)
    tc = _pick(d_ff, 256, LANES)

    w_in_b = transpose_cast(jnp.transpose(w_in, (2, 0, 1)))

    h, hn = embed_norm(x, meta_tokens, attn_norm[0])
    for layer in range(depth):
        w_big, w_gates = relayout_cols(w_in_b, (W_IN_PLAN, W_GATE_PLAN), (N_PROJ, LANES), layer=layer)
        w_uq, = relayout_cols(mla_w_uq, (W_UQ_PLAN,), (MLA_HEADS * 2 * LANES,), layer=layer)
        if layer > 0:
            hn = rmsnorm(h, attn_norm[layer])
        proj, gates = proj_matmul(hn, w_big, w_gates, tm=tm_big, tn=PROJ_TN)

        q_raw = norm_matmul(proj, mla_q_norm[layer], w_uq, col0=COL_CQ, width=MLA_Q_LORA,
                            tm=tm_big, tn=1024)
        kv = norm_matmul(proj, mla_kv_norm[layer], mla_w_ukv, col0=COL_CKV, width=MLA_KV_LORA,
                         tm=tm_big, tn=1024, layer=layer)
        o_mla = mla_attention(q_raw, kv, proj, cos_t, sin_a, sin_b, out_norm_mla[layer],
                              batch=batch, seq_total=seq_total, tb=tb)

        w2 = jnp.zeros((LANES, W_GQK), F32).at[:GLA_GATE_RANK].set(gla_w_gate2[layer])
        o_gla = gla(proj, gates, w2.astype(BF16), gla_b_gate[layer].reshape(1, -1),
                    out_norm_gla[layer], batch=batch, seq_total=seq_total)

        aug_q, aug_k = fox_prep(gates, fox_b_f[layer], batch=batch, seq_total=seq_total)
        o_fox = fox_attention(proj, aug_q, aug_k, out_norm_fox[layer],
                              batch=batch, seq_total=seq_total, tb=tb)

        h = out_proj(o_mla, o_gla, o_fox, w_out, h, seq_total=seq_total, tm=tm_mid, tn=1024,
                     layer=layer)

        hn = rmsnorm(h, ffn_norm[layer])
        act = ffn_up(hn, ffn_w_up, ffn_conv_w, ffn_conv_b, seq_total=seq_total, tm=tm_big,
                     tc=tc, d_ff=d_ff, layer=layer)
        h = ffn_down(act, ffn_w_down, h, seq_total=seq_total, tm=tm_big, tn=1024, tk=DOWN_TK,
                     layer=layer)

    return final_rmsnorm(h, final_norm, batch, seq_total)
```
